```python
import math
import jax, jax.numpy as jnp
from jax import lax
import numpy as np

D_MODEL = 1024
BATCH = 4
SEQ = 8192
DEPTH = 1

N_MOD = 6
GLA_HEADS = 4
GLA_DK = D_MODEL // 16
GLA_DV = D_MODEL // 8
GLA_WIDTH = GLA_HEADS * GLA_DV
GLA_LOWRANK = 16
GLA_GATE_NORM = 16.0
GLA_CHUNK = 64
FOX_HEADS = 8
FOX_DH = D_MODEL // 16
FOX_WIDTH = FOX_HEADS * FOX_DH
FOX_QBLOCK = 128
MIX_WIDTH = GLA_WIDTH + FOX_WIDTH
SPLIT_SIZES = (GLA_HEADS * GLA_DK, GLA_HEADS * GLA_DK, GLA_WIDTH, GLA_WIDTH, GLA_LOWRANK,
               FOX_WIDTH, FOX_WIDTH, FOX_WIDTH, FOX_HEADS)
IN_COLS = sum(SPLIT_SIZES)
N_EXPERTS = 32
TOP_K = 4
D_FF = D_MODEL
SWIGLU_ALPHA = 1.702
SWIGLU_LIMIT = 7.0
MOE_BLOCK = 128
EPS = 1e-6

kernel_name = "hymba_gla_fox_moe_sandwich_adaln"


def rmsnorm(x, g):
    xf = x.astype(jnp.float32)
    y = xf * lax.rsqrt(jnp.mean(xf * xf, axis=-1, keepdims=True) + EPS)
    return (y * g.astype(jnp.float32)).astype(x.dtype)


def gla_chunked(q, k, v, log_a):
    B, S, H, K = q.shape
    V = v.shape[-1]
    N = S // GLA_CHUNK
    qc = q.reshape(B, N, GLA_CHUNK, H, K).astype(jnp.float32)
    kc = k.reshape(B, N, GLA_CHUNK, H, K).astype(jnp.float32)
    vc = v.reshape(B, N, GLA_CHUNK, H, V)
    b = jnp.cumsum(log_a.reshape(B, N, GLA_CHUNK, H, K), axis=2)
    b_last = b[:, :, -1:]
    q_t = qc * jnp.exp(b) * (K ** -0.5)
    k_t = kc * jnp.exp(-b)
    k_d = kc * jnp.exp(b_last - b)
    causal = jnp.tril(jnp.ones((GLA_CHUNK, GLA_CHUNK), dtype=bool))
    att = jnp.einsum('bnihk,bnjhk->bnhij', q_t, k_t)
    att = jnp.where(causal, att, 0.0)
    o_intra = jnp.einsum('bnhij,bnjhv->bnihv', att, vc)
    kv = jnp.einsum('bnjhk,bnjhv->bnhkv', k_d, vc)
    decay = jnp.exp(b_last[:, :, 0])

    def step(state, inp):
        dec, kv_n = inp
        return dec[..., None] * state + kv_n, state

    init = jnp.zeros((B, H, K, V), jnp.float32)
    _, s_prev = lax.scan(step, init, (jnp.moveaxis(decay, 1, 0), jnp.moveaxis(kv, 1, 0)))
    s_prev = jnp.moveaxis(s_prev, 0, 1)
    o_inter = jnp.einsum('bnihk,bnhkv->bnihv', q_t, s_prev)
    return (o_intra + o_inter).reshape(B, S, H, V)


def forgetting_attention(q, k, v, log_f):
    B, S, H, D = q.shape
    Ft = jnp.moveaxis(jnp.cumsum(log_f.astype(jnp.float32), axis=1), 1, 2)
    scale = D ** -0.5
    outs = []
    for i in range(S // FOX_QBLOCK):
        lo, hi = i * FOX_QBLOCK, (i + 1) * FOX_QBLOCK
        s = jnp.einsum('bqhd,bkhd->bhqk', q[:, lo:hi], k[:, :hi]).astype(jnp.float32) * scale
        s = s + (Ft[:, :, lo:hi, None] - Ft[:, :, None, :hi])
        mask = jnp.arange(hi)[None, :] <= (lo + jnp.arange(FOX_QBLOCK))[:, None]
        s = jnp.where(mask, s, -jnp.inf)
        p = jax.nn.softmax(s, axis=-1)
        outs.append(jnp.einsum('bhqk,bkhd->bqhd', p.astype(v.dtype), v[:, :hi]))
    return jnp.concatenate(outs, axis=1)


def token_mixer(h, w_in, w_gk2, b_gk, gla_norm_gain, b_f, w_o):
    B, S, _ = h.shape
    proj = h @ w_in
    idx = list(np.cumsum(SPLIT_SIZES)[:-1])
    qg, kg, vg, gg, zg, qf, kf, vf, ff = jnp.split(proj, idx, axis=-1)
    gk = zg @ w_gk2 + b_gk
    log_a = jax.nn.log_sigmoid(gk.astype(jnp.float32)) / GLA_GATE_NORM
    o_gla = gla_chunked(qg.reshape(B, S, GLA_HEADS, GLA_DK),
                        kg.reshape(B, S, GLA_HEADS, GLA_DK),
                        vg.reshape(B, S, GLA_HEADS, GLA_DV),
                        log_a.reshape(B, S, GLA_HEADS, GLA_DK))
    o_gla = rmsnorm(o_gla, gla_norm_gain.reshape(GLA_HEADS, GLA_DV)).reshape(B, S, GLA_WIDTH)
    o_gla = (o_gla * jax.nn.silu(gg)).astype(h.dtype)
    log_f = jax.nn.log_sigmoid((ff + b_f).astype(jnp.float32))
    o_fox = forgetting_attention(qf.reshape(B, S, FOX_HEADS, FOX_DH),
                                 kf.reshape(B, S, FOX_HEADS, FOX_DH),
                                 vf.reshape(B, S, FOX_HEADS, FOX_DH), log_f)
    o_fox = o_fox.reshape(B, S, FOX_WIDTH).astype(h.dtype)
    return jnp.concatenate([o_gla, o_fox], axis=-1) @ w_o


def moe(h2d, w_router, b_router, w_e_in, b_e_in, w_e_out, b_e_out):
    T, D = h2d.shape
    logits = (h2d @ w_router + b_router).astype(jnp.float32)
    top_vals, top_idx = lax.top_k(logits, TOP_K)
    gates = jax.nn.softmax(top_vals, axis=-1)
    TK = T * TOP_K
    e_flat = top_idx.reshape(TK).astype(jnp.int32)
    g_flat = gates.reshape(TK)
    tok_flat = jnp.arange(TK, dtype=jnp.int32) // TOP_K
    order = jnp.argsort(e_flat)
    e_sorted, tok_sorted, g_sorted = e_flat[order], tok_flat[order], g_flat[order]
    counts = jnp.bincount(e_flat, length=N_EXPERTS)
    padded = ((counts + MOE_BLOCK - 1) // MOE_BLOCK) * MOE_BLOCK
    start = jnp.cumsum(counts) - counts
    pend = jnp.cumsum(padded)
    pstart = pend - padded
    dest = pstart[e_sorted] + (jnp.arange(TK, dtype=jnp.int32) - start[e_sorted])
    n_blocks = -(-TK // MOE_BLOCK) + N_EXPERTS
    P = n_blocks * MOE_BLOCK
    slot_tok = jnp.full((P,), T, jnp.int32).at[dest].set(tok_sorted)
    block_expert = jnp.minimum(
        jnp.searchsorted(pend, jnp.arange(n_blocks) * MOE_BLOCK, side='right'), N_EXPERTS - 1)
    h_pad = jnp.concatenate([h2d, jnp.zeros((1, D), h2d.dtype)], axis=0)
    xb = h_pad[slot_tok].reshape(n_blocks, MOE_BLOCK, D)

    def expert_block(args):
        xblk, e = args
        u = xblk @ w_e_in[e] + b_e_in[e]
        glu, lin = u[:, :D_FF], u[:, D_FF:]
        glu = jnp.minimum(glu, SWIGLU_LIMIT)
        lin = jnp.clip(lin, -SWIGLU_LIMIT, SWIGLU_LIMIT)
        a = glu * jax.nn.sigmoid(SWIGLU_ALPHA * glu) * (lin + 1.0)
        return a @ w_e_out[e] + b_e_out[e]

    yb = lax.map(expert_block, (xb, block_expert)).reshape(P, D)
    y_slot = yb[dest] * g_sorted[:, None].astype(yb.dtype)
    return jax.ops.segment_sum(y_slot, tok_sorted, num_segments=T)


def setup_inputs(seed: int = 0) -> dict:
    key = jax.random.key(seed)
    ks = jax.random.split(key, 24)
    f32 = jnp.float32
    n = lambda k, shape, s: jax.random.normal(k, shape, f32) * s
    D = D_MODEL
    return {
        "x": n(ks[0], (BATCH, SEQ, D), 1.0),
        "c": n(ks[1], (BATCH, D), 1.0),
        "w_ada": n(ks[2], (D, N_MOD * D), 0.5 * D ** -0.5),
        "b_ada": n(ks[3], (N_MOD * D,), 0.1),
        "attn_pre_gain": 1.0 + n(ks[4], (D,), 0.02),
        "attn_post_gain": 1.0 + n(ks[5], (D,), 0.02),
        "w_in": n(ks[6], (D, IN_COLS), D ** -0.5),
        "w_gk2": n(ks[7], (GLA_LOWRANK, GLA_HEADS * GLA_DK), GLA_LOWRANK ** -0.5),
        "b_gk": n(ks[8], (GLA_HEADS * GLA_DK,), 0.1),
        "gla_norm_gain": 1.0 + n(ks[9], (GLA_WIDTH,), 0.02),
        "b_f": 3.0 + n(ks[10], (FOX_HEADS,), 0.5),
        "w_o": n(ks[11], (MIX_WIDTH, D), MIX_WIDTH ** -0.5),
        "mlp_pre_gain": 1.0 + n(ks[12], (D,), 0.02),
        "mlp_post_gain": 1.0 + n(ks[13], (D,), 0.02),
        "w_router": n(ks[14], (D, N_EXPERTS), D ** -0.5),
        "b_router": n(ks[15], (N_EXPERTS,), 0.01),
        "w_e_in": n(ks[16], (N_EXPERTS, D, 2 * D_FF), D ** -0.5),
        "b_e_in": n(ks[17], (N_EXPERTS, 2 * D_FF), 0.01),
        "w_e_out": n(ks[18], (N_EXPERTS, D_FF, D), D_FF ** -0.5),
        "b_e_out": n(ks[19], (N_EXPERTS, D), 0.01),
    }


def reference(x, c, w_ada, b_ada, attn_pre_gain, attn_post_gain, w_in, w_gk2, b_gk,
              gla_norm_gain, b_f, w_o, mlp_pre_gain, mlp_post_gain, w_router, b_router,
              w_e_in, b_e_in, w_e_out, b_e_out):
    B, S, D = x.shape
    mod = jax.nn.silu(c) @ w_ada + b_ada
    shift_a, scale_a, gate_a, shift_m, scale_m, gate_m = [m[:, None, :] for m in jnp.split(mod, N_MOD, axis=-1)]
    for _ in range(DEPTH):
        h = rmsnorm(x, attn_pre_gain) * (1.0 + scale_a) + shift_a
        y = token_mixer(h, w_in, w_gk2, b_gk, gla_norm_gain, b_f, w_o)
        x = x + gate_a * rmsnorm(y, attn_post_gain)
        h = rmsnorm(x, mlp_pre_gain) * (1.0 + scale_m) + shift_m
        y = moe(h.reshape(B * S, D), w_router, b_router, w_e_in, b_e_in, w_e_out, b_e_out).reshape(B, S, D)
        x = x + gate_m * rmsnorm(y, mlp_post_gain)
    return x
```

```python
import functools

import jax
import jax.numpy as jnp
from jax import lax
from jax.experimental import pallas as pl
from jax.experimental.pallas import tpu as pltpu

F32 = jnp.float32
BF16 = jnp.bfloat16

D_MODEL = 1024
N_MOD = 6
GLA_HEADS = 4
GLA_DK = 64
GLA_DV = 128
GLA_QK = GLA_HEADS * GLA_DK
GLA_WIDTH = GLA_HEADS * GLA_DV
GLA_LOWRANK = 16
GLA_CHUNK = 64
FOX_HEADS = 8
FOX_DH = 64
FOX_WIDTH = FOX_HEADS * FOX_DH
N_EXPERTS = 32
TOP_K = 4
D_FF = D_MODEL
SWIGLU_ALPHA = 1.702
SWIGLU_LIMIT = 7.0
EPS = 1e-6

LANES = 128
SUBLANES = 8
ROW_TILES = D_MODEL // LANES

COL_QG, COL_KG, COL_VG, COL_GG = 0, 256, 512, 1024
COL_QF, COL_KF, COL_VF = 1536, 2048, 2560
MAIN_COLS = 3072
SMALL_FF, SMALL_ZG = 0, 8

NEG = -1e30
VMEM_LIMIT = 56 * 1024 * 1024


def _dot(a, b):
    return jnp.dot(a, b, preferred_element_type=F32)


def _dot_nt(a, b):
    return lax.dot_general(a, b, (((1,), (1,)), ((), ())), preferred_element_type=F32)


def _dot_tn(a, b):
    return lax.dot_general(a, b, (((0,), (0,)), ((), ())), preferred_element_type=F32)


def _split2(a):
    hi = a.astype(BF16)
    lo = (a - hi.astype(F32)).astype(BF16)
    return hi, lo


def _split3(a):
    hi = a.astype(BF16)
    r = a - hi.astype(F32)
    mid = r.astype(BF16)
    lo = (r - mid.astype(F32)).astype(BF16)
    return hi, mid, lo


def _log_sigmoid(x):
    return jnp.minimum(x, 0.0) - jnp.log1p(jnp.exp(-jnp.abs(x)))


def _rms(x):
    return x * lax.rsqrt(jnp.mean(x * x, axis=-1, keepdims=True) + EPS)


def _params(*sem):
    return pltpu.CompilerParams(dimension_semantics=sem, vmem_limit_bytes=VMEM_LIMIT)


def _mod_kernel(c_ref, wh_ref, wl_ref, b_ref, o_ref):
    c = c_ref[...]
    s = c * jax.nn.sigmoid(c)
    sh, sl = _split2(s)
    o_ref[...] = _dot(sh, wh_ref[...]) + (_dot(sh, wl_ref[...]) + _dot(sl, wh_ref[...])) + b_ref[...]


def _mod_call(c_pad, w_hi, w_lo, b_ada):
    rows, d = c_pad.shape
    n = w_hi.shape[1]
    tn = 1024
    return pl.pallas_call(
        _mod_kernel,
        grid=(n // tn,),
        in_specs=[
            pl.BlockSpec((rows, d), lambda j: (0, 0)),
            pl.BlockSpec((d, tn), lambda j: (0, j)),
            pl.BlockSpec((d, tn), lambda j: (0, j)),
            pl.BlockSpec((1, tn), lambda j: (0, j)),
        ],
        out_specs=pl.BlockSpec((rows, tn), lambda j: (0, j)),
        out_shape=jax.ShapeDtypeStruct((rows, n), F32),
        compiler_params=_params("parallel"),
        name="mod",
    )(c_pad, w_hi, w_lo, b_ada)


def _inproj_kernel(x_ref, mod_ref, g_ref, w_ref, wsh_ref, wsl_ref, proj_ref, small_ref):
    h = _rms(x_ref[...]) * g_ref[...]
    h = h * (1.0 + mod_ref[0, 1:2, :]) + mod_ref[0, 0:1, :]
    hh, hl = _split2(h)
    cw = 512
    for j in range(MAIN_COLS // cw):
        proj_ref[:, j * cw:(j + 1) * cw] = _dot(hh, w_ref[:, j * cw:(j + 1) * cw]).astype(BF16)
    small_ref[...] = _dot(hh, wsh_ref[...]) + (_dot(hh, wsl_ref[...]) + _dot(hl, wsh_ref[...]))


def _inproj_call(x2, mod3, gain, w_main, ws_hi, ws_lo, seq):
    t, d = x2.shape
    tm = 512
    per_b = seq // tm
    return pl.pallas_call(
        _inproj_kernel,
        grid=(t // tm,),
        in_specs=[
            pl.BlockSpec((tm, d), lambda i: (i, 0)),
            pl.BlockSpec((1, N_MOD, d), lambda i: (i // per_b, 0, 0)),
            pl.BlockSpec((1, d), lambda i: (0, 0)),
            pl.BlockSpec((d, MAIN_COLS), lambda i: (0, 0)),
            pl.BlockSpec((d, LANES), lambda i: (0, 0)),
            pl.BlockSpec((d, LANES), lambda i: (0, 0)),
        ],
        out_specs=[
            pl.BlockSpec((tm, MAIN_COLS), lambda i: (i, 0)),
            pl.BlockSpec((tm, LANES), lambda i: (i, 0)),
        ],
        out_shape=[
            jax.ShapeDtypeStruct((t, MAIN_COLS), BF16),
            jax.ShapeDtypeStruct((t, LANES), F32),
        ],
        compiler_params=_params("parallel"),
        name="inproj",
    )(x2, mod3, gain, w_main, ws_hi, ws_lo)


GLA_TM = 512


def _gla_kernel(q_ref, k_ref, v_ref, g_ref, small_ref, wgh_ref, wgl_ref, bgk_ref, gain_ref, bf_ref,
                o_ref, fcol_ref, frow_ref, st_ref, fcar_ref, oacc_ref):
    tm = GLA_TM
    nc = tm // GLA_CHUNK

    @pl.when(pl.program_id(1) == 0)
    def _():
        st_ref[...] = jnp.zeros_like(st_ref)
        fcar_ref[...] = jnp.zeros_like(fcar_ref)

    small = small_ref[...]
    sh, sl = _split2(small)
    gk = _dot(sh, wgh_ref[...]) + (_dot(sh, wgl_ref[...]) + _dot(sl, wgh_ref[...])) + bgk_ref[...]
    log_a = _log_sigmoid(gk) * (1.0 / 16.0)

    row = lax.broadcasted_iota(jnp.int32, (tm, tm), 0)
    col = lax.broadcasted_iota(jnp.int32, (tm, tm), 1)
    same = (row >> 6) == (col >> 6)
    lower = col <= row
    tri = jnp.logical_and(same, lower)
    tri_b = jnp.where(tri, 1.0, 0.0).astype(BF16)
    ones_b = jnp.where(same, 1.0, 0.0).astype(BF16)
    lower_b = jnp.where(lower, 1.0, 0.0).astype(BF16)

    lh, ll = _split2(log_a)
    b = _dot(tri_b, lh) + _dot(tri_b, ll)
    btot = _dot(ones_b, lh) + _dot(ones_b, ll)

    q = q_ref[...].astype(F32)
    k = k_ref[...].astype(F32)
    qt = q * jnp.exp(b) * (GLA_DK ** -0.5)
    kt = (k * jnp.exp(-b)).astype(BF16)
    kd = (k * jnp.exp(btot - b)).astype(BF16)

    lane_qk = lax.broadcasted_iota(jnp.int32, (1, GLA_QK), 1)
    head_masks = [jnp.logical_and(lane_qk >= h * GLA_DK, lane_qk < (h + 1) * GLA_DK)
                  for h in range(GLA_HEADS)]

    for h in range(GLA_HEADS):
        qm = jnp.where(head_masks[h], qt, 0.0).astype(BF16)
        att = _dot_nt(qm, kt)
        att = jnp.where(tri, att, 0.0).astype(BF16)
        oacc_ref[:, h * GLA_DV:(h + 1) * GLA_DV] = _dot(att, v_ref[:, h * GLA_DV:(h + 1) * GLA_DV])

    for c in range(nc):
        rs = slice(c * GLA_CHUNK, (c + 1) * GLA_CHUNK)
        st = st_ref[...]
        qc = qt[rs]
        qs = jnp.concatenate([jnp.where(head_masks[h], qc, 0.0) for h in range(GLA_HEADS)],
                             axis=0).astype(BF16)
        oi = _dot_nt(qs, st.astype(BF16))
        for h in range(GLA_HEADS):
            oacc_ref[rs, h * GLA_DV:(h + 1) * GLA_DV] += oi[h * GLA_CHUNK:(h + 1) * GLA_CHUNK, :]
        kvt = _dot_tn(v_ref[rs, :], kd[rs])
        decay = jnp.exp(btot[c * GLA_CHUNK:c * GLA_CHUNK + 1, :])
        new = decay * st
        for h in range(GLA_HEADS):
            new = new + jnp.where(head_masks[h], kvt[h * GLA_DV:(h + 1) * GLA_DV, :], 0.0)
        st_ref[...] = new

    for h in range(GLA_HEADS):
        cs = slice(h * GLA_DV, (h + 1) * GLA_DV)
        y = _rms(oacc_ref[:, cs]) * gain_ref[:, cs]
        g = g_ref[:, cs].astype(F32)
        o_ref[:, cs] = (y * (g * jax.nn.sigmoid(g))).astype(BF16)

    lane = lax.broadcasted_iota(jnp.int32, (1, LANES), 1)
    lf = jnp.where(lane < FOX_HEADS, _log_sigmoid(small + bf_ref[...]), 0.0)
    l1, l2, l3 = _split3(lf)
    fl = _dot(lower_b, l1) + (_dot(lower_b, l2) + _dot(lower_b, l3)) + fcar_ref[...]
    fcol_ref[...] = fl
    flt = fl.T
    for h in range(FOX_HEADS):
        frow_ref[0, h] = flt[h:h + 1, :]
    fcar_ref[...] = fl[tm - 1:tm, :]


def _gla_call(proj, small, wg_hi, wg_lo, b_gk, gain, bf_pad, batch, seq):
    t = proj.shape[0]
    tm = GLA_TM
    nb = seq // tm

    def rows(b, i):
        return b * nb + i

    return pl.pallas_call(
        _gla_kernel,
        grid=(batch, nb),
        in_specs=[
            pl.BlockSpec((tm, GLA_QK), lambda b, i: (rows(b, i), COL_QG // GLA_QK)),
            pl.BlockSpec((tm, GLA_QK), lambda b, i: (rows(b, i), COL_KG // GLA_QK)),
            pl.BlockSpec((tm, GLA_WIDTH), lambda b, i: (rows(b, i), COL_VG // GLA_WIDTH)),
            pl.BlockSpec((tm, GLA_WIDTH), lambda b, i: (rows(b, i), COL_GG // GLA_WIDTH)),
            pl.BlockSpec((tm, LANES), lambda b, i: (rows(b, i), 0)),
            pl.BlockSpec((LANES, GLA_QK), lambda b, i: (0, 0)),
            pl.BlockSpec((LANES, GLA_QK), lambda b, i: (0, 0)),
            pl.BlockSpec((1, GLA_QK), lambda b, i: (0, 0)),
            pl.BlockSpec((1, GLA_WIDTH), lambda b, i: (0, 0)),
            pl.BlockSpec((1, LANES), lambda b, i: (0, 0)),
        ],
        out_specs=[
            pl.BlockSpec((tm, GLA_WIDTH), lambda b, i: (rows(b, i), 0)),
            pl.BlockSpec((tm, LANES), lambda b, i: (rows(b, i), 0)),
            pl.BlockSpec((1, FOX_HEADS, 1, tm), lambda b, i: (b, 0, 0, i)),
        ],
        out_shape=[
            jax.ShapeDtypeStruct((t, GLA_WIDTH), BF16),
            jax.ShapeDtypeStruct((t, LANES), F32),
            jax.ShapeDtypeStruct((batch, FOX_HEADS, 1, seq), F32),
        ],
        scratch_shapes=[
            pltpu.VMEM((GLA_DV, GLA_QK), F32),
            pltpu.VMEM((1, LANES), F32),
            pltpu.VMEM((tm, GLA_WIDTH), F32),
        ],
        compiler_params=_params("parallel", "arbitrary"),
        name="gla",
    )(proj, proj, proj, proj, small, wg_hi, wg_lo, b_gk, gain, bf_pad)


FOX_TQ = 512
FOX_TK = 512


def _fox_kernel(q_ref, k_ref, v_ref, frow_ref, o_ref, m_ref, l_ref, acc_ref):
    tq, tk = FOX_TQ, FOX_TK
    i = pl.program_id(2)
    lane = lax.broadcasted_iota(jnp.int32, (1, LANES), 1)
    q = q_ref[...] * (FOX_DH ** -0.5)
    lo_mask = lane < FOX_DH
    qm = [jnp.where(lo_mask, q, 0.0).astype(BF16), jnp.where(lo_mask, 0.0, q).astype(BF16)]
    q0 = pl.multiple_of(i * tq, tq)
    fref = [frow_ref[0, hh, :, pl.ds(q0, LANES)][:, 0:1] for hh in range(2)]

    m_ref[...] = jnp.full(m_ref.shape, NEG, F32)
    l_ref[...] = jnp.zeros_like(l_ref)
    acc_ref[...] = jnp.zeros_like(acc_ref)

    def step(k0, masked):
        kj = k_ref[pl.ds(k0, tk), :]
        vj = v_ref[pl.ds(k0, tk), :]
        for hh in range(2):
            bias = fref[hh] - frow_ref[0, hh, :, pl.ds(k0, tk)]
            s = _dot_nt(qm[hh], kj) + bias
            if masked:
                r = lax.broadcasted_iota(jnp.int32, (tq, tk), 0)
                c = lax.broadcasted_iota(jnp.int32, (tq, tk), 1)
                s = jnp.where(c <= r, s, NEG)
            m_old = m_ref[hh]
            m_new = jnp.maximum(m_old, jnp.max(s, axis=-1, keepdims=True))
            p = jnp.exp(s - m_new)
            alpha = jnp.exp(m_old - m_new)
            l_ref[hh] = alpha * l_ref[hh] + jnp.sum(p, axis=-1, keepdims=True)
            acc_ref[hh] = alpha * acc_ref[hh] + _dot(p.astype(BF16), vj)
            m_ref[hh] = m_new

    def body(j, carry):
        step(pl.multiple_of(j * tk, tk), False)
        return carry

    lax.fori_loop(0, i, body, 0)
    step(q0, True)

    o0 = acc_ref[0] / l_ref[0]
    o1 = acc_ref[1] / l_ref[1]
    o_ref[...] = jnp.where(lo_mask, o0, o1).astype(BF16)


def _fox_call(proj, frow, batch, seq):
    t = proj.shape[0]
    tq = FOX_TQ
    nq = seq // tq
    hpairs = FOX_WIDTH // LANES
    return pl.pallas_call(
        _fox_kernel,
        grid=(batch, hpairs, nq),
        in_specs=[
            pl.BlockSpec((tq, LANES), lambda b, hp, i: (b * nq + i, COL_QF // LANES + hp)),
            pl.BlockSpec((seq, LANES), lambda b, hp, i: (b, COL_KF // LANES + hp)),
            pl.BlockSpec((seq, LANES), lambda b, hp, i: (b, COL_VF // LANES + hp)),
            pl.BlockSpec((1, 2, 1, seq), lambda b, hp, i: (b, hp, 0, 0)),
        ],
        out_specs=pl.BlockSpec((tq, LANES), lambda b, hp, i: (b * nq + i, hp)),
        out_shape=jax.ShapeDtypeStruct((t, FOX_WIDTH), BF16),
        scratch_shapes=[
            pltpu.VMEM((2, tq, 1), F32),
            pltpu.VMEM((2, tq, 1), F32),
            pltpu.VMEM((2, tq, LANES), F32),
        ],
        compiler_params=_params("parallel", "parallel", "arbitrary"),
        name="fox",
    )(proj, proj, proj, frow)


POST_TM = 512


def _post_kernel(og_ref, of_ref, x_ref, mod_ref, pg_ref, mg_ref, wo_ref, wrh_ref, wrl_ref, br_ref,
                 x1_ref, h2_ref, idx_ref, rank_ref, gate_ref, cnt_ref, car_ref):
    tm = POST_TM

    @pl.when(pl.program_id(0) == 0)
    def _():
        car_ref[...] = jnp.zeros_like(car_ref)

    y = _dot(og_ref[...], wo_ref[:GLA_WIDTH, :]) + _dot(of_ref[...], wo_ref[GLA_WIDTH:, :])
    x1 = x_ref[...] + mod_ref[0, 2:3, :] * (_rms(y) * pg_ref[...])
    x1_ref[...] = x1
    h2 = _rms(x1) * mg_ref[...]
    h2 = h2 * (1.0 + mod_ref[0, 4:5, :]) + mod_ref[0, 3:4, :]
    for j in range(ROW_TILES):
        h2_ref[pl.ds(j, tm, stride=ROW_TILES), :] = h2[:, j * LANES:(j + 1) * LANES]

    hh, hl = _split2(h2)
    logits = _dot(hh, wrh_ref[...]) + (_dot(hh, wrl_ref[...]) + _dot(hl, wrh_ref[...])) + br_ref[...]
    lane = lax.broadcasted_iota(jnp.int32, (tm, LANES), 1).astype(F32)
    work = jnp.where(lane < N_EXPERTS, logits, -jnp.inf)
    vals, idxs = [], []
    for _ in range(TOP_K):
        mx = jnp.max(work, axis=-1, keepdims=True)
        ix = jnp.min(jnp.where(work == mx, lane, float(LANES)), axis=-1, keepdims=True)
        vals.append(mx)
        idxs.append(ix)
        work = jnp.where(lane == ix, -jnp.inf, work)
    es = [jnp.exp(v - vals[0]) for v in vals]
    den = es[0] + es[1] + es[2] + es[3]

    onehot = jnp.zeros((tm, LANES), F32)
    for ix in idxs:
        onehot = onehot + jnp.where(lane == ix, 1.0, 0.0)
    row = lax.broadcasted_iota(jnp.int32, (tm, tm), 0)
    col = lax.broadcasted_iota(jnp.int32, (tm, tm), 1)
    strict_b = jnp.where(col < row, 1.0, 0.0).astype(BF16)
    prefix = _dot(strict_b, onehot.astype(BF16)) + car_ref[...]
    car_new = car_ref[...] + jnp.sum(onehot, axis=0, keepdims=True)
    car_ref[...] = car_new
    cnt_ref[...] = car_new

    idx_o = jnp.zeros((tm, LANES), F32)
    rank_o = jnp.zeros((tm, LANES), F32)
    gate_o = jnp.zeros((tm, LANES), F32)
    for kk in range(TOP_K):
        rk = jnp.sum(jnp.where(lane == idxs[kk], prefix, 0.0), axis=-1, keepdims=True)
        sel = lane == float(kk)
        idx_o = jnp.where(sel, idxs[kk], idx_o)
        rank_o = jnp.where(sel, rk, rank_o)
        gate_o = jnp.where(sel, es[kk] / den, gate_o)
    idx_ref[...] = idx_o.astype(jnp.int32)
    rank_ref[...] = rank_o.astype(jnp.int32)
    gate_ref[...] = gate_o


def _post_call(o_gla, o_fox, x2, mod3, post_gain, mlp_gain, w_o, wr_hi, wr_lo, br_pad, seq):
    t, d = x2.shape
    tm = POST_TM
    per_b = seq // tm
    return pl.pallas_call(
        _post_kernel,
        grid=(t // tm,),
        in_specs=[
            pl.BlockSpec((tm, GLA_WIDTH), lambda i: (i, 0)),
            pl.BlockSpec((tm, FOX_WIDTH), lambda i: (i, 0)),
            pl.BlockSpec((tm, d), lambda i: (i, 0)),
            pl.BlockSpec((1, N_MOD, d), lambda i: (i // per_b, 0, 0)),
            pl.BlockSpec((1, d), lambda i: (0, 0)),
            pl.BlockSpec((1, d), lambda i: (0, 0)),
            pl.BlockSpec((GLA_WIDTH + FOX_WIDTH, d), lambda i: (0, 0)),
            pl.BlockSpec((d, LANES), lambda i: (0, 0)),
            pl.BlockSpec((d, LANES), lambda i: (0, 0)),
            pl.BlockSpec((1, LANES), lambda i: (0, 0)),
        ],
        out_specs=[
            pl.BlockSpec((tm, d), lambda i: (i, 0)),
            pl.BlockSpec((tm * ROW_TILES, LANES), lambda i: (i, 0)),
            pl.BlockSpec((tm, LANES), lambda i: (i, 0)),
            pl.BlockSpec((tm, LANES), lambda i: (i, 0)),
            pl.BlockSpec((tm, LANES), lambda i: (i, 0)),
            pl.BlockSpec((1, LANES), lambda i: (0, 0)),
        ],
        out_shape=[
            jax.ShapeDtypeStruct((t, d), F32),
            jax.ShapeDtypeStruct((t * ROW_TILES, LANES), F32),
            jax.ShapeDtypeStruct((t, LANES), jnp.int32),
            jax.ShapeDtypeStruct((t, LANES), jnp.int32),
            jax.ShapeDtypeStruct((t, LANES), F32),
            jax.ShapeDtypeStruct((1, LANES), F32),
        ],
        scratch_shapes=[pltpu.VMEM((1, LANES), F32)],
        compiler_params=_params("arbitrary"),
        name="post",
    )(o_gla, o_fox, x2, mod3, post_gain, mlp_gain, w_o, wr_hi, wr_lo, br_pad)


DISP_TM = 256


def _row_copy(src_ref, src_row, dst_ref, dst_row, sem):
    return pltpu.make_async_copy(
        src_ref.at[pl.ds(pl.multiple_of(src_row * ROW_TILES, ROW_TILES), ROW_TILES)],
        dst_ref.at[pl.ds(pl.multiple_of(dst_row * ROW_TILES, ROW_TILES), ROW_TILES)],
        sem)


def _dispatch_kernel(pos_ref, h_ref, xs_in_ref, xs_ref, sem):
    del xs_in_ref
    tm = DISP_TM

    def issue(t, carry):
        for kk in range(TOP_K):
            _row_copy(h_ref, t, xs_ref, pos_ref[0, 0, t * TOP_K + kk], sem).start()
        return carry

    lax.fori_loop(0, tm, issue, 0)

    def drain(t, carry):
        for kk in range(TOP_K):
            _row_copy(h_ref, 0, xs_ref, 0, sem).wait()
        return carry

    lax.fori_loop(0, tm, drain, 0)


def _dispatch_call(pos3, h2, xs_init):
    tm = DISP_TM
    n_steps = pos3.shape[0]
    return pl.pallas_call(
        _dispatch_kernel,
        grid=(n_steps,),
        in_specs=[
            pl.BlockSpec((1, 1, tm * TOP_K), lambda i: (i, 0, 0), memory_space=pltpu.SMEM),
            pl.BlockSpec((tm * ROW_TILES, LANES), lambda i: (i, 0)),
            pl.BlockSpec(memory_space=pl.ANY),
        ],
        out_specs=pl.BlockSpec(memory_space=pl.ANY),
        out_shape=jax.ShapeDtypeStruct(xs_init.shape, xs_init.dtype),
        scratch_shapes=[pltpu.SemaphoreType.DMA],
        input_output_aliases={2: 0},
        compiler_params=_params("arbitrary"),
        name="dispatch",
    )(pos3, h2, xs_init)


MOE_TM = 512


def _experts_kernel(te_ref, nu_ref, x_ref, win_ref, bin_ref, wout_ref, bout_ref, y_ref):
    del te_ref
    tm = MOE_TM
    used = pl.program_id(0) < nu_ref[0]

    @pl.when(jnp.logical_not(used))
    def _():
        y_ref[...] = jnp.zeros_like(y_ref)

    @pl.when(used)
    def _():
        x = jnp.concatenate([x_ref[pl.ds(j, tm, stride=ROW_TILES), :] for j in range(ROW_TILES)],
                            axis=1).astype(BF16)
        u = _dot(x, win_ref[0]) + bin_ref[0]
        glu = jnp.minimum(u[:, :D_FF], SWIGLU_LIMIT)
        lin = jnp.clip(u[:, D_FF:], -SWIGLU_LIMIT, SWIGLU_LIMIT)
        a = glu * jax.nn.sigmoid(SWIGLU_ALPHA * glu) * (lin + 1.0)
        y = _dot(a.astype(BF16), wout_ref[0]) + bout_ref[0]
        for j in range(ROW_TILES):
            y_ref[pl.ds(j, tm, stride=ROW_TILES), :] = y[:, j * LANES:(j + 1) * LANES]


def _experts_call(tile_expert, n_used, xs, w_in, b_in, w_out, b_out):
    tm = MOE_TM
    n_tiles = tile_expert.shape[0]
    d = D_MODEL

    def tile(i, te, nu):
        return (jnp.minimum(i, nu[0] - 1), 0)

    grid_spec = pltpu.PrefetchScalarGridSpec(
        num_scalar_prefetch=2,
        grid=(n_tiles,),
        in_specs=[
            pl.BlockSpec((tm * ROW_TILES, LANES), tile),
            pl.BlockSpec((1, d, 2 * D_FF), lambda i, te, nu: (te[i], 0, 0)),
            pl.BlockSpec((1, 1, 2 * D_FF), lambda i, te, nu: (te[i], 0, 0)),
            pl.BlockSpec((1, D_FF, d), lambda i, te, nu: (te[i], 0, 0)),
            pl.BlockSpec((1, 1, d), lambda i, te, nu: (te[i], 0, 0)),
        ],
        out_specs=pl.BlockSpec((tm * ROW_TILES, LANES), lambda i, te, nu: (i, 0)),
    )
    return pl.pallas_call(
        _experts_kernel,
        grid_spec=grid_spec,
        out_shape=jax.ShapeDtypeStruct(xs.shape, F32),
        compiler_params=_params("arbitrary"),
        name="experts",
    )(tile_expert, n_used, xs, w_in, b_in, w_out, b_out)


COMB_TM = 256


def _combine_kernel(pos_ref, y_ref, gate_ref, x1_ref, mod_ref, pg_ref, o_ref, buf_ref, sem):
    tm = COMB_TM

    def issue(t, carry):
        for kk in range(TOP_K):
            _row_copy(y_ref, pos_ref[0, 0, t * TOP_K + kk], buf_ref.at[kk], t, sem).start()
        return carry

    lax.fori_loop(0, tm, issue, 0)

    def drain(t, carry):
        for kk in range(TOP_K):
            _row_copy(y_ref, 0, buf_ref.at[kk], 0, sem).wait()
        return carry

    lax.fori_loop(0, tm, drain, 0)

    gates = gate_ref[...]
    acc = None
    for kk in range(TOP_K):
        rows = jnp.concatenate(
            [buf_ref[kk, pl.ds(j, tm, stride=ROW_TILES), :] for j in range(ROW_TILES)], axis=1)
        term = rows * gates[:, kk:kk + 1]
        acc = term if acc is None else acc + term
    o_ref[...] = x1_ref[...] + mod_ref[0, 5:6, :] * (_rms(acc) * pg_ref[...])


def _combine_call(pos3, y, gates, x1, mod3, post_gain, seq):
    t, d = x1.shape
    tm = COMB_TM
    per_b = seq // tm
    return pl.pallas_call(
        _combine_kernel,
        grid=(t // tm,),
        in_specs=[
            pl.BlockSpec((1, 1, tm * TOP_K), lambda i: (i, 0, 0), memory_space=pltpu.SMEM),
            pl.BlockSpec(memory_space=pl.ANY),
            pl.BlockSpec((tm, LANES), lambda i: (i, 0)),
            pl.BlockSpec((tm, d), lambda i: (i, 0)),
            pl.BlockSpec((1, N_MOD, d), lambda i: (i // per_b, 0, 0)),
            pl.BlockSpec((1, d), lambda i: (0, 0)),
        ],
        out_specs=pl.BlockSpec((tm, d), lambda i: (i, 0)),
        out_shape=jax.ShapeDtypeStruct((t, d), F32),
        scratch_shapes=[
            pltpu.VMEM((TOP_K, tm * ROW_TILES, LANES), F32),
            pltpu.SemaphoreType.DMA,
        ],
        compiler_params=_params("arbitrary"),
        name="combine",
    )(pos3, y, gates, x1, mod3, post_gain)


def _pad_cols(w, n):
    return jnp.pad(w, ((0, 0), (0, n - w.shape[1])))


def kernel(x, c, w_ada, b_ada, attn_pre_gain, attn_post_gain, w_in, w_gk2, b_gk, gla_norm_gain, b_f,
           w_o, mlp_pre_gain, mlp_post_gain, w_router, b_router, w_e_in, b_e_in, w_e_out, b_e_out):
    batch, seq, d = x.shape
    t = batch * seq
    x2 = x.reshape(t, d)

    o = 0
    cols = {}
    for name, width in (("qg", GLA_QK), ("kg", GLA_QK), ("vg", GLA_WIDTH), ("gg", GLA_WIDTH),
                        ("zg", GLA_LOWRANK), ("qf", FOX_WIDTH), ("kf", FOX_WIDTH), ("vf", FOX_WIDTH),
                        ("ff", FOX_HEADS)):
        cols[name] = w_in[:, o:o + width]
        o += width
    w_main = jnp.concatenate([cols[n] for n in ("qg", "kg", "vg", "gg", "qf", "kf", "vf")],
                             axis=1).astype(BF16)
    w_small = _pad_cols(jnp.concatenate([cols["ff"], cols["zg"]], axis=1), LANES)
    ws_hi, ws_lo = _split2(w_small)
    wg = jnp.zeros((LANES, GLA_QK), F32).at[SMALL_ZG:SMALL_ZG + GLA_LOWRANK].set(w_gk2)
    wg_hi, wg_lo = _split2(wg)
    bf_pad = jnp.zeros((1, LANES), F32).at[0, SMALL_FF:SMALL_FF + FOX_HEADS].set(b_f)
    wr_hi, wr_lo = _split2(_pad_cols(w_router, LANES))
    br_pad = _pad_cols(b_router.reshape(1, N_EXPERTS), LANES)
    wa_hi, wa_lo = _split2(w_ada)

    c_pad = jnp.pad(c, ((0, SUBLANES - batch % SUBLANES if batch % SUBLANES else 0), (0, 0)))
    mod = _mod_call(c_pad, wa_hi, wa_lo, b_ada.reshape(1, -1))
    mod3 = mod[:batch].reshape(batch, N_MOD, d)

    proj, small = _inproj_call(x2, mod3, attn_pre_gain.reshape(1, d), w_main, ws_hi, ws_lo, seq)
    o_gla, _, frow = _gla_call(proj, small, wg_hi, wg_lo, b_gk.reshape(1, -1),
                               gla_norm_gain.reshape(1, -1), bf_pad, batch, seq)
    o_fox = _fox_call(proj, frow, batch, seq)

    x1, h2, idx_w, rank_w, gate_w, cnt = _post_call(
        o_gla, o_fox, x2, mod3, attn_post_gain.reshape(1, d), mlp_pre_gain.reshape(1, d),
        w_o.astype(BF16), wr_hi, wr_lo, br_pad, seq)

    tm = MOE_TM
    n_tiles = (t * TOP_K) // tm + N_EXPERTS
    counts = cnt[0, :N_EXPERTS].astype(jnp.int32)
    padded = ((counts + tm - 1) // tm) * tm
    pend = jnp.cumsum(padded)
    pstart = pend - padded
    pos = pstart[idx_w[:, :TOP_K]] + rank_w[:, :TOP_K]
    n_used = (pend[-1] // tm).astype(jnp.int32).reshape(1)
    tile_expert = jnp.minimum(
        jnp.searchsorted(pend, jnp.arange(n_tiles, dtype=jnp.int32) * tm, side="right"),
        N_EXPERTS - 1).astype(jnp.int32)

    pos_d = pos.reshape(t // DISP_TM, 1, DISP_TM * TOP_K)
    xs = _dispatch_call(pos_d, h2, jnp.zeros((n_tiles * tm * ROW_TILES, LANES), F32))
    y = _experts_call(tile_expert, n_used, xs, w_e_in.astype(BF16),
                      b_e_in.reshape(N_EXPERTS, 1, -1), w_e_out.astype(BF16),
                      b_e_out.reshape(N_EXPERTS, 1, -1))
    pos_c = pos.reshape(t // COMB_TM, 1, COMB_TM * TOP_K)
    out = _combine_call(pos_c, y, gate_w, x1, mod3, mlp_post_gain.reshape(1, d), seq)
    return out.reshape(batch, seq, d)
```

```python
import functools

import jax
import jax.numpy as jnp
from jax import lax
from jax.experimental import pallas as pl
from jax.experimental.pallas import tpu as pltpu

F32 = jnp.float32
BF16 = jnp.bfloat16

D_MODEL = 1024
N_MOD = 6
GLA_HEADS = 4
GLA_DK = 64
GLA_DV = 128
GLA_QK = GLA_HEADS * GLA_DK
GLA_WIDTH = GLA_HEADS * GLA_DV
GLA_LOWRANK = 16
GLA_CHUNK = 64
FOX_HEADS = 8
FOX_DH = 64
FOX_WIDTH = FOX_HEADS * FOX_DH
N_EXPERTS = 32
TOP_K = 4
D_FF = D_MODEL
SWIGLU_ALPHA = 1.702
SWIGLU_LIMIT = 7.0
EPS = 1e-6

LANES = 128
SUBLANES = 8
ROW_TILES = D_MODEL // LANES

COL_QG, COL_KG, COL_VG, COL_GG = 0, 256, 512, 1024
COL_QF, COL_KF = 1536, 2048
MAIN_COLS = 2560
LOG2E = 1.4426950408889634
FOX_Q_SCALE = (FOX_DH ** -0.5) * LOG2E
FOX_BIAS_LANES = 6
SMALL_FF, SMALL_ZG = 0, 8

NEG = -1e30
VMEM_LIMIT = 56 * 1024 * 1024


def _dot(a, b):
    return jnp.dot(a, b, preferred_element_type=F32)


def _dot_nt(a, b):
    return lax.dot_general(a, b, (((1,), (1,)), ((), ())), preferred_element_type=F32)


def _dot_tn(a, b):
    return lax.dot_general(a, b, (((0,), (0,)), ((), ())), preferred_element_type=F32)


def _split2(a):
    hi = a.astype(BF16)
    lo = (a - hi.astype(F32)).astype(BF16)
    return hi, lo


def _split3(a):
    hi = a.astype(BF16)
    r = a - hi.astype(F32)
    mid = r.astype(BF16)
    lo = (r - mid.astype(F32)).astype(BF16)
    return hi, mid, lo


def _log_sigmoid(x):
    return jnp.minimum(x, 0.0) - jnp.log1p(jnp.exp(-jnp.abs(x)))


def _rms(x):
    return x * lax.rsqrt(jnp.mean(x * x, axis=-1, keepdims=True) + EPS)


def _params(*sem):
    return pltpu.CompilerParams(dimension_semantics=sem, vmem_limit_bytes=VMEM_LIMIT)


def _mod_kernel(c_ref, wh_ref, wl_ref, b_ref, o_ref):
    c = c_ref[...]
    s = c * jax.nn.sigmoid(c)
    sh, sl = _split2(s)
    o_ref[...] = _dot(sh, wh_ref[...]) + (_dot(sh, wl_ref[...]) + _dot(sl, wh_ref[...])) + b_ref[...]


def _mod_call(c_pad, w_hi, w_lo, b_ada):
    rows, d = c_pad.shape
    n = w_hi.shape[1]
    tn = 1024
    return pl.pallas_call(
        _mod_kernel,
        grid=(n // tn,),
        in_specs=[
            pl.BlockSpec((rows, d), lambda j: (0, 0)),
            pl.BlockSpec((d, tn), lambda j: (0, j)),
            pl.BlockSpec((d, tn), lambda j: (0, j)),
            pl.BlockSpec((1, tn), lambda j: (0, j)),
        ],
        out_specs=pl.BlockSpec((rows, tn), lambda j: (0, j)),
        out_shape=jax.ShapeDtypeStruct((rows, n), F32),
        compiler_params=_params("parallel"),
        name="mod",
    )(c_pad, w_hi, w_lo, b_ada)


def _inproj_kernel(x_ref, mod_ref, g_ref, w_ref, wvt_ref, wsh_ref, wsl_ref, proj_ref, vt_ref, small_ref):
    h = _rms(x_ref[...]) * g_ref[...]
    h = h * (1.0 + mod_ref[0, 1:2, :]) + mod_ref[0, 0:1, :]
    hh, hl = _split2(h)
    cw = 512
    for j in range(MAIN_COLS // cw):
        r = _dot(hh, w_ref[:, j * cw:(j + 1) * cw])
        if j * cw == COL_QF:
            r = r * FOX_Q_SCALE
        proj_ref[:, j * cw:(j + 1) * cw] = r.astype(BF16)
    vt = _dot_nt(wvt_ref[...], hh).astype(BF16)
    ones = jnp.ones((LANES - FOX_DH, vt.shape[1]), BF16)
    for h in range(FOX_HEADS):
        vt_ref[h * LANES:h * LANES + FOX_DH, :] = vt[h * FOX_DH:(h + 1) * FOX_DH, :]
        vt_ref[h * LANES + FOX_DH:(h + 1) * LANES, :] = ones
    small_ref[...] = _dot(hh, wsh_ref[...]) + (_dot(hh, wsl_ref[...]) + _dot(hl, wsh_ref[...]))


def _inproj_call(x2, mod3, gain, w_main, w_vt, ws_hi, ws_lo, seq):
    t, d = x2.shape
    tm = 512
    per_b = seq // tm
    return pl.pallas_call(
        _inproj_kernel,
        grid=(t // tm,),
        in_specs=[
            pl.BlockSpec((tm, d), lambda i: (i, 0)),
            pl.BlockSpec((1, N_MOD, d), lambda i: (i // per_b, 0, 0)),
            pl.BlockSpec((1, d), lambda i: (0, 0)),
            pl.BlockSpec((d, MAIN_COLS), lambda i: (0, 0)),
            pl.BlockSpec((FOX_WIDTH, d), lambda i: (0, 0)),
            pl.BlockSpec((d, LANES), lambda i: (0, 0)),
            pl.BlockSpec((d, LANES), lambda i: (0, 0)),
        ],
        out_specs=[
            pl.BlockSpec((tm, MAIN_COLS), lambda i: (i, 0)),
            pl.BlockSpec((FOX_HEADS * LANES, tm), lambda i: (0, i)),
            pl.BlockSpec((tm, LANES), lambda i: (i, 0)),
        ],
        out_shape=[
            jax.ShapeDtypeStruct((t, MAIN_COLS), BF16),
            jax.ShapeDtypeStruct((FOX_HEADS * LANES, t), BF16),
            jax.ShapeDtypeStruct((t, LANES), F32),
        ],
        compiler_params=_params("parallel"),
        name="inproj",
    )(x2, mod3, gain, w_main, w_vt, ws_hi, ws_lo)


GLA_TM = 512


def _gla_kernel(q_ref, k_ref, v_ref, g_ref, small_ref, wgh_ref, wgl_ref, bgk_ref, gain_ref, bf_ref,
                sel_ref, ones_ref, o_ref, kx_ref, frow_ref, st_ref, fcar_ref, oacc_ref):
    tm = GLA_TM
    nc = tm // GLA_CHUNK

    @pl.when(pl.program_id(1) == 0)
    def _():
        st_ref[...] = jnp.zeros_like(st_ref)
        fcar_ref[...] = jnp.zeros_like(fcar_ref)

    small = small_ref[...]
    sh, sl = _split2(small)
    gk = _dot(sh, wgh_ref[...]) + (_dot(sh, wgl_ref[...]) + _dot(sl, wgh_ref[...])) + bgk_ref[...]
    log_a = _log_sigmoid(gk) * (1.0 / 16.0)

    row = lax.broadcasted_iota(jnp.int32, (tm, tm), 0)
    col = lax.broadcasted_iota(jnp.int32, (tm, tm), 1)
    same = (row >> 6) == (col >> 6)
    lower = col <= row
    tri = jnp.logical_and(same, lower)
    tri_b = jnp.where(tri, 1.0, 0.0).astype(BF16)
    ones_b = jnp.where(same, 1.0, 0.0).astype(BF16)
    lower_b = jnp.where(lower, 1.0, 0.0).astype(BF16)

    lh, ll = _split2(log_a)
    b = _dot(tri_b, lh) + _dot(tri_b, ll)
    btot = _dot(ones_b, lh) + _dot(ones_b, ll)

    q = q_ref[...].astype(F32)
    k = k_ref[...].astype(F32)
    qt = q * jnp.exp(b) * (GLA_DK ** -0.5)
    kt = (k * jnp.exp(-b)).astype(BF16)
    kd = (k * jnp.exp(btot - b)).astype(BF16)

    lane_qk = lax.broadcasted_iota(jnp.int32, (1, GLA_QK), 1)
    head_masks = [jnp.logical_and(lane_qk >= h * GLA_DK, lane_qk < (h + 1) * GLA_DK)
                  for h in range(GLA_HEADS)]

    for h in range(GLA_HEADS):
        qm = jnp.where(head_masks[h], qt, 0.0).astype(BF16)
        att = _dot_nt(qm, kt)
        att = jnp.where(tri, att, 0.0).astype(BF16)
        oacc_ref[:, h * GLA_DV:(h + 1) * GLA_DV] = _dot(att, v_ref[:, h * GLA_DV:(h + 1) * GLA_DV])

    for c in range(nc):
        rs = slice(c * GLA_CHUNK, (c + 1) * GLA_CHUNK)
        st = st_ref[...]
        qc = qt[rs]
        qs = jnp.concatenate([jnp.where(head_masks[h], qc, 0.0) for h in range(GLA_HEADS)],
                             axis=0).astype(BF16)
        oi = _dot_nt(qs, st.astype(BF16))
        for h in range(GLA_HEADS):
            oacc_ref[rs, h * GLA_DV:(h + 1) * GLA_DV] += oi[h * GLA_CHUNK:(h + 1) * GLA_CHUNK, :]
        kvt = _dot_tn(v_ref[rs, :], kd[rs])
        decay = jnp.exp(btot[c * GLA_CHUNK:c * GLA_CHUNK + 1, :])
        new = decay * st
        for h in range(GLA_HEADS):
            new = new + jnp.where(head_masks[h], kvt[h * GLA_DV:(h + 1) * GLA_DV, :], 0.0)
        st_ref[...] = new

    for h in range(GLA_HEADS):
        cs = slice(h * GLA_DV, (h + 1) * GLA_DV)
        y = _rms(oacc_ref[:, cs]) * gain_ref[:, cs]
        g = g_ref[:, cs].astype(F32)
        o_ref[:, cs] = (y * (g * jax.nn.sigmoid(g))).astype(BF16)

    lane = lax.broadcasted_iota(jnp.int32, (1, LANES), 1)
    lf = jnp.where(lane < FOX_HEADS, _log_sigmoid(small + bf_ref[...]), 0.0)
    l1, l2, l3 = _split3(lf)
    fl = _dot(lower_b, l1) + (_dot(lower_b, l2) + _dot(lower_b, l3)) + fcar_ref[...]
    fcar_ref[...] = fl[tm - 1:tm, :]
    fs = fl * LOG2E
    fst = fs.T
    for h in range(FOX_HEADS):
        frow_ref[0, h] = fst[h:h + 1, :]
    n1, n2, n3 = _split3(-fs)
    kx = _dot(n1, sel_ref[0]) + (_dot(n2, sel_ref[1]) + _dot(n3, sel_ref[2])) + ones_ref[...]
    kx_ref[...] = kx.astype(BF16)


def _fox_bias_selectors():
    import numpy as np
    sel = np.zeros((3, LANES, FOX_WIDTH), np.float32)
    ones = np.zeros((1, FOX_WIDTH), np.float32)
    for h in range(FOX_HEADS):
        base = (h // 2) * LANES + (h % 2) * FOX_BIAS_LANES
        for i in range(3):
            sel[i, h, base + i] = 1.0
            ones[0, base + 3 + i] = 1.0
    return jnp.asarray(sel, BF16), jnp.asarray(ones, F32)


def _gla_call(proj, small, wg_hi, wg_lo, b_gk, gain, bf_pad, batch, seq):
    t = proj.shape[0]
    tm = GLA_TM
    nb = seq // tm

    def rows(b, i):
        return b * nb + i

    sel, ones = _fox_bias_selectors()
    return pl.pallas_call(
        _gla_kernel,
        grid=(batch, nb),
        in_specs=[
            pl.BlockSpec((tm, GLA_QK), lambda b, i: (rows(b, i), COL_QG // GLA_QK)),
            pl.BlockSpec((tm, GLA_QK), lambda b, i: (rows(b, i), COL_KG // GLA_QK)),
            pl.BlockSpec((tm, GLA_WIDTH), lambda b, i: (rows(b, i), COL_VG // GLA_WIDTH)),
            pl.BlockSpec((tm, GLA_WIDTH), lambda b, i: (rows(b, i), COL_GG // GLA_WIDTH)),
            pl.BlockSpec((tm, LANES), lambda b, i: (rows(b, i), 0)),
            pl.BlockSpec((LANES, GLA_QK), lambda b, i: (0, 0)),
            pl.BlockSpec((LANES, GLA_QK), lambda b, i: (0, 0)),
            pl.BlockSpec((1, GLA_QK), lambda b, i: (0, 0)),
            pl.BlockSpec((1, GLA_WIDTH), lambda b, i: (0, 0)),
            pl.BlockSpec((1, LANES), lambda b, i: (0, 0)),
            pl.BlockSpec((3, LANES, FOX_WIDTH), lambda b, i: (0, 0, 0)),
            pl.BlockSpec((1, FOX_WIDTH), lambda b, i: (0, 0)),
        ],
        out_specs=[
            pl.BlockSpec((tm, GLA_WIDTH), lambda b, i: (rows(b, i), 0)),
            pl.BlockSpec((tm, FOX_WIDTH), lambda b, i: (rows(b, i), 0)),
            pl.BlockSpec((1, FOX_HEADS, 1, tm), lambda b, i: (b, 0, 0, i)),
        ],
        out_shape=[
            jax.ShapeDtypeStruct((t, GLA_WIDTH), BF16),
            jax.ShapeDtypeStruct((t, FOX_WIDTH), BF16),
            jax.ShapeDtypeStruct((batch, FOX_HEADS, 1, seq), F32),
        ],
        scratch_shapes=[
            pltpu.VMEM((GLA_DV, GLA_QK), F32),
            pltpu.VMEM((1, LANES), F32),
            pltpu.VMEM((tm, GLA_WIDTH), F32),
        ],
        compiler_params=_params("parallel", "arbitrary"),
        name="gla",
    )(proj, proj, proj, proj, small, wg_hi, wg_lo, b_gk, gain, bf_pad, sel, ones)


FOX_TQ = 512
FOX_TK = 512


def _fox_kernel(q_ref, k_ref, kx_ref, vt_ref, frow_ref, o_ref, m_ref, acc_ref, sa_ref, sb_ref):
    tq, tk = FOX_TQ, FOX_TK
    i = pl.program_id(2)
    q0 = pl.multiple_of(i * tq, tq)
    lane = lax.broadcasted_iota(jnp.int32, (1, LANES), 1)
    q = q_ref[...]
    q_aug = []
    for hh in range(2):
        head_lanes = (lane < FOX_DH) if hh == 0 else (lane >= FOX_DH)
        qm = jnp.where(head_lanes, q, 0.0).astype(BF16)
        fr = jnp.broadcast_to(frow_ref[0, hh, :, pl.ds(q0, LANES)][:, 0:1], (1, LANES))
        f1, f2, f3 = _split3(fr)
        base = hh * FOX_BIAS_LANES
        qx = jnp.where(jnp.logical_and(lane >= base, lane < base + 3), 1.0, 0.0).astype(BF16)
        qx = jnp.where(lane == base + 3, f1, qx)
        qx = jnp.where(lane == base + 4, f2, qx)
        qx = jnp.where(lane == base + 5, f3, qx)
        q_aug.append(jnp.concatenate([qm, jnp.broadcast_to(qx, (tq, LANES))], axis=1))

    m_ref[...] = jnp.full(m_ref.shape, NEG, F32)
    acc_ref[...] = jnp.zeros_like(acc_ref)

    def produce(buf, blk):
        k0 = pl.multiple_of(blk * tk, tk)
        k_aug = jnp.concatenate([k_ref[pl.ds(k0, tk), :], kx_ref[pl.ds(k0, tk), :]], axis=1)
        for hh in range(2):
            buf[hh] = _dot_nt(k_aug, q_aug[hh])

    def consume(buf, blk, masked):
        k0 = pl.multiple_of(blk * tk, tk)
        for hh in range(2):
            st = buf[hh]
            if masked:
                key = lax.broadcasted_iota(jnp.int32, (tk, tq), 0)
                qry = lax.broadcasted_iota(jnp.int32, (tk, tq), 1)
                st = jnp.where(key <= qry, st, NEG)
            m_old = m_ref[hh]
            m_new = jnp.maximum(m_old, jnp.max(st, axis=0, keepdims=True))
            p = jnp.exp2(st - m_new).astype(BF16)
            alpha = jnp.exp2(m_old - m_new)
            vt = vt_ref[hh * LANES:(hh + 1) * LANES, pl.ds(k0, tk)]
            acc_ref[hh] = alpha * acc_ref[hh] + _dot(vt, p)
            m_ref[hh] = m_new

    produce(sa_ref, 0)

    def pair(t, carry):
        b0 = 2 * t
        consume(sa_ref, b0, False)
        produce(sb_ref, b0 + 1)
        consume(sb_ref, b0 + 1, False)
        produce(sa_ref, b0 + 2)
        return carry

    lax.fori_loop(0, lax.shift_right_logical(i, 1), pair, 0)

    @pl.when((i & 1) == 0)
    def _():
        consume(sa_ref, i, True)

    @pl.when((i & 1) == 1)
    def _():
        consume(sa_ref, i - 1, False)
        produce(sb_ref, i)
        consume(sb_ref, i, True)

    outs = []
    for hh in range(2):
        acc = acc_ref[hh]
        outs.append(acc[:FOX_DH] * (1.0 / acc[FOX_DH:FOX_DH + 1]))
    o_ref[...] = jnp.concatenate(outs, axis=0).T.astype(BF16)


def _fox_call(proj, kx, vt, frow, batch, seq):
    t = proj.shape[0]
    tq = FOX_TQ
    nq = seq // tq
    hpairs = FOX_WIDTH // LANES
    return pl.pallas_call(
        _fox_kernel,
        grid=(batch, hpairs, nq),
        in_specs=[
            pl.BlockSpec((tq, LANES), lambda b, hp, i: (b * nq + i, COL_QF // LANES + hp)),
            pl.BlockSpec((seq, LANES), lambda b, hp, i: (b, COL_KF // LANES + hp)),
            pl.BlockSpec((seq, LANES), lambda b, hp, i: (b, hp)),
            pl.BlockSpec((2 * LANES, seq), lambda b, hp, i: (hp, b)),
            pl.BlockSpec((1, 2, 1, seq), lambda b, hp, i: (b, hp, 0, 0)),
        ],
        out_specs=pl.BlockSpec((tq, LANES), lambda b, hp, i: (b * nq + i, hp)),
        out_shape=jax.ShapeDtypeStruct((t, FOX_WIDTH), BF16),
        scratch_shapes=[
            pltpu.VMEM((2, 1, tq), F32),
            pltpu.VMEM((2, LANES, tq), F32),
            pltpu.VMEM((2, FOX_TK, tq), F32),
            pltpu.VMEM((2, FOX_TK, tq), F32),
        ],
        compiler_params=_params("parallel", "parallel", "arbitrary"),
        name="fox",
    )(proj, proj, kx, vt, frow)


POST_TM = 512


def _post_kernel(og_ref, of_ref, x_ref, mod_ref, pg_ref, mg_ref, wo_ref, wrh_ref, wrl_ref, br_ref,
                 x1_ref, h2_ref, idx_ref, rank_ref, gate_ref, cnt_ref, car_ref):
    tm = POST_TM

    @pl.when(pl.program_id(0) == 0)
    def _():
        car_ref[...] = jnp.zeros_like(car_ref)

    y = _dot(og_ref[...], wo_ref[:GLA_WIDTH, :]) + _dot(of_ref[...], wo_ref[GLA_WIDTH:, :])
    x1 = x_ref[...] + mod_ref[0, 2:3, :] * (_rms(y) * pg_ref[...])
    x1_ref[...] = x1
    h2 = _rms(x1) * mg_ref[...]
    h2 = h2 * (1.0 + mod_ref[0, 4:5, :]) + mod_ref[0, 3:4, :]
    for j in range(ROW_TILES):
        h2_ref[pl.ds(j, tm, stride=ROW_TILES), :] = h2[:, j * LANES:(j + 1) * LANES]

    hh, hl = _split2(h2)
    logits = _dot(hh, wrh_ref[...]) + (_dot(hh, wrl_ref[...]) + _dot(hl, wrh_ref[...])) + br_ref[...]
    lane = lax.broadcasted_iota(jnp.int32, (tm, LANES), 1).astype(F32)
    work = jnp.where(lane < N_EXPERTS, logits, -jnp.inf)
    vals, idxs = [], []
    for _ in range(TOP_K):
        mx = jnp.max(work, axis=-1, keepdims=True)
        ix = jnp.min(jnp.where(work == mx, lane, float(LANES)), axis=-1, keepdims=True)
        vals.append(mx)
        idxs.append(ix)
        work = jnp.where(lane == ix, -jnp.inf, work)
    es = [jnp.exp(v - vals[0]) for v in vals]
    den = es[0] + es[1] + es[2] + es[3]

    onehot = jnp.zeros((tm, LANES), F32)
    for ix in idxs:
        onehot = onehot + jnp.where(lane == ix, 1.0, 0.0)
    row = lax.broadcasted_iota(jnp.int32, (tm, tm), 0)
    col = lax.broadcasted_iota(jnp.int32, (tm, tm), 1)
    strict_b = jnp.where(col < row, 1.0, 0.0).astype(BF16)
    prefix = _dot(strict_b, onehot.astype(BF16)) + car_ref[...]
    car_new = car_ref[...] + jnp.sum(onehot, axis=0, keepdims=True)
    car_ref[...] = car_new
    cnt_ref[...] = car_new

    idx_o = jnp.zeros((tm, LANES), F32)
    rank_o = jnp.zeros((tm, LANES), F32)
    gate_o = jnp.zeros((tm, LANES), F32)
    for kk in range(TOP_K):
        rk = jnp.sum(jnp.where(lane == idxs[kk], prefix, 0.0), axis=-1, keepdims=True)
        sel = lane == float(kk)
        idx_o = jnp.where(sel, idxs[kk], idx_o)
        rank_o = jnp.where(sel, rk, rank_o)
        gate_o = jnp.where(sel, es[kk] / den, gate_o)
    idx_ref[...] = idx_o.astype(jnp.int32)
    rank_ref[...] = rank_o.astype(jnp.int32)
    gate_ref[...] = gate_o


def _post_call(o_gla, o_fox, x2, mod3, post_gain, mlp_gain, w_o, wr_hi, wr_lo, br_pad, seq):
    t, d = x2.shape
    tm = POST_TM
    per_b = seq // tm
    return pl.pallas_call(
        _post_kernel,
        grid=(t // tm,),
        in_specs=[
            pl.BlockSpec((tm, GLA_WIDTH), lambda i: (i, 0)),
            pl.BlockSpec((tm, FOX_WIDTH), lambda i: (i, 0)),
            pl.BlockSpec((tm, d), lambda i: (i, 0)),
            pl.BlockSpec((1, N_MOD, d), lambda i: (i // per_b, 0, 0)),
            pl.BlockSpec((1, d), lambda i: (0, 0)),
            pl.BlockSpec((1, d), lambda i: (0, 0)),
            pl.BlockSpec((GLA_WIDTH + FOX_WIDTH, d), lambda i: (0, 0)),
            pl.BlockSpec((d, LANES), lambda i: (0, 0)),
            pl.BlockSpec((d, LANES), lambda i: (0, 0)),
            pl.BlockSpec((1, LANES), lambda i: (0, 0)),
        ],
        out_specs=[
            pl.BlockSpec((tm, d), lambda i: (i, 0)),
            pl.BlockSpec((tm * ROW_TILES, LANES), lambda i: (i, 0)),
            pl.BlockSpec((tm, LANES), lambda i: (i, 0)),
            pl.BlockSpec((tm, LANES), lambda i: (i, 0)),
            pl.BlockSpec((tm, LANES), lambda i: (i, 0)),
            pl.BlockSpec((1, LANES), lambda i: (0, 0)),
        ],
        out_shape=[
            jax.ShapeDtypeStruct((t, d), F32),
            jax.ShapeDtypeStruct((t * ROW_TILES, LANES), F32),
            jax.ShapeDtypeStruct((t, LANES), jnp.int32),
            jax.ShapeDtypeStruct((t, LANES), jnp.int32),
            jax.ShapeDtypeStruct((t, LANES), F32),
            jax.ShapeDtypeStruct((1, LANES), F32),
        ],
        scratch_shapes=[pltpu.VMEM((1, LANES), F32)],
        compiler_params=_params("arbitrary"),
        name="post",
    )(o_gla, o_fox, x2, mod3, post_gain, mlp_gain, w_o, wr_hi, wr_lo, br_pad)


DISP_TM = 256


def _row_copy(src_ref, src_row, dst_ref, dst_row, sem):
    return pltpu.make_async_copy(
        src_ref.at[pl.ds(pl.multiple_of(src_row * ROW_TILES, ROW_TILES), ROW_TILES)],
        dst_ref.at[pl.ds(pl.multiple_of(dst_row * ROW_TILES, ROW_TILES), ROW_TILES)],
        sem)


def _dispatch_kernel(pos_ref, h_ref, xs_in_ref, xs_ref, sem):
    del xs_in_ref
    tm = DISP_TM

    def issue(t, carry):
        for kk in range(TOP_K):
            _row_copy(h_ref, t, xs_ref, pos_ref[0, 0, t * TOP_K + kk], sem).start(priority=kk % 2)
        return carry

    lax.fori_loop(0, tm, issue, 0)
    for _ in range(TOP_K):
        pltpu.make_async_copy(h_ref, xs_ref.at[pl.ds(0, tm * ROW_TILES)], sem).wait()


def _dispatch_call(pos3, h2, xs_init):
    tm = DISP_TM
    n_steps = pos3.shape[0]
    return pl.pallas_call(
        _dispatch_kernel,
        grid=(n_steps,),
        in_specs=[
            pl.BlockSpec((1, 1, tm * TOP_K), lambda i: (i, 0, 0), memory_space=pltpu.SMEM),
            pl.BlockSpec((tm * ROW_TILES, LANES), lambda i: (i, 0)),
            pl.BlockSpec(memory_space=pl.ANY),
        ],
        out_specs=pl.BlockSpec(memory_space=pl.ANY),
        out_shape=jax.ShapeDtypeStruct(xs_init.shape, xs_init.dtype),
        scratch_shapes=[pltpu.SemaphoreType.DMA],
        input_output_aliases={2: 0},
        compiler_params=_params("arbitrary"),
        name="dispatch",
    )(pos3, h2, xs_init)


MOE_TM = 512


def _experts_kernel(te_ref, nu_ref, x_ref, win_ref, bin_ref, wout_ref, bout_ref, y_ref,
                    wib_ref, wob_ref):
    tm = MOE_TM
    i = pl.program_id(0)
    used = i < nu_ref[0]

    @pl.when(jnp.logical_not(used))
    def _():
        y_ref[...] = jnp.zeros_like(y_ref)

    fresh = jnp.logical_or(i == 0, te_ref[i] != te_ref[jnp.maximum(i - 1, 0)])

    @pl.when(jnp.logical_and(used, fresh))
    def _():
        wib_ref[...] = win_ref[0].astype(BF16)
        wob_ref[...] = wout_ref[0].astype(BF16)

    @pl.when(used)
    def _():
        x = jnp.concatenate([x_ref[pl.ds(j, tm, stride=ROW_TILES), :] for j in range(ROW_TILES)],
                            axis=1).astype(BF16)
        u = _dot(x, wib_ref[...]) + bin_ref[0]
        glu = jnp.minimum(u[:, :D_FF], SWIGLU_LIMIT)
        lin = jnp.clip(u[:, D_FF:], -SWIGLU_LIMIT, SWIGLU_LIMIT)
        a = glu * jax.nn.sigmoid(SWIGLU_ALPHA * glu) * (lin + 1.0)
        y = _dot(a.astype(BF16), wob_ref[...]) + bout_ref[0]
        for j in range(ROW_TILES):
            y_ref[pl.ds(j, tm, stride=ROW_TILES), :] = y[:, j * LANES:(j + 1) * LANES]


def _experts_call(tile_expert, n_used, xs, w_in, b_in, w_out, b_out):
    tm = MOE_TM
    n_tiles = tile_expert.shape[0]
    d = D_MODEL

    def tile(i, te, nu):
        return (jnp.minimum(i, nu[0] - 1), 0)

    grid_spec = pltpu.PrefetchScalarGridSpec(
        num_scalar_prefetch=2,
        grid=(n_tiles,),
        in_specs=[
            pl.BlockSpec((tm * ROW_TILES, LANES), tile),
            pl.BlockSpec((1, d, 2 * D_FF), lambda i, te, nu: (te[i], 0, 0)),
            pl.BlockSpec((1, 1, 2 * D_FF), lambda i, te, nu: (te[i], 0, 0)),
            pl.BlockSpec((1, D_FF, d), lambda i, te, nu: (te[i], 0, 0)),
            pl.BlockSpec((1, 1, d), lambda i, te, nu: (te[i], 0, 0)),
        ],
        out_specs=pl.BlockSpec((tm * ROW_TILES, LANES), lambda i, te, nu: (i, 0)),
        scratch_shapes=[
            pltpu.VMEM((d, 2 * D_FF), BF16),
            pltpu.VMEM((D_FF, d), BF16),
        ],
    )
    return pl.pallas_call(
        _experts_kernel,
        grid_spec=grid_spec,
        out_shape=jax.ShapeDtypeStruct(xs.shape, F32),
        compiler_params=_params("arbitrary"),
        name="experts",
    )(tile_expert, n_used, xs, w_in, b_in, w_out, b_out)


COMB_TM = 256


def _combine_kernel(pos_ref, posn_ref, y_ref, gate_ref, x1_ref, mod_ref, pg_ref, o_ref, buf_ref, sem):
    tm = COMB_TM
    i = pl.program_id(0)
    slot = i & 1

    def gather(p_ref, s):
        def issue(t, carry):
            for kk in range(TOP_K):
                _row_copy(y_ref, p_ref[0, 0, t * TOP_K + kk], buf_ref.at[s, kk], t,
                          sem.at[s]).start(priority=kk % 2)
            return carry

        lax.fori_loop(0, tm, issue, 0)

    @pl.when(i == 0)
    def _():
        gather(pos_ref, 0)

    @pl.when(i + 1 < pl.num_programs(0))
    def _():
        gather(posn_ref, 1 - slot)

    for kk in range(TOP_K):
        pltpu.make_async_copy(y_ref.at[pl.ds(0, tm * ROW_TILES)], buf_ref.at[slot, kk],
                              sem.at[slot]).wait()

    gates = gate_ref[...]
    acc = None
    for kk in range(TOP_K):
        rows = jnp.concatenate(
            [buf_ref[slot, kk, pl.ds(j, tm, stride=ROW_TILES), :] for j in range(ROW_TILES)], axis=1)
        term = rows * gates[:, kk:kk + 1]
        acc = term if acc is None else acc + term
    o_ref[...] = x1_ref[...] + mod_ref[0, 5:6, :] * (_rms(acc) * pg_ref[...])


def _combine_call(pos3, y, gates, x1, mod3, post_gain, seq):
    t, d = x1.shape
    tm = COMB_TM
    per_b = seq // tm
    n_steps = t // tm
    return pl.pallas_call(
        _combine_kernel,
        grid=(n_steps,),
        in_specs=[
            pl.BlockSpec((1, 1, tm * TOP_K), lambda i: (i, 0, 0), memory_space=pltpu.SMEM),
            pl.BlockSpec((1, 1, tm * TOP_K), lambda i: (jnp.minimum(i + 1, n_steps - 1), 0, 0),
                         memory_space=pltpu.SMEM),
            pl.BlockSpec(memory_space=pl.ANY),
            pl.BlockSpec((tm, LANES), lambda i: (i, 0)),
            pl.BlockSpec((tm, d), lambda i: (i, 0)),
            pl.BlockSpec((1, N_MOD, d), lambda i: (i // per_b, 0, 0)),
            pl.BlockSpec((1, d), lambda i: (0, 0)),
        ],
        out_specs=pl.BlockSpec((tm, d), lambda i: (i, 0)),
        out_shape=jax.ShapeDtypeStruct((t, d), F32),
        scratch_shapes=[
            pltpu.VMEM((2, TOP_K, tm * ROW_TILES, LANES), F32),
            pltpu.SemaphoreType.DMA((2,)),
        ],
        compiler_params=_params("arbitrary"),
        name="combine",
    )(pos3, pos3, y, gates, x1, mod3, post_gain)


def _pad_cols(w, n):
    return jnp.pad(w, ((0, 0), (0, n - w.shape[1])))


def kernel(x, c, w_ada, b_ada, attn_pre_gain, attn_post_gain, w_in, w_gk2, b_gk, gla_norm_gain, b_f,
           w_o, mlp_pre_gain, mlp_post_gain, w_router, b_router, w_e_in, b_e_in, w_e_out, b_e_out):
    batch, seq, d = x.shape
    t = batch * seq
    x2 = x.reshape(t, d)

    o = 0
    cols = {}
    for name, width in (("qg", GLA_QK), ("kg", GLA_QK), ("vg", GLA_WIDTH), ("gg", GLA_WIDTH),
                        ("zg", GLA_LOWRANK), ("qf", FOX_WIDTH), ("kf", FOX_WIDTH), ("vf", FOX_WIDTH),
                        ("ff", FOX_HEADS)):
        cols[name] = w_in[:, o:o + width]
        o += width
    gla_cols = 2 * GLA_QK + 2 * GLA_WIDTH
    fox_start = gla_cols + GLA_LOWRANK
    w_main = jnp.concatenate([w_in[:, :gla_cols], w_in[:, fox_start:fox_start + 2 * FOX_WIDTH]],
                             axis=1).astype(BF16)
    w_vt = cols["vf"].T.astype(BF16)
    w_small = _pad_cols(jnp.concatenate([cols["ff"], cols["zg"]], axis=1), LANES)
    ws_hi, ws_lo = _split2(w_small)
    wg = jnp.zeros((LANES, GLA_QK), F32).at[SMALL_ZG:SMALL_ZG + GLA_LOWRANK].set(w_gk2)
    wg_hi, wg_lo = _split2(wg)
    bf_pad = jnp.zeros((1, LANES), F32).at[0, SMALL_FF:SMALL_FF + FOX_HEADS].set(b_f)
    wr_hi, wr_lo = _split2(_pad_cols(w_router, LANES))
    br_pad = _pad_cols(b_router.reshape(1, N_EXPERTS), LANES)
    wa_hi, wa_lo = _split2(w_ada)

    c_pad = jnp.pad(c, ((0, SUBLANES - batch % SUBLANES if batch % SUBLANES else 0), (0, 0)))
    mod = _mod_call(c_pad, wa_hi, wa_lo, b_ada.reshape(1, -1))
    mod3 = mod[:batch].reshape(batch, N_MOD, d)

    proj, vt, small = _inproj_call(x2, mod3, attn_pre_gain.reshape(1, d), w_main, w_vt, ws_hi, ws_lo,
                                   seq)
    o_gla, kx, frow = _gla_call(proj, small, wg_hi, wg_lo, b_gk.reshape(1, -1),
                                gla_norm_gain.reshape(1, -1), bf_pad, batch, seq)
    o_fox = _fox_call(proj, kx, vt, frow, batch, seq)

    x1, h2, idx_w, rank_w, gate_w, cnt = _post_call(
        o_gla, o_fox, x2, mod3, attn_post_gain.reshape(1, d), mlp_pre_gain.reshape(1, d),
        w_o.astype(BF16), wr_hi, wr_lo, br_pad, seq)

    tm = MOE_TM
    n_tiles = (t * TOP_K) // tm + N_EXPERTS
    counts = cnt[0, :N_EXPERTS].astype(jnp.int32)
    padded = ((counts + tm - 1) // tm) * tm
    pend = jnp.cumsum(padded)
    pstart = pend - padded
    pos = pstart[idx_w[:, :TOP_K]] + rank_w[:, :TOP_K]
    n_used = (pend[-1] // tm).astype(jnp.int32).reshape(1)
    tile_start = jnp.arange(n_tiles, dtype=jnp.int32) * tm
    tile_expert = jnp.minimum(
        jnp.sum((pend[None, :] <= tile_start[:, None]).astype(jnp.int32), axis=1), N_EXPERTS - 1)

    pos_d = pos.reshape(t // DISP_TM, 1, DISP_TM * TOP_K)
    xs = _dispatch_call(pos_d, h2, jnp.zeros((n_tiles * tm * ROW_TILES, LANES), F32))
    y = _experts_call(tile_expert, n_used, xs, w_e_in, b_e_in.reshape(N_EXPERTS, 1, -1), w_e_out,
                      b_e_out.reshape(N_EXPERTS, 1, -1))
    pos_c = pos.reshape(t // COMB_TM, 1, COMB_TM * TOP_K)
    out = _combine_call(pos_c, y, gate_w, x1, mod3, mlp_post_gain.reshape(1, d), seq)
    return out.reshape(batch, seq, d)
```

```python
import functools

import jax
import jax.numpy as jnp
from jax import lax
from jax.experimental import pallas as pl
from jax.experimental.pallas import tpu as pltpu

F32 = jnp.float32
BF16 = jnp.bfloat16

D_MODEL = 1024
N_MOD = 6
GLA_HEADS = 4
GLA_DK = 64
GLA_DV = 128
GLA_QK = GLA_HEADS * GLA_DK
GLA_WIDTH = GLA_HEADS * GLA_DV
GLA_LOWRANK = 16
GLA_CHUNK = 64
FOX_HEADS = 8
FOX_DH = 64
FOX_WIDTH = FOX_HEADS * FOX_DH
N_EXPERTS = 32
TOP_K = 4
D_FF = D_MODEL
SWIGLU_ALPHA = 1.702
SWIGLU_LIMIT = 7.0
EPS = 1e-6

LANES = 128
SUBLANES = 8
ROW_TILES = D_MODEL // LANES

COL_QG, COL_KG, COL_VG, COL_GG = 0, 256, 512, 1024
COL_QF, COL_KF = 1536, 2048
MAIN_COLS = 2560
LOG2E = 1.4426950408889634
FOX_Q_SCALE = (FOX_DH ** -0.5) * LOG2E
FOX_BIAS_LANES = 6
BF16_SUBLANES = 16
FOX_VROWS = FOX_DH + BF16_SUBLANES
SMALL_FF, SMALL_ZG = 0, 8

NEG = -1e30
VMEM_LIMIT = 56 * 1024 * 1024


def _dot(a, b):
    return jnp.dot(a, b, preferred_element_type=F32)


def _dot_nt(a, b):
    return lax.dot_general(a, b, (((1,), (1,)), ((), ())), preferred_element_type=F32)


def _dot_tn(a, b):
    return lax.dot_general(a, b, (((0,), (0,)), ((), ())), preferred_element_type=F32)


def _split2(a):
    hi = a.astype(BF16)
    lo = (a - hi.astype(F32)).astype(BF16)
    return hi, lo


def _split3(a):
    hi = a.astype(BF16)
    r = a - hi.astype(F32)
    mid = r.astype(BF16)
    lo = (r - mid.astype(F32)).astype(BF16)
    return hi, mid, lo


def _log_sigmoid(x):
    return jnp.minimum(x, 0.0) - jnp.log1p(jnp.exp(-jnp.abs(x)))


def _rms(x):
    return x * lax.rsqrt(jnp.mean(x * x, axis=-1, keepdims=True) + EPS)


def _params(*sem):
    return pltpu.CompilerParams(dimension_semantics=sem, vmem_limit_bytes=VMEM_LIMIT)


def _mod_kernel(c_ref, wh_ref, wl_ref, b_ref, o_ref):
    c = c_ref[...]
    s = c * jax.nn.sigmoid(c)
    sh, sl = _split2(s)
    o_ref[...] = _dot(sh, wh_ref[...]) + (_dot(sh, wl_ref[...]) + _dot(sl, wh_ref[...])) + b_ref[...]


def _mod_call(c_pad, w_hi, w_lo, b_ada):
    rows, d = c_pad.shape
    n = w_hi.shape[1]
    tn = 1024
    return pl.pallas_call(
        _mod_kernel,
        grid=(n // tn,),
        in_specs=[
            pl.BlockSpec((rows, d), lambda j: (0, 0)),
            pl.BlockSpec((d, tn), lambda j: (0, j)),
            pl.BlockSpec((d, tn), lambda j: (0, j)),
            pl.BlockSpec((1, tn), lambda j: (0, j)),
        ],
        out_specs=pl.BlockSpec((rows, tn), lambda j: (0, j)),
        out_shape=jax.ShapeDtypeStruct((rows, n), F32),
        compiler_params=_params("parallel"),
        name="mod",
    )(c_pad, w_hi, w_lo, b_ada)


def _inproj_kernel(x_ref, mod_ref, g_ref, w_ref, wvt_ref, wsh_ref, wsl_ref, proj_ref, vt_ref, small_ref):
    h = _rms(x_ref[...]) * g_ref[...]
    h = h * (1.0 + mod_ref[0, 1:2, :]) + mod_ref[0, 0:1, :]
    hh, hl = _split2(h)
    cw = 512
    for j in range(MAIN_COLS // cw):
        r = _dot(hh, w_ref[:, j * cw:(j + 1) * cw])
        if j * cw == COL_QF:
            r = r * FOX_Q_SCALE
        proj_ref[:, j * cw:(j + 1) * cw] = r.astype(BF16)
    vt = _dot_nt(wvt_ref[...], hh).astype(BF16)
    ones = jnp.ones((FOX_VROWS - FOX_DH, vt.shape[1]), BF16)
    for h in range(FOX_HEADS):
        vt_ref[h * FOX_VROWS:h * FOX_VROWS + FOX_DH, :] = vt[h * FOX_DH:(h + 1) * FOX_DH, :]
        vt_ref[h * FOX_VROWS + FOX_DH:(h + 1) * FOX_VROWS, :] = ones
    small_ref[...] = _dot(hh, wsh_ref[...]) + (_dot(hh, wsl_ref[...]) + _dot(hl, wsh_ref[...]))


def _inproj_call(x2, mod3, gain, w_main, w_vt, ws_hi, ws_lo, seq):
    t, d = x2.shape
    tm = 512
    per_b = seq // tm
    return pl.pallas_call(
        _inproj_kernel,
        grid=(t // tm,),
        in_specs=[
            pl.BlockSpec((tm, d), lambda i: (i, 0)),
            pl.BlockSpec((1, N_MOD, d), lambda i: (i // per_b, 0, 0)),
            pl.BlockSpec((1, d), lambda i: (0, 0)),
            pl.BlockSpec((d, MAIN_COLS), lambda i: (0, 0)),
            pl.BlockSpec((FOX_WIDTH, d), lambda i: (0, 0)),
            pl.BlockSpec((d, LANES), lambda i: (0, 0)),
            pl.BlockSpec((d, LANES), lambda i: (0, 0)),
        ],
        out_specs=[
            pl.BlockSpec((tm, MAIN_COLS), lambda i: (i, 0)),
            pl.BlockSpec((FOX_HEADS * FOX_VROWS, tm), lambda i: (0, i)),
            pl.BlockSpec((tm, LANES), lambda i: (i, 0)),
        ],
        out_shape=[
            jax.ShapeDtypeStruct((t, MAIN_COLS), BF16),
            jax.ShapeDtypeStruct((FOX_HEADS * FOX_VROWS, t), BF16),
            jax.ShapeDtypeStruct((t, LANES), F32),
        ],
        compiler_params=_params("parallel"),
        name="inproj",
    )(x2, mod3, gain, w_main, w_vt, ws_hi, ws_lo)


GLA_TM = 512


def _gla_kernel(q_ref, k_ref, v_ref, g_ref, small_ref, wgh_ref, wgl_ref, bgk_ref, gain_ref, bf_ref,
                sel_ref, ones_ref, o_ref, kx_ref, frow_ref, st_ref, fcar_ref, oacc_ref):
    tm = GLA_TM
    nc = tm // GLA_CHUNK

    @pl.when(pl.program_id(1) == 0)
    def _():
        st_ref[...] = jnp.zeros_like(st_ref)
        fcar_ref[...] = jnp.zeros_like(fcar_ref)

    small = small_ref[...]
    sh, sl = _split2(small)
    gk = _dot(sh, wgh_ref[...]) + (_dot(sh, wgl_ref[...]) + _dot(sl, wgh_ref[...])) + bgk_ref[...]
    log_a = _log_sigmoid(gk) * (1.0 / 16.0)

    row = lax.broadcasted_iota(jnp.int32, (tm, tm), 0)
    col = lax.broadcasted_iota(jnp.int32, (tm, tm), 1)
    same = (row >> 6) == (col >> 6)
    lower = col <= row
    tri = jnp.logical_and(same, lower)
    tri_b = jnp.where(tri, 1.0, 0.0).astype(BF16)
    ones_b = jnp.where(same, 1.0, 0.0).astype(BF16)
    lower_b = jnp.where(lower, 1.0, 0.0).astype(BF16)

    lh, ll = _split2(log_a)
    b = _dot(tri_b, lh) + _dot(tri_b, ll)
    btot = _dot(ones_b, lh) + _dot(ones_b, ll)

    q = q_ref[...].astype(F32)
    k = k_ref[...].astype(F32)
    qt = q * jnp.exp(b) * (GLA_DK ** -0.5)
    kt = (k * jnp.exp(-b)).astype(BF16)
    kd = (k * jnp.exp(btot - b)).astype(BF16)

    lane_qk = lax.broadcasted_iota(jnp.int32, (1, GLA_QK), 1)
    head_masks = [jnp.logical_and(lane_qk >= h * GLA_DK, lane_qk < (h + 1) * GLA_DK)
                  for h in range(GLA_HEADS)]

    for h in range(GLA_HEADS):
        qm = jnp.where(head_masks[h], qt, 0.0).astype(BF16)
        att = _dot_nt(qm, kt)
        att = jnp.where(tri, att, 0.0).astype(BF16)
        oacc_ref[:, h * GLA_DV:(h + 1) * GLA_DV] = _dot(att, v_ref[:, h * GLA_DV:(h + 1) * GLA_DV])

    for c in range(nc):
        rs = slice(c * GLA_CHUNK, (c + 1) * GLA_CHUNK)
        st = st_ref[...]
        qc = qt[rs]
        qs = jnp.concatenate([jnp.where(head_masks[h], qc, 0.0) for h in range(GLA_HEADS)],
                             axis=0).astype(BF16)
        oi = _dot_nt(qs, st.astype(BF16))
        for h in range(GLA_HEADS):
            oacc_ref[rs, h * GLA_DV:(h + 1) * GLA_DV] += oi[h * GLA_CHUNK:(h + 1) * GLA_CHUNK, :]
        kvt = _dot_tn(v_ref[rs, :], kd[rs])
        decay = jnp.exp(btot[c * GLA_CHUNK:c * GLA_CHUNK + 1, :])
        new = decay * st
        for h in range(GLA_HEADS):
            new = new + jnp.where(head_masks[h], kvt[h * GLA_DV:(h + 1) * GLA_DV, :], 0.0)
        st_ref[...] = new

    for h in range(GLA_HEADS):
        cs = slice(h * GLA_DV, (h + 1) * GLA_DV)
        y = _rms(oacc_ref[:, cs]) * gain_ref[:, cs]
        g = g_ref[:, cs].astype(F32)
        o_ref[:, cs] = (y * (g * jax.nn.sigmoid(g))).astype(BF16)

    lane = lax.broadcasted_iota(jnp.int32, (1, LANES), 1)
    lf = jnp.where(lane < FOX_HEADS, _log_sigmoid(small + bf_ref[...]), 0.0)
    l1, l2, l3 = _split3(lf)
    fl = _dot(lower_b, l1) + (_dot(lower_b, l2) + _dot(lower_b, l3)) + fcar_ref[...]
    fcar_ref[...] = fl[tm - 1:tm, :]
    fs = fl * LOG2E
    fst = fs.T
    for h in range(FOX_HEADS):
        frow_ref[0, h] = fst[h:h + 1, :]
    n1, n2, n3 = _split3(-fs)
    kx = _dot(n1, sel_ref[0]) + (_dot(n2, sel_ref[1]) + _dot(n3, sel_ref[2])) + ones_ref[...]
    kx_ref[...] = kx.astype(BF16)


def _fox_bias_selectors():
    import numpy as np
    sel = np.zeros((3, LANES, FOX_WIDTH), np.float32)
    ones = np.zeros((1, FOX_WIDTH), np.float32)
    for h in range(FOX_HEADS):
        base = (h // 2) * LANES + (h % 2) * FOX_BIAS_LANES
        for i in range(3):
            sel[i, h, base + i] = 1.0
            ones[0, base + 3 + i] = 1.0
    return jnp.asarray(sel, BF16), jnp.asarray(ones, F32)


def _gla_call(proj, small, wg_hi, wg_lo, b_gk, gain, bf_pad, batch, seq):
    t = proj.shape[0]
    tm = GLA_TM
    nb = seq // tm

    def rows(b, i):
        return b * nb + i

    sel, ones = _fox_bias_selectors()
    return pl.pallas_call(
        _gla_kernel,
        grid=(batch, nb),
        in_specs=[
            pl.BlockSpec((tm, GLA_QK), lambda b, i: (rows(b, i), COL_QG // GLA_QK)),
            pl.BlockSpec((tm, GLA_QK), lambda b, i: (rows(b, i), COL_KG // GLA_QK)),
            pl.BlockSpec((tm, GLA_WIDTH), lambda b, i: (rows(b, i), COL_VG // GLA_WIDTH)),
            pl.BlockSpec((tm, GLA_WIDTH), lambda b, i: (rows(b, i), COL_GG // GLA_WIDTH)),
            pl.BlockSpec((tm, LANES), lambda b, i: (rows(b, i), 0)),
            pl.BlockSpec((LANES, GLA_QK), lambda b, i: (0, 0)),
            pl.BlockSpec((LANES, GLA_QK), lambda b, i: (0, 0)),
            pl.BlockSpec((1, GLA_QK), lambda b, i: (0, 0)),
            pl.BlockSpec((1, GLA_WIDTH), lambda b, i: (0, 0)),
            pl.BlockSpec((1, LANES), lambda b, i: (0, 0)),
            pl.BlockSpec((3, LANES, FOX_WIDTH), lambda b, i: (0, 0, 0)),
            pl.BlockSpec((1, FOX_WIDTH), lambda b, i: (0, 0)),
        ],
        out_specs=[
            pl.BlockSpec((tm, GLA_WIDTH), lambda b, i: (rows(b, i), 0)),
            pl.BlockSpec((tm, FOX_WIDTH), lambda b, i: (rows(b, i), 0)),
            pl.BlockSpec((1, FOX_HEADS, 1, tm), lambda b, i: (b, 0, 0, i)),
        ],
        out_shape=[
            jax.ShapeDtypeStruct((t, GLA_WIDTH), BF16),
            jax.ShapeDtypeStruct((t, FOX_WIDTH), BF16),
            jax.ShapeDtypeStruct((batch, FOX_HEADS, 1, seq), F32),
        ],
        scratch_shapes=[
            pltpu.VMEM((GLA_DV, GLA_QK), F32),
            pltpu.VMEM((1, LANES), F32),
            pltpu.VMEM((tm, GLA_WIDTH), F32),
        ],
        compiler_params=_params("parallel", "arbitrary"),
        name="gla",
    )(proj, proj, proj, proj, small, wg_hi, wg_lo, b_gk, gain, bf_pad, sel, ones)


FOX_TQ = 512
FOX_TK = 512


def _fox_kernel(q_ref, k_ref, kx_ref, vt_ref, frow_ref, o_ref, m_ref, acc_ref, sa_ref, sb_ref,
                ma_ref, mb_ref):
    tq, tk = FOX_TQ, FOX_TK
    i = pl.program_id(2)
    q0 = pl.multiple_of(i * tq, tq)
    lane = lax.broadcasted_iota(jnp.int32, (1, LANES), 1)
    q = q_ref[...]
    q_aug = []
    for hh in range(2):
        head_lanes = (lane < FOX_DH) if hh == 0 else (lane >= FOX_DH)
        qm = jnp.where(head_lanes, q, 0.0).astype(BF16)
        fr = jnp.broadcast_to(frow_ref[0, hh, :, pl.ds(q0, LANES)][:, 0:1], (1, LANES))
        f1, f2, f3 = _split3(fr)
        base = hh * FOX_BIAS_LANES
        qx = jnp.where(jnp.logical_and(lane >= base, lane < base + 3), 1.0, 0.0).astype(BF16)
        qx = jnp.where(lane == base + 3, f1, qx)
        qx = jnp.where(lane == base + 4, f2, qx)
        qx = jnp.where(lane == base + 5, f3, qx)
        q_aug.append(jnp.concatenate([qm, jnp.broadcast_to(qx, (tq, LANES))], axis=1))

    m_ref[...] = jnp.full(m_ref.shape, NEG, F32)
    acc_ref[...] = jnp.zeros_like(acc_ref)

    bufs = {"a": (sa_ref, ma_ref), "b": (sb_ref, mb_ref)}

    def produce(which, blk, hh, diagonal):
        buf, mx = bufs[which]
        k0 = pl.multiple_of(blk * tk, tk)
        k_aug = jnp.concatenate([k_ref[pl.ds(k0, tk), :], kx_ref[pl.ds(k0, tk), :]], axis=1)
        st = _dot_nt(k_aug, q_aug[hh])
        if diagonal:
            key = lax.broadcasted_iota(jnp.int32, (tk, tq), 0)
            qry = lax.broadcasted_iota(jnp.int32, (tk, tq), 1)
            st = jnp.where(key <= qry, st, NEG)
        buf[hh] = st
        mx[hh] = jnp.max(st, axis=0, keepdims=True)

    def consume(which, blk, hh):
        buf, mx = bufs[which]
        k0 = pl.multiple_of(blk * tk, tk)
        m_old = m_ref[hh]
        m_new = jnp.maximum(m_old, mx[hh])
        p = jnp.exp2(buf[hh] - m_new).astype(BF16)
        alpha = jnp.exp2(m_old - m_new)
        vt = vt_ref[hh * FOX_VROWS:(hh + 1) * FOX_VROWS, pl.ds(k0, tk)]
        acc_ref[hh] = alpha * acc_ref[hh] + _dot(vt, p)
        m_ref[hh] = m_new

    def stage(cur, cur_blk, nxt=None, nxt_blk=None, nxt_diagonal=False):
        for hh in range(2):
            if nxt is not None:
                produce(nxt, nxt_blk, hh, nxt_diagonal)
            consume(cur, cur_blk, hh)

    @pl.when(i == 0)
    def _():
        for hh in range(2):
            produce("a", 0, hh, True)
        stage("a", 0)

    @pl.when(i > 0)
    def _():
        for hh in range(2):
            produce("a", 0, hh, False)

    def pair(t, carry):
        b0 = 2 * t
        stage("a", b0, "b", b0 + 1)
        stage("b", b0 + 1, "a", b0 + 2)
        return carry

    n_pairs = lax.shift_right_logical(jnp.maximum(i - 1, 0), 1)
    lax.fori_loop(0, n_pairs, pair, 0)

    @pl.when((i & 1) == 1)
    def _():
        stage("a", i - 1, "b", i, True)
        stage("b", i)

    @pl.when(jnp.logical_and((i & 1) == 0, i > 0))
    def _():
        stage("a", i - 2, "b", i - 1)
        stage("b", i - 1, "a", i, True)
        stage("a", i)

    outs = []
    for hh in range(2):
        acc = acc_ref[hh]
        outs.append(acc[:FOX_DH] * (1.0 / acc[FOX_DH:FOX_DH + 1]))
    o_ref[...] = jnp.concatenate(outs, axis=0).T.astype(BF16)


def _fox_call(proj, kx, vt, frow, batch, seq):
    t = proj.shape[0]
    tq = FOX_TQ
    nq = seq // tq
    hpairs = FOX_WIDTH // LANES
    return pl.pallas_call(
        _fox_kernel,
        grid=(batch, hpairs, nq),
        in_specs=[
            pl.BlockSpec((tq, LANES), lambda b, hp, i: (b * nq + i, COL_QF // LANES + hp)),
            pl.BlockSpec((seq, LANES), lambda b, hp, i: (b, COL_KF // LANES + hp)),
            pl.BlockSpec((seq, LANES), lambda b, hp, i: (b, hp)),
            pl.BlockSpec((2 * FOX_VROWS, seq), lambda b, hp, i: (hp, b)),
            pl.BlockSpec((1, 2, 1, seq), lambda b, hp, i: (b, hp, 0, 0)),
        ],
        out_specs=pl.BlockSpec((tq, LANES), lambda b, hp, i: (b * nq + i, hp)),
        out_shape=jax.ShapeDtypeStruct((t, FOX_WIDTH), BF16),
        scratch_shapes=[
            pltpu.VMEM((2, 1, tq), F32),
            pltpu.VMEM((2, FOX_VROWS, tq), F32),
            pltpu.VMEM((2, FOX_TK, tq), F32),
            pltpu.VMEM((2, FOX_TK, tq), F32),
            pltpu.VMEM((2, 1, tq), F32),
            pltpu.VMEM((2, 1, tq), F32),
        ],
        compiler_params=_params("parallel", "parallel", "arbitrary"),
        name="fox",
    )(proj, proj, kx, vt, frow)


POST_TM = 512


def _post_kernel(og_ref, of_ref, x_ref, mod_ref, pg_ref, mg_ref, wo_ref, wrh_ref, wrl_ref, br_ref,
                 x1_ref, h2_ref, idx_ref, rank_ref, gate_ref, cnt_ref, car_ref):
    tm = POST_TM

    @pl.when(pl.program_id(0) == 0)
    def _():
        car_ref[...] = jnp.zeros_like(car_ref)

    y = _dot(og_ref[...], wo_ref[:GLA_WIDTH, :]) + _dot(of_ref[...], wo_ref[GLA_WIDTH:, :])
    x1 = x_ref[...] + mod_ref[0, 2:3, :] * (_rms(y) * pg_ref[...])
    x1_ref[...] = x1
    h2 = _rms(x1) * mg_ref[...]
    h2 = h2 * (1.0 + mod_ref[0, 4:5, :]) + mod_ref[0, 3:4, :]
    for j in range(ROW_TILES):
        h2_ref[pl.ds(j, tm, stride=ROW_TILES), :] = h2[:, j * LANES:(j + 1) * LANES]

    hh, hl = _split2(h2)
    logits = _dot(hh, wrh_ref[...]) + (_dot(hh, wrl_ref[...]) + _dot(hl, wrh_ref[...])) + br_ref[...]
    lane = lax.broadcasted_iota(jnp.int32, (tm, LANES), 1).astype(F32)
    work = jnp.where(lane < N_EXPERTS, logits, -jnp.inf)
    vals, idxs = [], []
    for _ in range(TOP_K):
        mx = jnp.max(work, axis=-1, keepdims=True)
        ix = jnp.min(jnp.where(work == mx, lane, float(LANES)), axis=-1, keepdims=True)
        vals.append(mx)
        idxs.append(ix)
        work = jnp.where(lane == ix, -jnp.inf, work)
    es = [jnp.exp(v - vals[0]) for v in vals]
    den = es[0] + es[1] + es[2] + es[3]

    onehot = jnp.zeros((tm, LANES), F32)
    for ix in idxs:
        onehot = onehot + jnp.where(lane == ix, 1.0, 0.0)
    row = lax.broadcasted_iota(jnp.int32, (tm, tm), 0)
    col = lax.broadcasted_iota(jnp.int32, (tm, tm), 1)
    strict_b = jnp.where(col < row, 1.0, 0.0).astype(BF16)
    prefix = _dot(strict_b, onehot.astype(BF16)) + car_ref[...]
    car_new = car_ref[...] + jnp.sum(onehot, axis=0, keepdims=True)
    car_ref[...] = car_new
    cnt_ref[...] = car_new

    idx_o = jnp.zeros((tm, LANES), F32)
    rank_o = jnp.zeros((tm, LANES), F32)
    gate_o = jnp.zeros((tm, LANES), F32)
    for kk in range(TOP_K):
        rk = jnp.sum(jnp.where(lane == idxs[kk], prefix, 0.0), axis=-1, keepdims=True)
        sel = lane == float(kk)
        idx_o = jnp.where(sel, idxs[kk], idx_o)
        rank_o = jnp.where(sel, rk, rank_o)
        gate_o = jnp.where(sel, es[kk] / den, gate_o)
    idx_ref[...] = idx_o.astype(jnp.int32)
    rank_ref[...] = rank_o.astype(jnp.int32)
    gate_ref[...] = gate_o


def _post_call(o_gla, o_fox, x2, mod3, post_gain, mlp_gain, w_o, wr_hi, wr_lo, br_pad, seq):
    t, d = x2.shape
    tm = POST_TM
    per_b = seq // tm
    return pl.pallas_call(
        _post_kernel,
        grid=(t // tm,),
        in_specs=[
            pl.BlockSpec((tm, GLA_WIDTH), lambda i: (i, 0)),
            pl.BlockSpec((tm, FOX_WIDTH), lambda i: (i, 0)),
            pl.BlockSpec((tm, d), lambda i: (i, 0)),
            pl.BlockSpec((1, N_MOD, d), lambda i: (i // per_b, 0, 0)),
            pl.BlockSpec((1, d), lambda i: (0, 0)),
            pl.BlockSpec((1, d), lambda i: (0, 0)),
            pl.BlockSpec((GLA_WIDTH + FOX_WIDTH, d), lambda i: (0, 0)),
            pl.BlockSpec((d, LANES), lambda i: (0, 0)),
            pl.BlockSpec((d, LANES), lambda i: (0, 0)),
            pl.BlockSpec((1, LANES), lambda i: (0, 0)),
        ],
        out_specs=[
            pl.BlockSpec((tm, d), lambda i: (i, 0)),
            pl.BlockSpec((tm * ROW_TILES, LANES), lambda i: (i, 0)),
            pl.BlockSpec((tm, LANES), lambda i: (i, 0)),
            pl.BlockSpec((tm, LANES), lambda i: (i, 0)),
            pl.BlockSpec((tm, LANES), lambda i: (i, 0)),
            pl.BlockSpec((1, LANES), lambda i: (0, 0)),
        ],
        out_shape=[
            jax.ShapeDtypeStruct((t, d), F32),
            jax.ShapeDtypeStruct((t * ROW_TILES, LANES), F32),
            jax.ShapeDtypeStruct((t, LANES), jnp.int32),
            jax.ShapeDtypeStruct((t, LANES), jnp.int32),
            jax.ShapeDtypeStruct((t, LANES), F32),
            jax.ShapeDtypeStruct((1, LANES), F32),
        ],
        scratch_shapes=[pltpu.VMEM((1, LANES), F32)],
        compiler_params=_params("arbitrary"),
        name="post",
    )(o_gla, o_fox, x2, mod3, post_gain, mlp_gain, w_o, wr_hi, wr_lo, br_pad)


DISP_TM = 256


def _row_copy(src_ref, src_row, dst_ref, dst_row, sem):
    return pltpu.make_async_copy(
        src_ref.at[pl.ds(pl.multiple_of(src_row * ROW_TILES, ROW_TILES), ROW_TILES)],
        dst_ref.at[pl.ds(pl.multiple_of(dst_row * ROW_TILES, ROW_TILES), ROW_TILES)],
        sem)


def _dispatch_kernel(pend_ref, pos_ref, h_ref, xs_ref, zero_ref, sem):
    tm = DISP_TM
    zrows = MOE_TM * ROW_TILES

    @pl.when(pl.program_id(0) == 0)
    def _():
        zero_ref[...] = jnp.zeros_like(zero_ref)

        def last_tile(e):
            start = pl.multiple_of((pend_ref[e] - MOE_TM) * ROW_TILES, zrows)
            return pltpu.make_async_copy(zero_ref, xs_ref.at[pl.ds(start, zrows)], sem)

        def nonempty(e):
            return pend_ref[e] > jnp.where(e == 0, 0, pend_ref[jnp.maximum(e - 1, 0)])

        def clear(e, carry):
            @pl.when(nonempty(e))
            def _():
                last_tile(e).start()
            return carry

        def clear_wait(e, carry):
            @pl.when(nonempty(e))
            def _():
                last_tile(e).wait()
            return carry

        lax.fori_loop(0, N_EXPERTS, clear, 0)
        lax.fori_loop(0, N_EXPERTS, clear_wait, 0)

        def tail_tile(j):
            return pltpu.make_async_copy(
                zero_ref, xs_ref.at[pl.ds(pl.multiple_of(j * zrows, zrows), zrows)], sem)

        def tail(j, carry):
            tail_tile(j).start()
            return carry

        def tail_wait(j, carry):
            tail_tile(j).wait()
            return carry

        first_unused = lax.div(pend_ref[N_EXPERTS - 1], MOE_TM)
        n_tiles = xs_ref.shape[0] // zrows
        lax.fori_loop(first_unused, n_tiles, tail, 0)
        lax.fori_loop(first_unused, n_tiles, tail_wait, 0)

    def issue(t, carry):
        for kk in range(TOP_K):
            _row_copy(h_ref, t, xs_ref, pos_ref[0, 0, t * TOP_K + kk], sem).start(priority=kk % 2)
        return carry

    lax.fori_loop(0, tm, issue, 0)
    for _ in range(TOP_K):
        pltpu.make_async_copy(h_ref, xs_ref.at[pl.ds(0, tm * ROW_TILES)], sem).wait()


def _dispatch_call(pend, pos3, h2, n_tiles):
    tm = DISP_TM
    n_steps = pos3.shape[0]
    grid_spec = pltpu.PrefetchScalarGridSpec(
        num_scalar_prefetch=1,
        grid=(n_steps,),
        in_specs=[
            pl.BlockSpec((1, 1, tm * TOP_K), lambda i, pend: (i, 0, 0), memory_space=pltpu.SMEM),
            pl.BlockSpec((tm * ROW_TILES, LANES), lambda i, pend: (i, 0)),
        ],
        out_specs=pl.BlockSpec(memory_space=pl.ANY),
        scratch_shapes=[
            pltpu.VMEM((MOE_TM * ROW_TILES, LANES), F32),
            pltpu.SemaphoreType.DMA,
        ],
    )
    return pl.pallas_call(
        _dispatch_kernel,
        grid_spec=grid_spec,
        out_shape=jax.ShapeDtypeStruct((n_tiles * MOE_TM * ROW_TILES, LANES), F32),
        compiler_params=_params("arbitrary"),
        name="dispatch",
    )(pend, pos3, h2)


MOE_TM = 512


def _experts_kernel(te_ref, nu_ref, x_ref, win_ref, bin_ref, wout_ref, bout_ref, y_ref,
                    wib_ref, wob_ref):
    tm = MOE_TM
    i = pl.program_id(0)
    used = i < nu_ref[0]

    @pl.when(jnp.logical_not(used))
    def _():
        y_ref[...] = jnp.zeros_like(y_ref)

    fresh = jnp.logical_or(i == 0, te_ref[i] != te_ref[jnp.maximum(i - 1, 0)])

    @pl.when(jnp.logical_and(used, fresh))
    def _():
        wib_ref[...] = win_ref[0].astype(BF16)
        wob_ref[...] = wout_ref[0].astype(BF16)

    @pl.when(used)
    def _():
        x = jnp.concatenate([x_ref[pl.ds(j, tm, stride=ROW_TILES), :] for j in range(ROW_TILES)],
                            axis=1).astype(BF16)
        u = _dot(x, wib_ref[...]) + bin_ref[0]
        glu = jnp.minimum(u[:, :D_FF], SWIGLU_LIMIT)
        lin = jnp.clip(u[:, D_FF:], -SWIGLU_LIMIT, SWIGLU_LIMIT)
        a = glu * jax.nn.sigmoid(SWIGLU_ALPHA * glu) * (lin + 1.0)
        y = _dot(a.astype(BF16), wob_ref[...]) + bout_ref[0]
        for j in range(ROW_TILES):
            y_ref[pl.ds(j, tm, stride=ROW_TILES), :] = y[:, j * LANES:(j + 1) * LANES]


def _experts_call(tile_expert, n_used, xs, w_in, b_in, w_out, b_out):
    tm = MOE_TM
    n_tiles = tile_expert.shape[0]
    d = D_MODEL

    def tile(i, te, nu):
        return (jnp.minimum(i, nu[0] - 1), 0)

    grid_spec = pltpu.PrefetchScalarGridSpec(
        num_scalar_prefetch=2,
        grid=(n_tiles,),
        in_specs=[
            pl.BlockSpec((tm * ROW_TILES, LANES), tile),
            pl.BlockSpec((1, d, 2 * D_FF), lambda i, te, nu: (te[i], 0, 0)),
            pl.BlockSpec((1, 1, 2 * D_FF), lambda i, te, nu: (te[i], 0, 0)),
            pl.BlockSpec((1, D_FF, d), lambda i, te, nu: (te[i], 0, 0)),
            pl.BlockSpec((1, 1, d), lambda i, te, nu: (te[i], 0, 0)),
        ],
        out_specs=pl.BlockSpec((tm * ROW_TILES, LANES), lambda i, te, nu: (i, 0)),
        scratch_shapes=[
            pltpu.VMEM((d, 2 * D_FF), BF16),
            pltpu.VMEM((D_FF, d), BF16),
        ],
    )
    return pl.pallas_call(
        _experts_kernel,
        grid_spec=grid_spec,
        out_shape=jax.ShapeDtypeStruct(xs.shape, F32),
        compiler_params=_params("arbitrary"),
        name="experts",
    )(tile_expert, n_used, xs, w_in, b_in, w_out, b_out)


COMB_TM = 256


def _combine_kernel(pos_ref, posn_ref, y_ref, gate_ref, x1_ref, mod_ref, pg_ref, o_ref, buf_ref, sem):
    tm = COMB_TM
    i = pl.program_id(0)
    slot = i & 1

    def gather(p_ref, s):
        def issue(t, carry):
            for kk in range(TOP_K):
                _row_copy(y_ref, p_ref[0, 0, t * TOP_K + kk], buf_ref.at[s, kk], t,
                          sem.at[s]).start(priority=kk % 2)
            return carry

        lax.fori_loop(0, tm, issue, 0)

    @pl.when(i == 0)
    def _():
        gather(pos_ref, 0)

    @pl.when(i + 1 < pl.num_programs(0))
    def _():
        gather(posn_ref, 1 - slot)

    for kk in range(TOP_K):
        pltpu.make_async_copy(y_ref.at[pl.ds(0, tm * ROW_TILES)], buf_ref.at[slot, kk],
                              sem.at[slot]).wait()

    gates = gate_ref[...]
    acc = None
    for kk in range(TOP_K):
        rows = jnp.concatenate(
            [buf_ref[slot, kk, pl.ds(j, tm, stride=ROW_TILES), :] for j in range(ROW_TILES)], axis=1)
        term = rows * gates[:, kk:kk + 1]
        acc = term if acc is None else acc + term
    o_ref[...] = x1_ref[...] + mod_ref[0, 5:6, :] * (_rms(acc) * pg_ref[...])


def _combine_call(pos3, y, gates, x1, mod3, post_gain, seq):
    t, d = x1.shape
    tm = COMB_TM
    per_b = seq // tm
    n_steps = t // tm
    return pl.pallas_call(
        _combine_kernel,
        grid=(n_steps,),
        in_specs=[
            pl.BlockSpec((1, 1, tm * TOP_K), lambda i: (i, 0, 0), memory_space=pltpu.SMEM),
            pl.BlockSpec((1, 1, tm * TOP_K), lambda i: (jnp.minimum(i + 1, n_steps - 1), 0, 0),
                         memory_space=pltpu.SMEM),
            pl.BlockSpec(memory_space=pl.ANY),
            pl.BlockSpec((tm, LANES), lambda i: (i, 0)),
            pl.BlockSpec((tm, d), lambda i: (i, 0)),
            pl.BlockSpec((1, N_MOD, d), lambda i: (i // per_b, 0, 0)),
            pl.BlockSpec((1, d), lambda i: (0, 0)),
        ],
        out_specs=pl.BlockSpec((tm, d), lambda i: (i, 0)),
        out_shape=jax.ShapeDtypeStruct((t, d), F32),
        scratch_shapes=[
            pltpu.VMEM((2, TOP_K, tm * ROW_TILES, LANES), F32),
            pltpu.SemaphoreType.DMA((2,)),
        ],
        compiler_params=_params("arbitrary"),
        name="combine",
    )(pos3, pos3, y, gates, x1, mod3, post_gain)


def _pad_cols(w, n):
    return jnp.pad(w, ((0, 0), (0, n - w.shape[1])))


def kernel(x, c, w_ada, b_ada, attn_pre_gain, attn_post_gain, w_in, w_gk2, b_gk, gla_norm_gain, b_f,
           w_o, mlp_pre_gain, mlp_post_gain, w_router, b_router, w_e_in, b_e_in, w_e_out, b_e_out):
    batch, seq, d = x.shape
    t = batch * seq
    x2 = x.reshape(t, d)

    o = 0
    cols = {}
    for name, width in (("qg", GLA_QK), ("kg", GLA_QK), ("vg", GLA_WIDTH), ("gg", GLA_WIDTH),
                        ("zg", GLA_LOWRANK), ("qf", FOX_WIDTH), ("kf", FOX_WIDTH), ("vf", FOX_WIDTH),
                        ("ff", FOX_HEADS)):
        cols[name] = w_in[:, o:o + width]
        o += width
    gla_cols = 2 * GLA_QK + 2 * GLA_WIDTH
    fox_start = gla_cols + GLA_LOWRANK
    w_main = jnp.concatenate([w_in[:, :gla_cols], w_in[:, fox_start:fox_start + 2 * FOX_WIDTH]],
                             axis=1).astype(BF16)
    w_vt = cols["vf"].T.astype(BF16)
    w_small = _pad_cols(jnp.concatenate([cols["ff"], cols["zg"]], axis=1), LANES)
    ws_hi, ws_lo = _split2(w_small)
    wg = jnp.zeros((LANES, GLA_QK), F32).at[SMALL_ZG:SMALL_ZG + GLA_LOWRANK].set(w_gk2)
    wg_hi, wg_lo = _split2(wg)
    bf_pad = jnp.zeros((1, LANES), F32).at[0, SMALL_FF:SMALL_FF + FOX_HEADS].set(b_f)
    wr_hi, wr_lo = _split2(_pad_cols(w_router, LANES))
    br_pad = _pad_cols(b_router.reshape(1, N_EXPERTS), LANES)
    wa_hi, wa_lo = _split2(w_ada)

    c_pad = jnp.pad(c, ((0, SUBLANES - batch % SUBLANES if batch % SUBLANES else 0), (0, 0)))
    mod = _mod_call(c_pad, wa_hi, wa_lo, b_ada.reshape(1, -1))
    mod3 = mod[:batch].reshape(batch, N_MOD, d)

    proj, vt, small = _inproj_call(x2, mod3, attn_pre_gain.reshape(1, d), w_main, w_vt, ws_hi, ws_lo,
                                   seq)
    o_gla, kx, frow = _gla_call(proj, small, wg_hi, wg_lo, b_gk.reshape(1, -1),
                                gla_norm_gain.reshape(1, -1), bf_pad, batch, seq)
    o_fox = _fox_call(proj, kx, vt, frow, batch, seq)

    x1, h2, idx_w, rank_w, gate_w, cnt = _post_call(
        o_gla, o_fox, x2, mod3, attn_post_gain.reshape(1, d), mlp_pre_gain.reshape(1, d),
        w_o.astype(BF16), wr_hi, wr_lo, br_pad, seq)

    tm = MOE_TM
    n_tiles = (t * TOP_K) // tm + N_EXPERTS
    counts = cnt[0, :N_EXPERTS].astype(jnp.int32)
    padded = ((counts + tm - 1) // tm) * tm
    pend = jnp.cumsum(padded)
    pstart = pend - padded
    pos = pstart[idx_w[:, :TOP_K]] + rank_w[:, :TOP_K]
    n_used = (pend[-1] // tm).astype(jnp.int32).reshape(1)
    tile_start = jnp.arange(n_tiles, dtype=jnp.int32) * tm
    tile_expert = jnp.minimum(
        jnp.sum((pend[None, :] <= tile_start[:, None]).astype(jnp.int32), axis=1), N_EXPERTS - 1)

    pos_d = pos.reshape(t // DISP_TM, 1, DISP_TM * TOP_K)
    xs = _dispatch_call(pend.astype(jnp.int32), pos_d, h2, n_tiles)
    y = _experts_call(tile_expert, n_used, xs, w_e_in, b_e_in.reshape(N_EXPERTS, 1, -1), w_e_out,
                      b_e_out.reshape(N_EXPERTS, 1, -1))
    pos_c = pos.reshape(t // COMB_TM, 1, COMB_TM * TOP_K)
    out = _combine_call(pos_c, y, gate_w, x1, mod3, mlp_post_gain.reshape(1, d), seq)
    return out.reshape(batch, seq, d)
```

```python
import functools

import jax
import jax.numpy as jnp
from jax import lax
from jax.experimental import pallas as pl
from jax.experimental.pallas import tpu as pltpu

F32 = jnp.float32
BF16 = jnp.bfloat16

D_MODEL = 1024
N_MOD = 6
GLA_HEADS = 4
GLA_DK = 64
GLA_DV = 128
GLA_QK = GLA_HEADS * GLA_DK
GLA_WIDTH = GLA_HEADS * GLA_DV
GLA_LOWRANK = 16
GLA_CHUNK = 64
FOX_HEADS = 8
FOX_DH = 64
FOX_WIDTH = FOX_HEADS * FOX_DH
N_EXPERTS = 32
TOP_K = 4
D_FF = D_MODEL
SWIGLU_ALPHA = 1.702
SWIGLU_LIMIT = 7.0
EPS = 1e-6

LANES = 128
SUBLANES = 8
ROW_TILES = D_MODEL // LANES

COL_QG, COL_KG, COL_VG, COL_GG = 0, 256, 512, 1024
COL_QF, COL_KF = 1536, 2048
MAIN_COLS = 2560
LOG2E = 1.4426950408889634
FOX_Q_SCALE = (FOX_DH ** -0.5) * LOG2E
FOX_BIAS_LANES = 6
BF16_SUBLANES = 16
FOX_VROWS = FOX_DH + BF16_SUBLANES
SMALL_FF, SMALL_ZG = 0, 8

NEG = -1e30
VMEM_LIMIT = 56 * 1024 * 1024


def _dot(a, b):
    return jnp.dot(a, b, preferred_element_type=F32)


def _dot_nt(a, b):
    return lax.dot_general(a, b, (((1,), (1,)), ((), ())), preferred_element_type=F32)


def _dot_tn(a, b):
    return lax.dot_general(a, b, (((0,), (0,)), ((), ())), preferred_element_type=F32)


def _split2(a):
    hi = a.astype(BF16)
    lo = (a - hi.astype(F32)).astype(BF16)
    return hi, lo


def _split3(a):
    hi = a.astype(BF16)
    r = a - hi.astype(F32)
    mid = r.astype(BF16)
    lo = (r - mid.astype(F32)).astype(BF16)
    return hi, mid, lo


def _log_sigmoid(x):
    return jnp.minimum(x, 0.0) - jnp.log1p(jnp.exp(-jnp.abs(x)))


def _rms(x):
    return x * lax.rsqrt(jnp.mean(x * x, axis=-1, keepdims=True) + EPS)


def _params(*sem, flags=None):
    return pltpu.CompilerParams(dimension_semantics=sem, vmem_limit_bytes=VMEM_LIMIT, flags=flags)


def _mod_kernel(c_ref, wh_ref, wl_ref, b_ref, o_ref):
    c = c_ref[...]
    s = c * jax.nn.sigmoid(c)
    sh, sl = _split2(s)
    o_ref[...] = _dot(sh, wh_ref[...]) + (_dot(sh, wl_ref[...]) + _dot(sl, wh_ref[...])) + b_ref[...]


def _mod_call(c_pad, w_hi, w_lo, b_ada):
    rows, d = c_pad.shape
    n = w_hi.shape[1]
    tn = 1024
    return pl.pallas_call(
        _mod_kernel,
        grid=(n // tn,),
        in_specs=[
            pl.BlockSpec((rows, d), lambda j: (0, 0)),
            pl.BlockSpec((d, tn), lambda j: (0, j)),
            pl.BlockSpec((d, tn), lambda j: (0, j)),
            pl.BlockSpec((1, tn), lambda j: (0, j)),
        ],
        out_specs=pl.BlockSpec((rows, tn), lambda j: (0, j)),
        out_shape=jax.ShapeDtypeStruct((rows, n), F32),
        compiler_params=_params("parallel"),
        name="mod",
    )(c_pad, w_hi, w_lo, b_ada)


def _inproj_kernel(x_ref, mod_ref, g_ref, w_ref, wvt_ref, wsh_ref, wsl_ref, proj_ref, vt_ref, small_ref):
    h = _rms(x_ref[...]) * g_ref[...]
    h = h * (1.0 + mod_ref[0, 1:2, :]) + mod_ref[0, 0:1, :]
    hh, hl = _split2(h)
    cw = 512
    for j in range(MAIN_COLS // cw):
        r = _dot(hh, w_ref[:, j * cw:(j + 1) * cw])
        if j * cw == COL_QF:
            r = r * FOX_Q_SCALE
        proj_ref[:, j * cw:(j + 1) * cw] = r.astype(BF16)
    vt = _dot_nt(wvt_ref[...], hh).astype(BF16)
    ones = jnp.ones((FOX_VROWS - FOX_DH, vt.shape[1]), BF16)
    for h in range(FOX_HEADS):
        vt_ref[h * FOX_VROWS:h * FOX_VROWS + FOX_DH, :] = vt[h * FOX_DH:(h + 1) * FOX_DH, :]
        vt_ref[h * FOX_VROWS + FOX_DH:(h + 1) * FOX_VROWS, :] = ones
    a = _dot(hh, wsl_ref[...])
    small_ref[...] = a[:, :LANES] + (a[:, LANES:] + _dot(hl, wsh_ref[...]))


def _inproj_call(x2, mod3, gain, w_main, w_vt, ws_hi, ws_lo, seq):
    ws_hilo = jnp.concatenate([ws_hi, ws_lo], axis=1)
    t, d = x2.shape
    tm = 512
    per_b = seq // tm
    return pl.pallas_call(
        _inproj_kernel,
        grid=(t // tm,),
        in_specs=[
            pl.BlockSpec((tm, d), lambda i: (i, 0)),
            pl.BlockSpec((1, N_MOD, d), lambda i: (i // per_b, 0, 0)),
            pl.BlockSpec((1, d), lambda i: (0, 0)),
            pl.BlockSpec((d, MAIN_COLS), lambda i: (0, 0)),
            pl.BlockSpec((FOX_WIDTH, d), lambda i: (0, 0)),
            pl.BlockSpec((d, LANES), lambda i: (0, 0)),
            pl.BlockSpec((d, 2 * LANES), lambda i: (0, 0)),
        ],
        out_specs=[
            pl.BlockSpec((tm, MAIN_COLS), lambda i: (i, 0)),
            pl.BlockSpec((FOX_HEADS * FOX_VROWS, tm), lambda i: (0, i)),
            pl.BlockSpec((tm, LANES), lambda i: (i, 0)),
        ],
        out_shape=[
            jax.ShapeDtypeStruct((t, MAIN_COLS), BF16),
            jax.ShapeDtypeStruct((FOX_HEADS * FOX_VROWS, t), BF16),
            jax.ShapeDtypeStruct((t, LANES), F32),
        ],
        compiler_params=_params("parallel"),
        name="inproj",
    )(x2, mod3, gain, w_main, w_vt, ws_hi, ws_hilo)


GLA_TM = 512


def _gla_kernel(q_ref, k_ref, v_ref, g_ref, small_ref, wgh_ref, wgl_ref, bgk_ref, gain_ref, bf_ref,
                sel_ref, ones_ref, o_ref, kx_ref, frow_ref, st_ref, fcar_ref, oacc_ref):
    tm = GLA_TM
    nc = tm // GLA_CHUNK

    @pl.when(pl.program_id(1) == 0)
    def _():
        st_ref[...] = jnp.zeros_like(st_ref)
        fcar_ref[...] = jnp.zeros_like(fcar_ref)

    small = small_ref[...]
    sh, sl = _split2(small)
    gk = _dot(sh, wgh_ref[...]) + (_dot(sh, wgl_ref[...]) + _dot(sl, wgh_ref[...])) + bgk_ref[...]
    log_a = _log_sigmoid(gk) * (1.0 / 16.0)

    row = lax.broadcasted_iota(jnp.int32, (tm, tm), 0)
    col = lax.broadcasted_iota(jnp.int32, (tm, tm), 1)
    same = (row >> 6) == (col >> 6)
    lower = col <= row
    tri = jnp.logical_and(same, lower)
    tri_b = jnp.where(tri, 1.0, 0.0).astype(BF16)
    ones_b = jnp.where(same, 1.0, 0.0).astype(BF16)
    lower_b = jnp.where(lower, 1.0, 0.0).astype(BF16)

    lh, ll = _split2(log_a)
    b = _dot(tri_b, lh) + _dot(tri_b, ll)
    btot = _dot(ones_b, lh) + _dot(ones_b, ll)

    q = q_ref[...].astype(F32)
    k = k_ref[...].astype(F32)
    qt = q * jnp.exp(b) * (GLA_DK ** -0.5)
    kt = (k * jnp.exp(-b)).astype(BF16)
    kd = (k * jnp.exp(btot - b)).astype(BF16)

    lane_qk = lax.broadcasted_iota(jnp.int32, (1, GLA_QK), 1)
    head_masks = [jnp.logical_and(lane_qk >= h * GLA_DK, lane_qk < (h + 1) * GLA_DK)
                  for h in range(GLA_HEADS)]

    for h in range(GLA_HEADS):
        qm = jnp.where(head_masks[h], qt, 0.0).astype(BF16)
        att = _dot_nt(qm, kt)
        att = jnp.where(tri, att, 0.0).astype(BF16)
        oacc_ref[:, h * GLA_DV:(h + 1) * GLA_DV] = _dot(att, v_ref[:, h * GLA_DV:(h + 1) * GLA_DV])

    for c in range(nc):
        rs = slice(c * GLA_CHUNK, (c + 1) * GLA_CHUNK)
        st = st_ref[...]
        qc = qt[rs]
        qs = jnp.concatenate([jnp.where(head_masks[h], qc, 0.0) for h in range(GLA_HEADS)],
                             axis=0).astype(BF16)
        oi = _dot_nt(qs, st.astype(BF16))
        for h in range(GLA_HEADS):
            oacc_ref[rs, h * GLA_DV:(h + 1) * GLA_DV] += oi[h * GLA_CHUNK:(h + 1) * GLA_CHUNK, :]
        kvt = _dot_tn(v_ref[rs, :], kd[rs])
        decay = jnp.exp(btot[c * GLA_CHUNK:c * GLA_CHUNK + 1, :])
        new = decay * st
        for h in range(GLA_HEADS):
            new = new + jnp.where(head_masks[h], kvt[h * GLA_DV:(h + 1) * GLA_DV, :], 0.0)
        st_ref[...] = new

    for h in range(GLA_HEADS):
        cs = slice(h * GLA_DV, (h + 1) * GLA_DV)
        y = _rms(oacc_ref[:, cs]) * gain_ref[:, cs]
        g = g_ref[:, cs].astype(F32)
        o_ref[:, cs] = (y * (g * jax.nn.sigmoid(g))).astype(BF16)

    lane = lax.broadcasted_iota(jnp.int32, (1, LANES), 1)
    lf = jnp.where(lane < FOX_HEADS, _log_sigmoid(small + bf_ref[...]), 0.0)
    l1, l2, l3 = _split3(lf)
    fl = _dot(lower_b, l1) + (_dot(lower_b, l2) + _dot(lower_b, l3)) + fcar_ref[...]
    fcar_ref[...] = fl[tm - 1:tm, :]
    fs = fl * LOG2E
    fst = fs.T
    for h in range(FOX_HEADS):
        frow_ref[0, h] = fst[h:h + 1, :]
    n1, n2, n3 = _split3(-fs)
    kx = _dot(n1, sel_ref[0]) + (_dot(n2, sel_ref[1]) + _dot(n3, sel_ref[2])) + ones_ref[...]
    kx_ref[...] = kx.astype(BF16)


def _fox_bias_selectors():
    import numpy as np
    sel = np.zeros((3, LANES, FOX_WIDTH), np.float32)
    ones = np.zeros((1, FOX_WIDTH), np.float32)
    for h in range(FOX_HEADS):
        base = (h // 2) * LANES + (h % 2) * FOX_BIAS_LANES
        for i in range(3):
            sel[i, h, base + i] = 1.0
            ones[0, base + 3 + i] = 1.0
    return jnp.asarray(sel, BF16), jnp.asarray(ones, F32)


def _gla_call(proj, small, wg_hi, wg_lo, b_gk, gain, bf_pad, batch, seq):
    t = proj.shape[0]
    tm = GLA_TM
    nb = seq // tm

    def rows(b, i):
        return b * nb + i

    sel, ones = _fox_bias_selectors()
    return pl.pallas_call(
        _gla_kernel,
        grid=(batch, nb),
        in_specs=[
            pl.BlockSpec((tm, GLA_QK), lambda b, i: (rows(b, i), COL_QG // GLA_QK)),
            pl.BlockSpec((tm, GLA_QK), lambda b, i: (rows(b, i), COL_KG // GLA_QK)),
            pl.BlockSpec((tm, GLA_WIDTH), lambda b, i: (rows(b, i), COL_VG // GLA_WIDTH)),
            pl.BlockSpec((tm, GLA_WIDTH), lambda b, i: (rows(b, i), COL_GG // GLA_WIDTH)),
            pl.BlockSpec((tm, LANES), lambda b, i: (rows(b, i), 0)),
            pl.BlockSpec((LANES, GLA_QK), lambda b, i: (0, 0)),
            pl.BlockSpec((LANES, GLA_QK), lambda b, i: (0, 0)),
            pl.BlockSpec((1, GLA_QK), lambda b, i: (0, 0)),
            pl.BlockSpec((1, GLA_WIDTH), lambda b, i: (0, 0)),
            pl.BlockSpec((1, LANES), lambda b, i: (0, 0)),
            pl.BlockSpec((3, LANES, FOX_WIDTH), lambda b, i: (0, 0, 0)),
            pl.BlockSpec((1, FOX_WIDTH), lambda b, i: (0, 0)),
        ],
        out_specs=[
            pl.BlockSpec((tm, GLA_WIDTH), lambda b, i: (rows(b, i), 0)),
            pl.BlockSpec((tm, FOX_WIDTH), lambda b, i: (rows(b, i), 0)),
            pl.BlockSpec((1, FOX_HEADS, 1, tm), lambda b, i: (b, 0, 0, i)),
        ],
        out_shape=[
            jax.ShapeDtypeStruct((t, GLA_WIDTH), BF16),
            jax.ShapeDtypeStruct((t, FOX_WIDTH), BF16),
            jax.ShapeDtypeStruct((batch, FOX_HEADS, 1, seq), F32),
        ],
        scratch_shapes=[
            pltpu.VMEM((GLA_DV, GLA_QK), F32),
            pltpu.VMEM((1, LANES), F32),
            pltpu.VMEM((tm, GLA_WIDTH), F32),
        ],
        compiler_params=_params("parallel", "arbitrary"),
        name="gla",
    )(proj, proj, proj, proj, small, wg_hi, wg_lo, b_gk, gain, bf_pad, sel, ones)


FOX_TQ = 512
FOX_TK = 512
FOX_PAIRS = 2


def _fox_kernel(q_ref, k_ref, kx_ref, vt_ref, frow_ref, o_ref, m_ref, acc_ref, sa_ref, sb_ref,
                ma_ref, mb_ref):
    tq, tk = FOX_TQ, FOX_TK
    i = pl.program_id(2)
    q0 = pl.multiple_of(i * tq, tq)
    lane = lax.broadcasted_iota(jnp.int32, (1, LANES), 1)
    heads = range(2 * FOX_PAIRS)

    def pair_lanes(hh):
        return slice((hh // 2) * LANES, (hh // 2 + 1) * LANES)

    q_aug = []
    for hh in heads:
        head_lanes = (lane < FOX_DH) if hh % 2 == 0 else (lane >= FOX_DH)
        qm = jnp.where(head_lanes, q_ref[:, pair_lanes(hh)], 0.0).astype(BF16)
        fr = jnp.broadcast_to(frow_ref[0, hh, :, pl.ds(q0, LANES)][:, 0:1], (1, LANES))
        f1, f2, f3 = _split3(fr)
        base = (hh % 2) * FOX_BIAS_LANES
        qx = jnp.where(jnp.logical_and(lane >= base, lane < base + 3), 1.0, 0.0).astype(BF16)
        qx = jnp.where(lane == base + 3, f1, qx)
        qx = jnp.where(lane == base + 4, f2, qx)
        qx = jnp.where(lane == base + 5, f3, qx)
        q_aug.append(jnp.concatenate([qm, jnp.broadcast_to(qx, (tq, LANES))], axis=1))

    m_ref[...] = jnp.full(m_ref.shape, NEG, F32)
    acc_ref[...] = jnp.zeros_like(acc_ref)

    bufs = {"a": (sa_ref, ma_ref), "b": (sb_ref, mb_ref)}

    def produce(which, blk, hh, diagonal):
        buf, mx = bufs[which]
        k0 = pl.multiple_of(blk * tk, tk)
        k_aug = jnp.concatenate([k_ref[pl.ds(k0, tk), pair_lanes(hh)],
                                 kx_ref[pl.ds(k0, tk), pair_lanes(hh)]], axis=1)
        st = _dot_nt(k_aug, q_aug[hh])
        if diagonal:
            key = lax.broadcasted_iota(jnp.int32, (tk, tq), 0)
            qry = lax.broadcasted_iota(jnp.int32, (tk, tq), 1)
            st = jnp.where(key <= qry, st, NEG)
        buf[hh] = st
        mx[hh] = jnp.max(st, axis=0, keepdims=True)

    def consume(which, blk, hh):
        buf, mx = bufs[which]
        k0 = pl.multiple_of(blk * tk, tk)
        m_old = m_ref[hh]
        m_new = jnp.maximum(m_old, mx[hh])
        p = jnp.exp2(buf[hh] - m_new).astype(BF16)
        alpha = jnp.exp2(m_old - m_new)
        vt = vt_ref[hh * FOX_VROWS:(hh + 1) * FOX_VROWS, pl.ds(k0, tk)]
        acc_ref[hh] = alpha * acc_ref[hh] + _dot(vt, p)
        m_ref[hh] = m_new

    def stage(cur, cur_blk, nxt=None, nxt_blk=None, nxt_diagonal=False):
        for hh in heads:
            if nxt is not None:
                produce(nxt, nxt_blk, hh, nxt_diagonal)
            if hh > 0:
                consume(cur, cur_blk, hh - 1)
        consume(cur, cur_blk, heads[-1])

    @pl.when(i == 0)
    def _():
        for hh in heads:
            produce("a", 0, hh, True)
        stage("a", 0)

    @pl.when(i > 0)
    def _():
        for hh in heads:
            produce("a", 0, hh, False)

    def pair(t, carry):
        b0 = 2 * t
        stage("a", b0, "b", b0 + 1)
        stage("b", b0 + 1, "a", b0 + 2)
        return carry

    n_pairs = lax.shift_right_logical(jnp.maximum(i - 1, 0), 1)
    lax.fori_loop(0, n_pairs, pair, 0)

    @pl.when((i & 1) == 1)
    def _():
        stage("a", i - 1, "b", i, True)
        stage("b", i)

    @pl.when(jnp.logical_and((i & 1) == 0, i > 0))
    def _():
        stage("a", i - 2, "b", i - 1)
        stage("b", i - 1, "a", i, True)
        stage("a", i)

    for pr in range(FOX_PAIRS):
        outs = []
        for hh in (2 * pr, 2 * pr + 1):
            acc = acc_ref[hh]
            outs.append(acc[:FOX_DH] * (1.0 / acc[FOX_DH:FOX_DH + 1]))
        o_ref[:, pr * LANES:(pr + 1) * LANES] = jnp.concatenate(outs, axis=0).T.astype(BF16)


def _fox_call(proj, kx, vt, frow, batch, seq):
    t = proj.shape[0]
    tq = FOX_TQ
    nq = seq // tq
    width = FOX_PAIRS * LANES
    groups = FOX_WIDTH // width
    nh = 2 * FOX_PAIRS
    return pl.pallas_call(
        _fox_kernel,
        grid=(batch, groups, nq),
        in_specs=[
            pl.BlockSpec((tq, width), lambda b, hp, i: (b * nq + i, COL_QF // width + hp)),
            pl.BlockSpec((seq, width), lambda b, hp, i: (b, COL_KF // width + hp)),
            pl.BlockSpec((seq, width), lambda b, hp, i: (b, hp)),
            pl.BlockSpec((nh * FOX_VROWS, seq), lambda b, hp, i: (hp, b)),
            pl.BlockSpec((1, nh, 1, seq), lambda b, hp, i: (b, hp, 0, 0)),
        ],
        out_specs=pl.BlockSpec((tq, width), lambda b, hp, i: (b * nq + i, hp)),
        out_shape=jax.ShapeDtypeStruct((t, FOX_WIDTH), BF16),
        scratch_shapes=[
            pltpu.VMEM((nh, 1, tq), F32),
            pltpu.VMEM((nh, FOX_VROWS, tq), F32),
            pltpu.VMEM((nh, FOX_TK, tq), F32),
            pltpu.VMEM((nh, FOX_TK, tq), F32),
            pltpu.VMEM((nh, 1, tq), F32),
            pltpu.VMEM((nh, 1, tq), F32),
        ],
        compiler_params=_params("parallel", "parallel", "arbitrary"),
        name="fox",
    )(proj, proj, kx, vt, frow)


POST_TM = 512


def _post_kernel(og_ref, of_ref, x_ref, mod_ref, pg_ref, mg_ref, wo_ref, wrh_ref, wrl_ref, br_ref,
                 x1_ref, h2_ref, idx_ref, rank_ref, gate_ref, cnt_ref, car_ref):
    tm = POST_TM

    @pl.when(pl.program_id(0) == 0)
    def _():
        car_ref[...] = jnp.zeros_like(car_ref)

    y = _dot(og_ref[...], wo_ref[:GLA_WIDTH, :]) + _dot(of_ref[...], wo_ref[GLA_WIDTH:, :])
    x1 = x_ref[...] + mod_ref[0, 2:3, :] * (_rms(y) * pg_ref[...])
    x1_ref[...] = x1
    h2 = _rms(x1) * mg_ref[...]
    h2 = h2 * (1.0 + mod_ref[0, 4:5, :]) + mod_ref[0, 3:4, :]
    for j in range(ROW_TILES):
        h2_ref[pl.ds(j, tm, stride=ROW_TILES), :] = h2[:, j * LANES:(j + 1) * LANES]

    hh, hl = _split2(h2)
    logits = _dot(hh, wrh_ref[...]) + (_dot(hh, wrl_ref[...]) + _dot(hl, wrh_ref[...])) + br_ref[...]
    lane = lax.broadcasted_iota(jnp.int32, (tm, LANES), 1).astype(F32)
    work = jnp.where(lane < N_EXPERTS, logits, -jnp.inf)
    vals, idxs = [], []
    for _ in range(TOP_K):
        mx = jnp.max(work, axis=-1, keepdims=True)
        ix = jnp.min(jnp.where(work == mx, lane, float(LANES)), axis=-1, keepdims=True)
        vals.append(mx)
        idxs.append(ix)
        work = jnp.where(lane == ix, -jnp.inf, work)
    es = [jnp.exp(v - vals[0]) for v in vals]
    den = es[0] + es[1] + es[2] + es[3]

    onehot = jnp.zeros((tm, LANES), F32)
    for ix in idxs:
        onehot = onehot + jnp.where(lane == ix, 1.0, 0.0)
    row = lax.broadcasted_iota(jnp.int32, (tm, tm), 0)
    col = lax.broadcasted_iota(jnp.int32, (tm, tm), 1)
    strict_b = jnp.where(col < row, 1.0, 0.0).astype(BF16)
    prefix = _dot(strict_b, onehot.astype(BF16)) + car_ref[...]
    car_new = car_ref[...] + jnp.sum(onehot, axis=0, keepdims=True)
    car_ref[...] = car_new
    cnt_ref[...] = car_new

    idx_o = jnp.zeros((tm, LANES), F32)
    rank_o = jnp.zeros((tm, LANES), F32)
    gate_o = jnp.zeros((tm, LANES), F32)
    for kk in range(TOP_K):
        rk = jnp.sum(jnp.where(lane == idxs[kk], prefix, 0.0), axis=-1, keepdims=True)
        sel = lane == float(kk)
        idx_o = jnp.where(sel, idxs[kk], idx_o)
        rank_o = jnp.where(sel, rk, rank_o)
        gate_o = jnp.where(sel, es[kk] / den, gate_o)
    idx_ref[...] = idx_o.astype(jnp.int32)
    rank_ref[...] = rank_o.astype(jnp.int32)
    gate_ref[...] = gate_o


def _post_call(o_gla, o_fox, x2, mod3, post_gain, mlp_gain, w_o, wr_hi, wr_lo, br_pad, seq):
    t, d = x2.shape
    tm = POST_TM
    per_b = seq // tm
    return pl.pallas_call(
        _post_kernel,
        grid=(t // tm,),
        in_specs=[
            pl.BlockSpec((tm, GLA_WIDTH), lambda i: (i, 0)),
            pl.BlockSpec((tm, FOX_WIDTH), lambda i: (i, 0)),
            pl.BlockSpec((tm, d), lambda i: (i, 0)),
            pl.BlockSpec((1, N_MOD, d), lambda i: (i // per_b, 0, 0)),
            pl.BlockSpec((1, d), lambda i: (0, 0)),
            pl.BlockSpec((1, d), lambda i: (0, 0)),
            pl.BlockSpec((GLA_WIDTH + FOX_WIDTH, d), lambda i: (0, 0)),
            pl.BlockSpec((d, LANES), lambda i: (0, 0)),
            pl.BlockSpec((d, LANES), lambda i: (0, 0)),
            pl.BlockSpec((1, LANES), lambda i: (0, 0)),
        ],
        out_specs=[
            pl.BlockSpec((tm, d), lambda i: (i, 0)),
            pl.BlockSpec((tm * ROW_TILES, LANES), lambda i: (i, 0)),
            pl.BlockSpec((tm, LANES), lambda i: (i, 0)),
            pl.BlockSpec((tm, LANES), lambda i: (i, 0)),
            pl.BlockSpec((tm, LANES), lambda i: (i, 0)),
            pl.BlockSpec((1, LANES), lambda i: (0, 0)),
        ],
        out_shape=[
            jax.ShapeDtypeStruct((t, d), F32),
            jax.ShapeDtypeStruct((t * ROW_TILES, LANES), F32),
            jax.ShapeDtypeStruct((t, LANES), jnp.int32),
            jax.ShapeDtypeStruct((t, LANES), jnp.int32),
            jax.ShapeDtypeStruct((t, LANES), F32),
            jax.ShapeDtypeStruct((1, LANES), F32),
        ],
        scratch_shapes=[pltpu.VMEM((1, LANES), F32)],
        compiler_params=_params("arbitrary"),
        name="post",
    )(o_gla, o_fox, x2, mod3, post_gain, mlp_gain, w_o, wr_hi, wr_lo, br_pad)


DISP_TM = 256
ISSUE_UNROLL = 4


def _row_copy(src_ref, src_row, dst_ref, dst_row, sem):
    return pltpu.make_async_copy(
        src_ref.at[pl.ds(pl.multiple_of(src_row * ROW_TILES, ROW_TILES), ROW_TILES)],
        dst_ref.at[pl.ds(pl.multiple_of(dst_row * ROW_TILES, ROW_TILES), ROW_TILES)],
        sem)


def _dispatch_kernel(pend_ref, pos_ref, h_ref, xs_ref, zero_ref, sem):
    tm = DISP_TM
    zrows = MOE_TM * ROW_TILES

    @pl.when(pl.program_id(0) == 0)
    def _():
        zero_ref[...] = jnp.zeros_like(zero_ref)

        def last_tile(e):
            start = pl.multiple_of((pend_ref[e] - MOE_TM) * ROW_TILES, zrows)
            return pltpu.make_async_copy(zero_ref, xs_ref.at[pl.ds(start, zrows)], sem)

        def nonempty(e):
            return pend_ref[e] > jnp.where(e == 0, 0, pend_ref[jnp.maximum(e - 1, 0)])

        def clear(e, carry):
            @pl.when(nonempty(e))
            def _():
                last_tile(e).start()
            return carry

        def clear_wait(e, carry):
            @pl.when(nonempty(e))
            def _():
                last_tile(e).wait()
            return carry

        lax.fori_loop(0, N_EXPERTS, clear, 0)
        lax.fori_loop(0, N_EXPERTS, clear_wait, 0)

        def tail_tile(j):
            return pltpu.make_async_copy(
                zero_ref, xs_ref.at[pl.ds(pl.multiple_of(j * zrows, zrows), zrows)], sem)

        def tail(j, carry):
            tail_tile(j).start()
            return carry

        def tail_wait(j, carry):
            tail_tile(j).wait()
            return carry

        first_unused = lax.div(pend_ref[N_EXPERTS - 1], MOE_TM)
        n_tiles = xs_ref.shape[0] // zrows
        lax.fori_loop(first_unused, n_tiles, tail, 0)
        lax.fori_loop(first_unused, n_tiles, tail_wait, 0)

    def issue(g, carry):
        for u in range(ISSUE_UNROLL):
            t = g * ISSUE_UNROLL + u
            for kk in range(TOP_K):
                _row_copy(h_ref, t, xs_ref, pos_ref[0, 0, t * TOP_K + kk],
                          sem).start(priority=kk % 2)
        return carry

    lax.fori_loop(0, tm // ISSUE_UNROLL, issue, 0)
    for _ in range(TOP_K):
        pltpu.make_async_copy(h_ref, xs_ref.at[pl.ds(0, tm * ROW_TILES)], sem).wait()


def _dispatch_call(pend, pos3, h2, n_tiles):
    tm = DISP_TM
    n_steps = pos3.shape[0]
    grid_spec = pltpu.PrefetchScalarGridSpec(
        num_scalar_prefetch=1,
        grid=(n_steps,),
        in_specs=[
            pl.BlockSpec((1, 1, tm * TOP_K), lambda i, pend: (i, 0, 0), memory_space=pltpu.SMEM),
            pl.BlockSpec((tm * ROW_TILES, LANES), lambda i, pend: (i, 0)),
        ],
        out_specs=pl.BlockSpec(memory_space=pl.ANY),
        scratch_shapes=[
            pltpu.VMEM((MOE_TM * ROW_TILES, LANES), F32),
            pltpu.SemaphoreType.DMA,
        ],
    )
    return pl.pallas_call(
        _dispatch_kernel,
        grid_spec=grid_spec,
        out_shape=jax.ShapeDtypeStruct((n_tiles * MOE_TM * ROW_TILES, LANES), F32),
        compiler_params=_params("arbitrary"),
        name="dispatch",
    )(pend, pos3, h2)


MOE_TM = 512


def _experts_kernel(te_ref, nu_ref, x_ref, win_ref, bin_ref, wout_ref, bout_ref, y_ref,
                    wib_ref, wob_ref):
    tm = MOE_TM
    i = pl.program_id(0)
    used = i < nu_ref[0]

    @pl.when(jnp.logical_not(used))
    def _():
        y_ref[...] = jnp.zeros_like(y_ref)

    fresh = jnp.logical_or(i == 0, te_ref[i] != te_ref[jnp.maximum(i - 1, 0)])

    @pl.when(jnp.logical_and(used, fresh))
    def _():
        wib_ref[...] = win_ref[0].astype(BF16)
        wob_ref[...] = wout_ref[0].astype(BF16)

    @pl.when(used)
    def _():
        x = jnp.concatenate([x_ref[pl.ds(j, tm, stride=ROW_TILES), :] for j in range(ROW_TILES)],
                            axis=1).astype(BF16)
        u = _dot(x, wib_ref[...]) + bin_ref[0]
        glu = jnp.minimum(u[:, :D_FF], SWIGLU_LIMIT)
        lin = jnp.clip(u[:, D_FF:], -SWIGLU_LIMIT, SWIGLU_LIMIT)
        a = glu * jax.nn.sigmoid(SWIGLU_ALPHA * glu) * (lin + 1.0)
        y = _dot(a.astype(BF16), wob_ref[...]) + bout_ref[0]
        for j in range(ROW_TILES):
            y_ref[pl.ds(j, tm, stride=ROW_TILES), :] = y[:, j * LANES:(j + 1) * LANES]


def _experts_call(tile_expert, n_used, xs, w_in, b_in, w_out, b_out):
    tm = MOE_TM
    n_tiles = tile_expert.shape[0]
    d = D_MODEL

    def tile(i, te, nu):
        return (jnp.minimum(i, nu[0] - 1), 0)

    grid_spec = pltpu.PrefetchScalarGridSpec(
        num_scalar_prefetch=2,
        grid=(n_tiles,),
        in_specs=[
            pl.BlockSpec((tm * ROW_TILES, LANES), tile),
            pl.BlockSpec((1, d, 2 * D_FF), lambda i, te, nu: (te[i], 0, 0)),
            pl.BlockSpec((1, 1, 2 * D_FF), lambda i, te, nu: (te[i], 0, 0)),
            pl.BlockSpec((1, D_FF, d), lambda i, te, nu: (te[i], 0, 0)),
            pl.BlockSpec((1, 1, d), lambda i, te, nu: (te[i], 0, 0)),
        ],
        out_specs=pl.BlockSpec((tm * ROW_TILES, LANES), lambda i, te, nu: (i, 0)),
        scratch_shapes=[
            pltpu.VMEM((d, 2 * D_FF), BF16),
            pltpu.VMEM((D_FF, d), BF16),
        ],
    )
    return pl.pallas_call(
        _experts_kernel,
        grid_spec=grid_spec,
        out_shape=jax.ShapeDtypeStruct(xs.shape, F32),
        compiler_params=_params("arbitrary"),
        name="experts",
    )(tile_expert, n_used, xs, w_in, b_in, w_out, b_out)


COMB_TM = 256


def _combine_kernel(pos_ref, posn_ref, y_ref, gate_ref, x1_ref, mod_ref, pg_ref, o_ref, buf_ref, sem):
    tm = COMB_TM
    i = pl.program_id(0)
    slot = i & 1

    def gather(p_ref, s):
        def issue(g, carry):
            for u in range(ISSUE_UNROLL):
                t = g * ISSUE_UNROLL + u
                for kk in range(TOP_K):
                    _row_copy(y_ref, p_ref[0, 0, t * TOP_K + kk], buf_ref.at[s, kk], t,
                              sem.at[s]).start(priority=kk % 2)
            return carry

        lax.fori_loop(0, tm // ISSUE_UNROLL, issue, 0)

    @pl.when(i == 0)
    def _():
        gather(pos_ref, 0)

    @pl.when(i + 1 < pl.num_programs(0))
    def _():
        gather(posn_ref, 1 - slot)

    for kk in range(TOP_K):
        pltpu.make_async_copy(y_ref.at[pl.ds(0, tm * ROW_TILES)], buf_ref.at[slot, kk],
                              sem.at[slot]).wait()

    gates = gate_ref[...]
    acc = None
    for kk in range(TOP_K):
        rows = jnp.concatenate(
            [buf_ref[slot, kk, pl.ds(j, tm, stride=ROW_TILES), :] for j in range(ROW_TILES)], axis=1)
        term = rows * gates[:, kk:kk + 1]
        acc = term if acc is None else acc + term
    o_ref[...] = x1_ref[...] + mod_ref[0, 5:6, :] * (_rms(acc) * pg_ref[...])


def _combine_call(pos3, y, gates, x1, mod3, post_gain, seq):
    t, d = x1.shape
    tm = COMB_TM
    per_b = seq // tm
    n_steps = t // tm
    return pl.pallas_call(
        _combine_kernel,
        grid=(n_steps,),
        in_specs=[
            pl.BlockSpec((1, 1, tm * TOP_K), lambda i: (i, 0, 0), memory_space=pltpu.SMEM),
            pl.BlockSpec((1, 1, tm * TOP_K), lambda i: (jnp.minimum(i + 1, n_steps - 1), 0, 0),
                         memory_space=pltpu.SMEM),
            pl.BlockSpec(memory_space=pl.ANY),
            pl.BlockSpec((tm, LANES), lambda i: (i, 0)),
            pl.BlockSpec((tm, d), lambda i: (i, 0)),
            pl.BlockSpec((1, N_MOD, d), lambda i: (i // per_b, 0, 0)),
            pl.BlockSpec((1, d), lambda i: (0, 0)),
        ],
        out_specs=pl.BlockSpec((tm, d), lambda i: (i, 0)),
        out_shape=jax.ShapeDtypeStruct((t, d), F32),
        scratch_shapes=[
            pltpu.VMEM((2, TOP_K, tm * ROW_TILES, LANES), F32),
            pltpu.SemaphoreType.DMA((2,)),
        ],
        compiler_params=_params("arbitrary"),
        name="combine",
    )(pos3, pos3, y, gates, x1, mod3, post_gain)


def _pad_cols(w, n):
    return jnp.pad(w, ((0, 0), (0, n - w.shape[1])))


def kernel(x, c, w_ada, b_ada, attn_pre_gain, attn_post_gain, w_in, w_gk2, b_gk, gla_norm_gain, b_f,
           w_o, mlp_pre_gain, mlp_post_gain, w_router, b_router, w_e_in, b_e_in, w_e_out, b_e_out):
    batch, seq, d = x.shape
    t = batch * seq
    x2 = x.reshape(t, d)

    o = 0
    cols = {}
    for name, width in (("qg", GLA_QK), ("kg", GLA_QK), ("vg", GLA_WIDTH), ("gg", GLA_WIDTH),
                        ("zg", GLA_LOWRANK), ("qf", FOX_WIDTH), ("kf", FOX_WIDTH), ("vf", FOX_WIDTH),
                        ("ff", FOX_HEADS)):
        cols[name] = w_in[:, o:o + width]
        o += width
    gla_cols = 2 * GLA_QK + 2 * GLA_WIDTH
    fox_start = gla_cols + GLA_LOWRANK
    w_main = jnp.concatenate([w_in[:, :gla_cols], w_in[:, fox_start:fox_start + 2 * FOX_WIDTH]],
                             axis=1).astype(BF16)
    w_vt = cols["vf"].T.astype(BF16)
    w_small = _pad_cols(jnp.concatenate([cols["ff"], cols["zg"]], axis=1), LANES)
    ws_hi, ws_lo = _split2(w_small)
    wg = jnp.zeros((LANES, GLA_QK), F32).at[SMALL_ZG:SMALL_ZG + GLA_LOWRANK].set(w_gk2)
    wg_hi, wg_lo = _split2(wg)
    bf_pad = jnp.zeros((1, LANES), F32).at[0, SMALL_FF:SMALL_FF + FOX_HEADS].set(b_f)
    wr_hi, wr_lo = _split2(_pad_cols(w_router, LANES))
    br_pad = _pad_cols(b_router.reshape(1, N_EXPERTS), LANES)
    wa_hi, wa_lo = _split2(w_ada)

    c_pad = jnp.pad(c, ((0, SUBLANES - batch % SUBLANES if batch % SUBLANES else 0), (0, 0)))
    mod = _mod_call(c_pad, wa_hi, wa_lo, b_ada.reshape(1, -1))
    mod3 = mod[:batch].reshape(batch, N_MOD, d)

    proj, vt, small = _inproj_call(x2, mod3, attn_pre_gain.reshape(1, d), w_main, w_vt, ws_hi, ws_lo,
                                   seq)
    o_gla, kx, frow = _gla_call(proj, small, wg_hi, wg_lo, b_gk.reshape(1, -1),
                                gla_norm_gain.reshape(1, -1), bf_pad, batch, seq)
    o_fox = _fox_call(proj, kx, vt, frow, batch, seq)

    x1, h2, idx_w, rank_w, gate_w, cnt = _post_call(
        o_gla, o_fox, x2, mod3, attn_post_gain.reshape(1, d), mlp_pre_gain.reshape(1, d),
        w_o.astype(BF16), wr_hi, wr_lo, br_pad, seq)

    tm = MOE_TM
    n_tiles = (t * TOP_K) // tm + N_EXPERTS
    counts = cnt[0, :N_EXPERTS].astype(jnp.int32)
    padded = ((counts + tm - 1) // tm) * tm
    pend = jnp.cumsum(padded)
    pstart = pend - padded
    pos = pstart[idx_w[:, :TOP_K]] + rank_w[:, :TOP_K]
    n_used = (pend[-1] // tm).astype(jnp.int32).reshape(1)
    tile_start = jnp.arange(n_tiles, dtype=jnp.int32) * tm
    tile_expert = jnp.minimum(
        jnp.sum((pend[None, :] <= tile_start[:, None]).astype(jnp.int32), axis=1), N_EXPERTS - 1)

    pos_d = pos.reshape(t // DISP_TM, 1, DISP_TM * TOP_K)
    xs = _dispatch_call(pend.astype(jnp.int32), pos_d, h2, n_tiles)
    y = _experts_call(tile_expert, n_used, xs, w_e_in, b_e_in.reshape(N_EXPERTS, 1, -1), w_e_out,
                      b_e_out.reshape(N_EXPERTS, 1, -1))
    pos_c = pos.reshape(t // COMB_TM, 1, COMB_TM * TOP_K)
    out = _combine_call(pos_c, y, gate_w, x1, mod3, mlp_post_gain.reshape(1, d), seq)
    return out.reshape(batch, seq, d)
```

```python
import functools

import jax
import jax.numpy as jnp
from jax import lax
from jax.experimental import pallas as pl
from jax.experimental.pallas import tpu as pltpu

F32 = jnp.float32
BF16 = jnp.bfloat16

D_MODEL = 1024
N_MOD = 6
GLA_HEADS = 4
GLA_DK = 64
GLA_DV = 128
GLA_QK = GLA_HEADS * GLA_DK
GLA_WIDTH = GLA_HEADS * GLA_DV
GLA_LOWRANK = 16
GLA_CHUNK = 64
FOX_HEADS = 8
FOX_DH = 64
FOX_WIDTH = FOX_HEADS * FOX_DH
N_EXPERTS = 32
TOP_K = 4
D_FF = D_MODEL
SWIGLU_ALPHA = 1.702
SWIGLU_LIMIT = 7.0
EPS = 1e-6

LANES = 128
SUBLANES = 8
ROW_TILES = D_MODEL // LANES

COL_QG, COL_KG, COL_VG, COL_GG = 0, 256, 512, 1024
COL_QF, COL_KF = 1536, 2048
MAIN_COLS = 2560
LOG2E = 1.4426950408889634
FOX_Q_SCALE = (FOX_DH ** -0.5) * LOG2E
FOX_BIAS_LANES = 6
BF16_SUBLANES = 16
FOX_VROWS = FOX_DH + BF16_SUBLANES
SMALL_FF, SMALL_ZG = 0, 8

NEG = -1e30
VMEM_LIMIT = 56 * 1024 * 1024


def _dot(a, b):
    return jnp.dot(a, b, preferred_element_type=F32)


def _dot_nt(a, b):
    return lax.dot_general(a, b, (((1,), (1,)), ((), ())), preferred_element_type=F32)


def _dot_tn(a, b):
    return lax.dot_general(a, b, (((0,), (0,)), ((), ())), preferred_element_type=F32)


def _split2(a):
    hi = a.astype(BF16)
    lo = (a - hi.astype(F32)).astype(BF16)
    return hi, lo


def _split3(a):
    hi = a.astype(BF16)
    r = a - hi.astype(F32)
    mid = r.astype(BF16)
    lo = (r - mid.astype(F32)).astype(BF16)
    return hi, mid, lo


def _log_sigmoid(x):
    return jnp.minimum(x, 0.0) - jnp.log1p(jnp.exp(-jnp.abs(x)))


def _rms(x):
    return x * lax.rsqrt(jnp.mean(x * x, axis=-1, keepdims=True) + EPS)


def _params(*sem, flags=None):
    return pltpu.CompilerParams(dimension_semantics=sem, vmem_limit_bytes=VMEM_LIMIT, flags=flags)


def _mod_kernel(c_ref, wh_ref, wl_ref, b_ref, o_ref):
    c = c_ref[...]
    s = c * jax.nn.sigmoid(c)
    sh, sl = _split2(s)
    o_ref[...] = _dot(sh, wh_ref[...]) + (_dot(sh, wl_ref[...]) + _dot(sl, wh_ref[...])) + b_ref[...]


def _mod_call(c_pad, w_hi, w_lo, b_ada):
    rows, d = c_pad.shape
    n = w_hi.shape[1]
    tn = 1024
    return pl.pallas_call(
        _mod_kernel,
        grid=(n // tn,),
        in_specs=[
            pl.BlockSpec((rows, d), lambda j: (0, 0)),
            pl.BlockSpec((d, tn), lambda j: (0, j)),
            pl.BlockSpec((d, tn), lambda j: (0, j)),
            pl.BlockSpec((1, tn), lambda j: (0, j)),
        ],
        out_specs=pl.BlockSpec((rows, tn), lambda j: (0, j)),
        out_shape=jax.ShapeDtypeStruct((rows, n), F32),
        compiler_params=_params("parallel"),
        name="mod",
    )(c_pad, w_hi, w_lo, b_ada)


def _inproj_kernel(x_ref, mod_ref, g_ref, w_ref, wvt_ref, wsh_ref, wsl_ref, proj_ref, vt_ref, small_ref):
    h = _rms(x_ref[...]) * g_ref[...]
    h = h * (1.0 + mod_ref[0, 1:2, :]) + mod_ref[0, 0:1, :]
    hh, hl = _split2(h)
    cw = 512
    for j in range(MAIN_COLS // cw):
        r = _dot(hh, w_ref[:, j * cw:(j + 1) * cw])
        if j * cw == COL_QF:
            r = r * FOX_Q_SCALE
        proj_ref[:, j * cw:(j + 1) * cw] = r.astype(BF16)
    vt = _dot_nt(wvt_ref[...], hh).astype(BF16)
    ones = jnp.ones((FOX_VROWS - FOX_DH, vt.shape[1]), BF16)
    for h in range(FOX_HEADS):
        vt_ref[h * FOX_VROWS:h * FOX_VROWS + FOX_DH, :] = vt[h * FOX_DH:(h + 1) * FOX_DH, :]
        vt_ref[h * FOX_VROWS + FOX_DH:(h + 1) * FOX_VROWS, :] = ones
    a = _dot(hh, wsl_ref[...])
    small_ref[...] = a[:, :LANES] + (a[:, LANES:] + _dot(hl, wsh_ref[...]))


def _inproj_call(x2, mod3, gain, w_main, w_vt, ws_hi, ws_lo, seq):
    ws_hilo = jnp.concatenate([ws_hi, ws_lo], axis=1)
    t, d = x2.shape
    tm = 512
    per_b = seq // tm
    return pl.pallas_call(
        _inproj_kernel,
        grid=(t // tm,),
        in_specs=[
            pl.BlockSpec((tm, d), lambda i: (i, 0)),
            pl.BlockSpec((1, N_MOD, d), lambda i: (i // per_b, 0, 0)),
            pl.BlockSpec((1, d), lambda i: (0, 0)),
            pl.BlockSpec((d, MAIN_COLS), lambda i: (0, 0)),
            pl.BlockSpec((FOX_WIDTH, d), lambda i: (0, 0)),
            pl.BlockSpec((d, LANES), lambda i: (0, 0)),
            pl.BlockSpec((d, 2 * LANES), lambda i: (0, 0)),
        ],
        out_specs=[
            pl.BlockSpec((tm, MAIN_COLS), lambda i: (i, 0)),
            pl.BlockSpec((FOX_HEADS * FOX_VROWS, tm), lambda i: (0, i)),
            pl.BlockSpec((tm, LANES), lambda i: (i, 0)),
        ],
        out_shape=[
            jax.ShapeDtypeStruct((t, MAIN_COLS), BF16),
            jax.ShapeDtypeStruct((FOX_HEADS * FOX_VROWS, t), BF16),
            jax.ShapeDtypeStruct((t, LANES), F32),
        ],
        compiler_params=_params("parallel"),
        name="inproj",
    )(x2, mod3, gain, w_main, w_vt, ws_hi, ws_hilo)


GLA_TM = 512


def _gla_kernel(q_ref, k_ref, v_ref, g_ref, small_ref, wgh_ref, wgl_ref, bgk_ref, gain_ref, bf_ref,
                sel_ref, ones_ref, o_ref, kx_ref, frow_ref, st_ref, fcar_ref, oacc_ref):
    tm = GLA_TM
    nc = tm // GLA_CHUNK

    @pl.when(pl.program_id(1) == 0)
    def _():
        st_ref[...] = jnp.zeros_like(st_ref)
        fcar_ref[...] = jnp.zeros_like(fcar_ref)

    small = small_ref[...]
    sh, sl = _split2(small)
    gk = _dot(sh, wgh_ref[...]) + (_dot(sh, wgl_ref[...]) + _dot(sl, wgh_ref[...])) + bgk_ref[...]
    log_a = _log_sigmoid(gk) * (1.0 / 16.0)

    row = lax.broadcasted_iota(jnp.int32, (tm, tm), 0)
    col = lax.broadcasted_iota(jnp.int32, (tm, tm), 1)
    same = (row >> 6) == (col >> 6)
    lower = col <= row
    tri = jnp.logical_and(same, lower)
    tri_b = jnp.where(tri, 1.0, 0.0).astype(BF16)
    lower_b = jnp.where(lower, 1.0, 0.0).astype(BF16)

    lh, ll = _split2(log_a)
    b = _dot(tri_b, lh) + _dot(tri_b, ll)
    btot = jnp.concatenate(
        [jnp.broadcast_to(b[(c + 1) * GLA_CHUNK - 1:(c + 1) * GLA_CHUNK, :], (GLA_CHUNK, GLA_QK))
         for c in range(nc)], axis=0)

    q = q_ref[...].astype(F32)
    k = k_ref[...].astype(F32)
    qt = q * jnp.exp(b) * (GLA_DK ** -0.5)
    kt = (k * jnp.exp(-b)).astype(BF16)
    kd = (k * jnp.exp(btot - b)).astype(BF16)

    lane_qk = lax.broadcasted_iota(jnp.int32, (1, GLA_QK), 1)
    head_masks = [jnp.logical_and(lane_qk >= h * GLA_DK, lane_qk < (h + 1) * GLA_DK)
                  for h in range(GLA_HEADS)]

    for h in range(GLA_HEADS):
        qm = jnp.where(head_masks[h], qt, 0.0).astype(BF16)
        att = _dot_nt(qm, kt)
        att = jnp.where(tri, att, 0.0).astype(BF16)
        oacc_ref[:, h * GLA_DV:(h + 1) * GLA_DV] = _dot(att, v_ref[:, h * GLA_DV:(h + 1) * GLA_DV])

    for c in range(nc):
        rs = slice(c * GLA_CHUNK, (c + 1) * GLA_CHUNK)
        st = st_ref[...]
        qc = qt[rs]
        qs = jnp.concatenate([jnp.where(head_masks[h], qc, 0.0) for h in range(GLA_HEADS)],
                             axis=0).astype(BF16)
        oi = _dot_nt(qs, st.astype(BF16))
        for h in range(GLA_HEADS):
            oacc_ref[rs, h * GLA_DV:(h + 1) * GLA_DV] += oi[h * GLA_CHUNK:(h + 1) * GLA_CHUNK, :]
        kvt = _dot_tn(v_ref[rs, :], kd[rs])
        decay = jnp.exp(btot[c * GLA_CHUNK:c * GLA_CHUNK + 1, :])
        new = decay * st
        for h in range(GLA_HEADS):
            new = new + jnp.where(head_masks[h], kvt[h * GLA_DV:(h + 1) * GLA_DV, :], 0.0)
        st_ref[...] = new

    for h in range(GLA_HEADS):
        cs = slice(h * GLA_DV, (h + 1) * GLA_DV)
        y = _rms(oacc_ref[:, cs]) * gain_ref[:, cs]
        g = g_ref[:, cs].astype(F32)
        o_ref[:, cs] = (y * (g * jax.nn.sigmoid(g))).astype(BF16)

    lane = lax.broadcasted_iota(jnp.int32, (1, LANES), 1)
    lf = jnp.where(lane < FOX_HEADS, _log_sigmoid(small + bf_ref[...]), 0.0)
    l1, l2, l3 = _split3(lf)
    fl = _dot(lower_b, l1) + (_dot(lower_b, l2) + _dot(lower_b, l3)) + fcar_ref[...]
    fcar_ref[...] = fl[tm - 1:tm, :]
    fs = fl * LOG2E
    fst = fs.T
    for h in range(FOX_HEADS):
        frow_ref[0, h] = fst[h:h + 1, :]
    n1, n2, n3 = _split3(-fs)
    kx = _dot(n1, sel_ref[0]) + (_dot(n2, sel_ref[1]) + _dot(n3, sel_ref[2])) + ones_ref[...]
    kx_ref[...] = kx.astype(BF16)


def _fox_bias_selectors():
    import numpy as np
    sel = np.zeros((3, LANES, FOX_WIDTH), np.float32)
    ones = np.zeros((1, FOX_WIDTH), np.float32)
    for h in range(FOX_HEADS):
        base = (h // 2) * LANES + (h % 2) * FOX_BIAS_LANES
        for i in range(3):
            sel[i, h, base + i] = 1.0
            ones[0, base + 3 + i] = 1.0
    return jnp.asarray(sel, BF16), jnp.asarray(ones, F32)


def _gla_call(proj, small, wg_hi, wg_lo, b_gk, gain, bf_pad, batch, seq):
    t = proj.shape[0]
    tm = GLA_TM
    nb = seq // tm

    def rows(b, i):
        return b * nb + i

    sel, ones = _fox_bias_selectors()
    return pl.pallas_call(
        _gla_kernel,
        grid=(batch, nb),
        in_specs=[
            pl.BlockSpec((tm, GLA_QK), lambda b, i: (rows(b, i), COL_QG // GLA_QK)),
            pl.BlockSpec((tm, GLA_QK), lambda b, i: (rows(b, i), COL_KG // GLA_QK)),
            pl.BlockSpec((tm, GLA_WIDTH), lambda b, i: (rows(b, i), COL_VG // GLA_WIDTH)),
            pl.BlockSpec((tm, GLA_WIDTH), lambda b, i: (rows(b, i), COL_GG // GLA_WIDTH)),
            pl.BlockSpec((tm, LANES), lambda b, i: (rows(b, i), 0)),
            pl.BlockSpec((LANES, GLA_QK), lambda b, i: (0, 0)),
            pl.BlockSpec((LANES, GLA_QK), lambda b, i: (0, 0)),
            pl.BlockSpec((1, GLA_QK), lambda b, i: (0, 0)),
            pl.BlockSpec((1, GLA_WIDTH), lambda b, i: (0, 0)),
            pl.BlockSpec((1, LANES), lambda b, i: (0, 0)),
            pl.BlockSpec((3, LANES, FOX_WIDTH), lambda b, i: (0, 0, 0)),
            pl.BlockSpec((1, FOX_WIDTH), lambda b, i: (0, 0)),
        ],
        out_specs=[
            pl.BlockSpec((tm, GLA_WIDTH), lambda b, i: (rows(b, i), 0)),
            pl.BlockSpec((tm, FOX_WIDTH), lambda b, i: (rows(b, i), 0)),
            pl.BlockSpec((1, FOX_HEADS, 1, tm), lambda b, i: (b, 0, 0, i)),
        ],
        out_shape=[
            jax.ShapeDtypeStruct((t, GLA_WIDTH), BF16),
            jax.ShapeDtypeStruct((t, FOX_WIDTH), BF16),
            jax.ShapeDtypeStruct((batch, FOX_HEADS, 1, seq), F32),
        ],
        scratch_shapes=[
            pltpu.VMEM((GLA_DV, GLA_QK), F32),
            pltpu.VMEM((1, LANES), F32),
            pltpu.VMEM((tm, GLA_WIDTH), F32),
        ],
        compiler_params=_params("parallel", "arbitrary"),
        name="gla",
    )(proj, proj, proj, proj, small, wg_hi, wg_lo, b_gk, gain, bf_pad, sel, ones)


FOX_TQ = 512
FOX_TK = 512
FOX_PAIRS = 2
FOX_SPLIT = 1


def _fox_kernel(q_ref, k_ref, kx_ref, vt_ref, frow_ref, o_ref, m_ref, acc_ref, sa_ref, sb_ref,
                ma_ref, mb_ref):
    tq, tk = FOX_TQ, FOX_TK
    i = pl.program_id(2)
    q0 = pl.multiple_of(i * tq, tq)
    lane = lax.broadcasted_iota(jnp.int32, (1, LANES), 1)
    heads = range(2 * FOX_PAIRS)

    def pair_lanes(hh):
        return slice((hh // 2) * LANES, (hh // 2 + 1) * LANES)

    q_aug = []
    for hh in heads:
        head_lanes = (lane < FOX_DH) if hh % 2 == 0 else (lane >= FOX_DH)
        qm = jnp.where(head_lanes, q_ref[:, pair_lanes(hh)], 0.0).astype(BF16)
        fr = jnp.broadcast_to(frow_ref[0, hh, :, pl.ds(q0, LANES)][:, 0:1], (1, LANES))
        f1, f2, f3 = _split3(fr)
        base = (hh % 2) * FOX_BIAS_LANES
        qx = jnp.where(jnp.logical_and(lane >= base, lane < base + 3), 1.0, 0.0).astype(BF16)
        qx = jnp.where(lane == base + 3, f1, qx)
        qx = jnp.where(lane == base + 4, f2, qx)
        qx = jnp.where(lane == base + 5, f3, qx)
        q_aug.append(jnp.concatenate([qm, jnp.broadcast_to(qx, (tq, LANES))], axis=1))

    m_ref[...] = jnp.full(m_ref.shape, NEG, F32)
    acc_ref[...] = jnp.zeros_like(acc_ref)

    bufs = {"a": (sa_ref, ma_ref), "b": (sb_ref, mb_ref)}

    th = tk // FOX_SPLIT

    def produce(which, blk, hh, diagonal, part):
        buf, mx = bufs[which]
        k0 = pl.multiple_of(blk * tk + part * th, th)
        k_aug = jnp.concatenate([k_ref[pl.ds(k0, th), pair_lanes(hh)],
                                 kx_ref[pl.ds(k0, th), pair_lanes(hh)]], axis=1)
        st = _dot_nt(k_aug, q_aug[hh])
        if diagonal:
            key = lax.broadcasted_iota(jnp.int32, (th, tq), 0) + part * th
            qry = lax.broadcasted_iota(jnp.int32, (th, tq), 1)
            st = jnp.where(key <= qry, st, NEG)
        buf[hh, part * th:(part + 1) * th, :] = st
        part_max = jnp.max(st, axis=0, keepdims=True)
        mx[hh] = part_max if part == 0 else jnp.maximum(mx[hh], part_max)

    def consume(which, blk, hh, part):
        buf, mx = bufs[which]
        k0 = pl.multiple_of(blk * tk + part * th, th)
        m_old = m_ref[hh]
        m_new = jnp.maximum(m_old, mx[hh])
        p = jnp.exp2(buf[hh, part * th:(part + 1) * th, :] - m_new).astype(BF16)
        vt = vt_ref[hh * FOX_VROWS:(hh + 1) * FOX_VROWS, pl.ds(k0, th)]
        pv = _dot(vt, p)
        if part == 0:
            acc_ref[hh] = jnp.exp2(m_old - m_new) * acc_ref[hh] + pv
        else:
            acc_ref[hh] += pv
        if part == FOX_SPLIT - 1:
            m_ref[hh] = m_new

    def stage(cur, cur_blk, nxt=None, nxt_blk=None, nxt_diagonal=False):
        for hh in heads:
            for part in range(FOX_SPLIT):
                if nxt is not None:
                    produce(nxt, nxt_blk, hh, nxt_diagonal, part)
                if hh > 0:
                    consume(cur, cur_blk, hh - 1, part)
        for part in range(FOX_SPLIT):
            consume(cur, cur_blk, heads[-1], part)

    @pl.when(i == 0)
    def _():
        for hh in heads:
            for part in range(FOX_SPLIT):
                produce("a", 0, hh, True, part)
        stage("a", 0)

    @pl.when(i > 0)
    def _():
        for hh in heads:
            for part in range(FOX_SPLIT):
                produce("a", 0, hh, False, part)

    def pair(t, carry):
        b0 = 2 * t
        stage("a", b0, "b", b0 + 1)
        stage("b", b0 + 1, "a", b0 + 2)
        return carry

    n_pairs = lax.shift_right_logical(jnp.maximum(i - 1, 0), 1)
    lax.fori_loop(0, n_pairs, pair, 0)

    @pl.when((i & 1) == 1)
    def _():
        stage("a", i - 1, "b", i, True)
        stage("b", i)

    @pl.when(jnp.logical_and((i & 1) == 0, i > 0))
    def _():
        stage("a", i - 2, "b", i - 1)
        stage("b", i - 1, "a", i, True)
        stage("a", i)

    for pr in range(FOX_PAIRS):
        outs = []
        for hh in (2 * pr, 2 * pr + 1):
            acc = acc_ref[hh]
            outs.append(acc[:FOX_DH] * (1.0 / acc[FOX_DH:FOX_DH + 1]))
        o_ref[:, pr * LANES:(pr + 1) * LANES] = jnp.concatenate(outs, axis=0).T.astype(BF16)


def _fox_call(proj, kx, vt, frow, batch, seq):
    t = proj.shape[0]
    tq = FOX_TQ
    nq = seq // tq
    width = FOX_PAIRS * LANES
    groups = FOX_WIDTH // width
    nh = 2 * FOX_PAIRS
    return pl.pallas_call(
        _fox_kernel,
        grid=(batch, groups, nq),
        in_specs=[
            pl.BlockSpec((tq, width), lambda b, hp, i: (b * nq + i, COL_QF // width + hp)),
            pl.BlockSpec((seq, width), lambda b, hp, i: (b, COL_KF // width + hp)),
            pl.BlockSpec((seq, width), lambda b, hp, i: (b, hp)),
            pl.BlockSpec((nh * FOX_VROWS, seq), lambda b, hp, i: (hp, b)),
            pl.BlockSpec((1, nh, 1, seq), lambda b, hp, i: (b, hp, 0, 0)),
        ],
        out_specs=pl.BlockSpec((tq, width), lambda b, hp, i: (b * nq + i, hp)),
        out_shape=jax.ShapeDtypeStruct((t, FOX_WIDTH), BF16),
        scratch_shapes=[
            pltpu.VMEM((nh, 1, tq), F32),
            pltpu.VMEM((nh, FOX_VROWS, tq), F32),
            pltpu.VMEM((nh, FOX_TK, tq), F32),
            pltpu.VMEM((nh, FOX_TK, tq), F32),
            pltpu.VMEM((nh, 1, tq), F32),
            pltpu.VMEM((nh, 1, tq), F32),
        ],
        compiler_params=_params("parallel", "parallel", "arbitrary"),
        name="fox",
    )(proj, proj, kx, vt, frow)


POST_TM = 512
POST_SUB = 2


def _post_kernel(og_ref, of_ref, x_ref, mod_ref, pg_ref, mg_ref, wo_ref, wrh_ref, wrl_ref, br_ref,
                 x1_ref, h2_ref, idx_ref, rank_ref, gate_ref, cnt_ref, car_ref):
    @pl.when(pl.program_id(0) == 0)
    def _():
        car_ref[...] = jnp.zeros_like(car_ref)

    ys = []
    for s in range(POST_SUB):
        rows = slice(s * POST_TM, (s + 1) * POST_TM)
        ys.append(_dot(og_ref[rows, :], wo_ref[:GLA_WIDTH, :])
                  + _dot(of_ref[rows, :], wo_ref[GLA_WIDTH:, :]))
    for s in range(POST_SUB):
        _post_route(s, ys[s], x_ref, mod_ref, pg_ref, mg_ref, wrh_ref, wrl_ref, br_ref,
                    x1_ref, h2_ref, idx_ref, rank_ref, gate_ref, cnt_ref, car_ref)


def _post_route(s, y, x_ref, mod_ref, pg_ref, mg_ref, wrh_ref, wrl_ref, br_ref,
                x1_ref, h2_ref, idx_ref, rank_ref, gate_ref, cnt_ref, car_ref):
    tm = POST_TM
    rows = slice(s * tm, (s + 1) * tm)
    x1 = x_ref[rows, :] + mod_ref[0, 2:3, :] * (_rms(y) * pg_ref[...])
    x1_ref[rows, :] = x1
    h2 = _rms(x1) * mg_ref[...]
    h2 = h2 * (1.0 + mod_ref[0, 4:5, :]) + mod_ref[0, 3:4, :]
    for j in range(ROW_TILES):
        h2_ref[pl.ds(s * tm * ROW_TILES + j, tm, stride=ROW_TILES), :] = h2[:, j * LANES:(j + 1) * LANES]

    hh, hl = _split2(h2)
    logits = _dot(hh, wrh_ref[...]) + (_dot(hh, wrl_ref[...]) + _dot(hl, wrh_ref[...])) + br_ref[...]
    lane = lax.broadcasted_iota(jnp.int32, (tm, LANES), 1).astype(F32)
    work = jnp.where(lane < N_EXPERTS, logits, -jnp.inf)
    vals, idxs = [], []
    for _ in range(TOP_K):
        mx = jnp.max(work, axis=-1, keepdims=True)
        ix = jnp.min(jnp.where(work == mx, lane, float(LANES)), axis=-1, keepdims=True)
        vals.append(mx)
        idxs.append(ix)
        work = jnp.where(lane == ix, -jnp.inf, work)
    es = [jnp.exp(v - vals[0]) for v in vals]
    den = es[0] + es[1] + es[2] + es[3]

    onehot = jnp.zeros((tm, LANES), F32)
    for ix in idxs:
        onehot = onehot + jnp.where(lane == ix, 1.0, 0.0)
    row = lax.broadcasted_iota(jnp.int32, (tm, tm), 0)
    col = lax.broadcasted_iota(jnp.int32, (tm, tm), 1)
    strict_b = jnp.where(col < row, 1.0, 0.0).astype(BF16)
    prefix = _dot(strict_b, onehot.astype(BF16)) + car_ref[...]
    car_new = car_ref[...] + jnp.sum(onehot, axis=0, keepdims=True)
    car_ref[...] = car_new
    cnt_ref[...] = car_new

    idx_o = jnp.zeros((tm, LANES), F32)
    rank_o = jnp.zeros((tm, LANES), F32)
    gate_o = jnp.zeros((tm, LANES), F32)
    for kk in range(TOP_K):
        rk = jnp.sum(jnp.where(lane == idxs[kk], prefix, 0.0), axis=-1, keepdims=True)
        sel = lane == float(kk)
        idx_o = jnp.where(sel, idxs[kk], idx_o)
        rank_o = jnp.where(sel, rk, rank_o)
        gate_o = jnp.where(sel, es[kk] / den, gate_o)
    idx_ref[rows, :] = idx_o.astype(jnp.int32)
    rank_ref[rows, :] = rank_o.astype(jnp.int32)
    gate_ref[rows, :] = gate_o


def _post_call(o_gla, o_fox, x2, mod3, post_gain, mlp_gain, w_o, wr_hi, wr_lo, br_pad, seq):
    t, d = x2.shape
    tm = POST_TM * POST_SUB
    per_b = seq // tm
    return pl.pallas_call(
        _post_kernel,
        grid=(t // tm,),
        in_specs=[
            pl.BlockSpec((tm, GLA_WIDTH), lambda i: (i, 0)),
            pl.BlockSpec((tm, FOX_WIDTH), lambda i: (i, 0)),
            pl.BlockSpec((tm, d), lambda i: (i, 0)),
            pl.BlockSpec((1, N_MOD, d), lambda i: (i // per_b, 0, 0)),
            pl.BlockSpec((1, d), lambda i: (0, 0)),
            pl.BlockSpec((1, d), lambda i: (0, 0)),
            pl.BlockSpec((GLA_WIDTH + FOX_WIDTH, d), lambda i: (0, 0)),
            pl.BlockSpec((d, LANES), lambda i: (0, 0)),
            pl.BlockSpec((d, LANES), lambda i: (0, 0)),
            pl.BlockSpec((1, LANES), lambda i: (0, 0)),
        ],
        out_specs=[
            pl.BlockSpec((tm, d), lambda i: (i, 0)),
            pl.BlockSpec((tm * ROW_TILES, LANES), lambda i: (i, 0)),
            pl.BlockSpec((tm, LANES), lambda i: (i, 0)),
            pl.BlockSpec((tm, LANES), lambda i: (i, 0)),
            pl.BlockSpec((tm, LANES), lambda i: (i, 0)),
            pl.BlockSpec((1, LANES), lambda i: (0, 0)),
        ],
        out_shape=[
            jax.ShapeDtypeStruct((t, d), F32),
            jax.ShapeDtypeStruct((t * ROW_TILES, LANES), F32),
            jax.ShapeDtypeStruct((t, LANES), jnp.int32),
            jax.ShapeDtypeStruct((t, LANES), jnp.int32),
            jax.ShapeDtypeStruct((t, LANES), F32),
            jax.ShapeDtypeStruct((1, LANES), F32),
        ],
        scratch_shapes=[pltpu.VMEM((1, LANES), F32)],
        compiler_params=_params("arbitrary"),
        name="post",
    )(o_gla, o_fox, x2, mod3, post_gain, mlp_gain, w_o, wr_hi, wr_lo, br_pad)


DISP_TM = 256
ISSUE_UNROLL = 8


def _row_copy(src_ref, src_row, dst_ref, dst_row, sem):
    return pltpu.make_async_copy(
        src_ref.at[pl.ds(pl.multiple_of(src_row * ROW_TILES, ROW_TILES), ROW_TILES)],
        dst_ref.at[pl.ds(pl.multiple_of(dst_row * ROW_TILES, ROW_TILES), ROW_TILES)],
        sem)


def _dispatch_kernel(pend_ref, pos_ref, h_ref, xs_ref, zero_ref, sem):
    tm = DISP_TM
    zrows = MOE_TM * ROW_TILES

    @pl.when(pl.program_id(0) == 0)
    def _():
        zero_ref[...] = jnp.zeros_like(zero_ref)

        def last_tile(e):
            start = pl.multiple_of((pend_ref[e] - MOE_TM) * ROW_TILES, zrows)
            return pltpu.make_async_copy(zero_ref, xs_ref.at[pl.ds(start, zrows)], sem)

        def nonempty(e):
            return pend_ref[e] > jnp.where(e == 0, 0, pend_ref[jnp.maximum(e - 1, 0)])

        def clear(e, carry):
            @pl.when(nonempty(e))
            def _():
                last_tile(e).start()
            return carry

        def clear_wait(e, carry):
            @pl.when(nonempty(e))
            def _():
                last_tile(e).wait()
            return carry

        lax.fori_loop(0, N_EXPERTS, clear, 0)
        lax.fori_loop(0, N_EXPERTS, clear_wait, 0)

        def tail_tile(j):
            return pltpu.make_async_copy(
                zero_ref, xs_ref.at[pl.ds(pl.multiple_of(j * zrows, zrows), zrows)], sem)

        def tail(j, carry):
            tail_tile(j).start()
            return carry

        def tail_wait(j, carry):
            tail_tile(j).wait()
            return carry

        first_unused = lax.div(pend_ref[N_EXPERTS - 1], MOE_TM)
        n_tiles = xs_ref.shape[0] // zrows
        lax.fori_loop(first_unused, n_tiles, tail, 0)
        lax.fori_loop(first_unused, n_tiles, tail_wait, 0)

    def issue(g, carry):
        for u in range(ISSUE_UNROLL):
            t = g * ISSUE_UNROLL + u
            for kk in range(TOP_K):
                _row_copy(h_ref, t, xs_ref, pos_ref[0, 0, t * TOP_K + kk],
                          sem).start(priority=kk % 2)
        return carry

    lax.fori_loop(0, tm // ISSUE_UNROLL, issue, 0)
    for _ in range(TOP_K):
        pltpu.make_async_copy(h_ref, xs_ref.at[pl.ds(0, tm * ROW_TILES)], sem).wait()


def _dispatch_call(pend, pos3, h2, n_tiles):
    tm = DISP_TM
    n_steps = pos3.shape[0]
    grid_spec = pltpu.PrefetchScalarGridSpec(
        num_scalar_prefetch=1,
        grid=(n_steps,),
        in_specs=[
            pl.BlockSpec((1, 1, tm * TOP_K), lambda i, pend: (i, 0, 0), memory_space=pltpu.SMEM),
            pl.BlockSpec((tm * ROW_TILES, LANES), lambda i, pend: (i, 0)),
        ],
        out_specs=pl.BlockSpec(memory_space=pl.ANY),
        scratch_shapes=[
            pltpu.VMEM((MOE_TM * ROW_TILES, LANES), F32),
            pltpu.SemaphoreType.DMA,
        ],
    )
    return pl.pallas_call(
        _dispatch_kernel,
        grid_spec=grid_spec,
        out_shape=jax.ShapeDtypeStruct((n_tiles * MOE_TM * ROW_TILES, LANES), F32),
        compiler_params=_params("arbitrary"),
        name="dispatch",
    )(pend, pos3, h2)


MOE_TM = 512


def _experts_kernel(te_ref, nu_ref, x_ref, win_ref, bin_ref, wout_ref, bout_ref, y_ref,
                    wib_ref, wob_ref):
    tm = MOE_TM
    i = pl.program_id(0)
    used = i < nu_ref[0]

    @pl.when(jnp.logical_not(used))
    def _():
        y_ref[...] = jnp.zeros_like(y_ref)

    fresh = jnp.logical_or(i == 0, te_ref[i] != te_ref[jnp.maximum(i - 1, 0)])

    @pl.when(jnp.logical_and(used, fresh))
    def _():
        wib_ref[...] = win_ref[0].astype(BF16)
        wob_ref[...] = wout_ref[0].astype(BF16)

    @pl.when(used)
    def _():
        x = jnp.concatenate([x_ref[pl.ds(j, tm, stride=ROW_TILES), :] for j in range(ROW_TILES)],
                            axis=1).astype(BF16)
        u = _dot(x, wib_ref[...]) + bin_ref[0]
        glu = jnp.minimum(u[:, :D_FF], SWIGLU_LIMIT)
        lin = jnp.clip(u[:, D_FF:], -SWIGLU_LIMIT, SWIGLU_LIMIT)
        a = glu * jax.nn.sigmoid(SWIGLU_ALPHA * glu) * (lin + 1.0)
        y = _dot(a.astype(BF16), wob_ref[...]) + bout_ref[0]
        for j in range(ROW_TILES):
            y_ref[pl.ds(j, tm, stride=ROW_TILES), :] = y[:, j * LANES:(j + 1) * LANES]


def _experts_call(tile_expert, n_used, xs, w_in, b_in, w_out, b_out):
    tm = MOE_TM
    n_tiles = tile_expert.shape[0]
    d = D_MODEL

    def tile(i, te, nu):
        return (jnp.minimum(i, nu[0] - 1), 0)

    grid_spec = pltpu.PrefetchScalarGridSpec(
        num_scalar_prefetch=2,
        grid=(n_tiles,),
        in_specs=[
            pl.BlockSpec((tm * ROW_TILES, LANES), tile),
            pl.BlockSpec((1, d, 2 * D_FF), lambda i, te, nu: (te[i], 0, 0)),
            pl.BlockSpec((1, 1, 2 * D_FF), lambda i, te, nu: (te[i], 0, 0)),
            pl.BlockSpec((1, D_FF, d), lambda i, te, nu: (te[i], 0, 0)),
            pl.BlockSpec((1, 1, d), lambda i, te, nu: (te[i], 0, 0)),
        ],
        out_specs=pl.BlockSpec((tm * ROW_TILES, LANES), lambda i, te, nu: (i, 0)),
        scratch_shapes=[
            pltpu.VMEM((d, 2 * D_FF), BF16),
            pltpu.VMEM((D_FF, d), BF16),
        ],
    )
    return pl.pallas_call(
        _experts_kernel,
        grid_spec=grid_spec,
        out_shape=jax.ShapeDtypeStruct(xs.shape, F32),
        compiler_params=_params("arbitrary"),
        name="experts",
    )(tile_expert, n_used, xs, w_in, b_in, w_out, b_out)


COMB_TM = 256


def _combine_kernel(pos_ref, posn_ref, y_ref, gate_ref, x1_ref, mod_ref, pg_ref, o_ref, buf_ref, sem):
    tm = COMB_TM
    i = pl.program_id(0)
    slot = i & 1

    def gather(p_ref, s):
        def issue(g, carry):
            for u in range(ISSUE_UNROLL):
                t = g * ISSUE_UNROLL + u
                for kk in range(TOP_K):
                    _row_copy(y_ref, p_ref[0, 0, t * TOP_K + kk], buf_ref.at[s, kk], t,
                              sem.at[s]).start(priority=kk % 2)
            return carry

        lax.fori_loop(0, tm // ISSUE_UNROLL, issue, 0)

    @pl.when(i == 0)
    def _():
        gather(pos_ref, 0)

    @pl.when(i + 1 < pl.num_programs(0))
    def _():
        gather(posn_ref, 1 - slot)

    for kk in range(TOP_K):
        pltpu.make_async_copy(y_ref.at[pl.ds(0, tm * ROW_TILES)], buf_ref.at[slot, kk],
                              sem.at[slot]).wait()

    gates = gate_ref[...]
    acc = None
    for kk in range(TOP_K):
        rows = jnp.concatenate(
            [buf_ref[slot, kk, pl.ds(j, tm, stride=ROW_TILES), :] for j in range(ROW_TILES)], axis=1)
        term = rows * gates[:, kk:kk + 1]
        acc = term if acc is None else acc + term
    o_ref[...] = x1_ref[...] + mod_ref[0, 5:6, :] * (_rms(acc) * pg_ref[...])


def _combine_call(pos3, y, gates, x1, mod3, post_gain, seq):
    t, d = x1.shape
    tm = COMB_TM
    per_b = seq // tm
    n_steps = t // tm
    return pl.pallas_call(
        _combine_kernel,
        grid=(n_steps,),
        in_specs=[
            pl.BlockSpec((1, 1, tm * TOP_K), lambda i: (i, 0, 0), memory_space=pltpu.SMEM),
            pl.BlockSpec((1, 1, tm * TOP_K), lambda i: (jnp.minimum(i + 1, n_steps - 1), 0, 0),
                         memory_space=pltpu.SMEM),
            pl.BlockSpec(memory_space=pl.ANY),
            pl.BlockSpec((tm, LANES), lambda i: (i, 0)),
            pl.BlockSpec((tm, d), lambda i: (i, 0)),
            pl.BlockSpec((1, N_MOD, d), lambda i: (i // per_b, 0, 0)),
            pl.BlockSpec((1, d), lambda i: (0, 0)),
        ],
        out_specs=pl.BlockSpec((tm, d), lambda i: (i, 0)),
        out_shape=jax.ShapeDtypeStruct((t, d), F32),
        scratch_shapes=[
            pltpu.VMEM((2, TOP_K, tm * ROW_TILES, LANES), F32),
            pltpu.SemaphoreType.DMA((2,)),
        ],
        compiler_params=_params("arbitrary"),
        name="combine",
    )(pos3, pos3, y, gates, x1, mod3, post_gain)


def _pad_cols(w, n):
    return jnp.pad(w, ((0, 0), (0, n - w.shape[1])))


def kernel(x, c, w_ada, b_ada, attn_pre_gain, attn_post_gain, w_in, w_gk2, b_gk, gla_norm_gain, b_f,
           w_o, mlp_pre_gain, mlp_post_gain, w_router, b_router, w_e_in, b_e_in, w_e_out, b_e_out):
    batch, seq, d = x.shape
    t = batch * seq
    x2 = x.reshape(t, d)

    o = 0
    cols = {}
    for name, width in (("qg", GLA_QK), ("kg", GLA_QK), ("vg", GLA_WIDTH), ("gg", GLA_WIDTH),
                        ("zg", GLA_LOWRANK), ("qf", FOX_WIDTH), ("kf", FOX_WIDTH), ("vf", FOX_WIDTH),
                        ("ff", FOX_HEADS)):
        cols[name] = w_in[:, o:o + width]
        o += width
    gla_cols = 2 * GLA_QK + 2 * GLA_WIDTH
    fox_start = gla_cols + GLA_LOWRANK
    w_main = jnp.concatenate([w_in[:, :gla_cols], w_in[:, fox_start:fox_start + 2 * FOX_WIDTH]],
                             axis=1).astype(BF16)
    w_vt = cols["vf"].T.astype(BF16)
    w_small = _pad_cols(jnp.concatenate([cols["ff"], cols["zg"]], axis=1), LANES)
    ws_hi, ws_lo = _split2(w_small)
    wg = jnp.zeros((LANES, GLA_QK), F32).at[SMALL_ZG:SMALL_ZG + GLA_LOWRANK].set(w_gk2)
    wg_hi, wg_lo = _split2(wg)
    bf_pad = jnp.zeros((1, LANES), F32).at[0, SMALL_FF:SMALL_FF + FOX_HEADS].set(b_f)
    wr_hi, wr_lo = _split2(_pad_cols(w_router, LANES))
    br_pad = _pad_cols(b_router.reshape(1, N_EXPERTS), LANES)
    wa_hi, wa_lo = _split2(w_ada)

    c_pad = jnp.pad(c, ((0, SUBLANES - batch % SUBLANES if batch % SUBLANES else 0), (0, 0)))
    mod = _mod_call(c_pad, wa_hi, wa_lo, b_ada.reshape(1, -1))
    mod3 = mod[:batch].reshape(batch, N_MOD, d)

    proj, vt, small = _inproj_call(x2, mod3, attn_pre_gain.reshape(1, d), w_main, w_vt, ws_hi, ws_lo,
                                   seq)
    o_gla, kx, frow = _gla_call(proj, small, wg_hi, wg_lo, b_gk.reshape(1, -1),
                                gla_norm_gain.reshape(1, -1), bf_pad, batch, seq)
    o_fox = _fox_call(proj, kx, vt, frow, batch, seq)

    x1, h2, idx_w, rank_w, gate_w, cnt = _post_call(
        o_gla, o_fox, x2, mod3, attn_post_gain.reshape(1, d), mlp_pre_gain.reshape(1, d),
        w_o.astype(BF16), wr_hi, wr_lo, br_pad, seq)

    tm = MOE_TM
    n_tiles = (t * TOP_K) // tm + N_EXPERTS
    counts = cnt[0, :N_EXPERTS].astype(jnp.int32)
    padded = ((counts + tm - 1) // tm) * tm
    pend = jnp.cumsum(padded)
    pstart = pend - padded
    experts = jnp.arange(N_EXPERTS, dtype=jnp.int32)
    first_slot = jnp.sum(jnp.where(idx_w[:, :TOP_K, None] == experts, pstart.astype(jnp.int32), 0),
                         axis=-1)
    pos = first_slot + rank_w[:, :TOP_K]
    n_used = (pend[-1] // tm).astype(jnp.int32).reshape(1)
    tile_start = jnp.arange(n_tiles, dtype=jnp.int32) * tm
    tile_expert = jnp.minimum(
        jnp.sum((pend[None, :] <= tile_start[:, None]).astype(jnp.int32), axis=1), N_EXPERTS - 1)

    pos_d = pos.reshape(t // DISP_TM, 1, DISP_TM * TOP_K)
    xs = _dispatch_call(pend.astype(jnp.int32), pos_d, h2, n_tiles)
    y = _experts_call(tile_expert, n_used, xs, w_e_in, b_e_in.reshape(N_EXPERTS, 1, -1), w_e_out,
                      b_e_out.reshape(N_EXPERTS, 1, -1))
    pos_c = pos.reshape(t // COMB_TM, 1, COMB_TM * TOP_K)
    out = _combine_call(pos_c, y, gate_w, x1, mod3, mlp_post_gain.reshape(1, d), seq)
    return out.reshape(batch, seq, d)
```

```python
import functools

import jax
import jax.numpy as jnp
from jax import lax
from jax.experimental import pallas as pl
from jax.experimental.pallas import tpu as pltpu

F32 = jnp.float32
BF16 = jnp.bfloat16

D_MODEL = 1024
N_MOD = 6
GLA_HEADS = 4
GLA_DK = 64
GLA_DV = 128
GLA_QK = GLA_HEADS * GLA_DK
GLA_WIDTH = GLA_HEADS * GLA_DV
GLA_LOWRANK = 16
GLA_CHUNK = 64
FOX_HEADS = 8
FOX_DH = 64
FOX_WIDTH = FOX_HEADS * FOX_DH
N_EXPERTS = 32
TOP_K = 4
D_FF = D_MODEL
SWIGLU_ALPHA = 1.702
SWIGLU_LIMIT = 7.0
EPS = 1e-6

LANES = 128
SUBLANES = 8
ROW_TILES = D_MODEL // LANES

COL_QG, COL_KG, COL_VG, COL_GG = 0, 256, 512, 1024
COL_QF, COL_KF = 1536, 2048
MAIN_COLS = 2560
LOG2E = 1.4426950408889634
FOX_Q_SCALE = (FOX_DH ** -0.5) * LOG2E
FOX_BIAS_LANES = 6
BF16_SUBLANES = 16
FOX_VROWS = FOX_DH + BF16_SUBLANES
SMALL_FF, SMALL_ZG = 0, 8

NEG = -1e30
VMEM_LIMIT = 56 * 1024 * 1024


def _dot(a, b):
    return jnp.dot(a, b, preferred_element_type=F32)


def _dot_nt(a, b):
    return lax.dot_general(a, b, (((1,), (1,)), ((), ())), preferred_element_type=F32)


def _dot_tn(a, b):
    return lax.dot_general(a, b, (((0,), (0,)), ((), ())), preferred_element_type=F32)


def _split2(a):
    hi = a.astype(BF16)
    lo = (a - hi.astype(F32)).astype(BF16)
    return hi, lo


def _split3(a):
    hi = a.astype(BF16)
    r = a - hi.astype(F32)
    mid = r.astype(BF16)
    lo = (r - mid.astype(F32)).astype(BF16)
    return hi, mid, lo


def _log_sigmoid(x):
    return jnp.minimum(x, 0.0) - jnp.log1p(jnp.exp(-jnp.abs(x)))


def _rms(x):
    return x * lax.rsqrt(jnp.mean(x * x, axis=-1, keepdims=True) + EPS)


def _params(*sem, flags=None):
    return pltpu.CompilerParams(dimension_semantics=sem, vmem_limit_bytes=VMEM_LIMIT, flags=flags)


def _mod_kernel(c_ref, wh_ref, wl_ref, b_ref, o_ref):
    c = c_ref[...]
    s = c * jax.nn.sigmoid(c)
    sh, sl = _split2(s)
    o_ref[...] = _dot(sh, wh_ref[...]) + (_dot(sh, wl_ref[...]) + _dot(sl, wh_ref[...])) + b_ref[...]


def _mod_call(c_pad, w_hi, w_lo, b_ada):
    rows, d = c_pad.shape
    n = w_hi.shape[1]
    tn = 1024
    return pl.pallas_call(
        _mod_kernel,
        grid=(n // tn,),
        in_specs=[
            pl.BlockSpec((rows, d), lambda j: (0, 0)),
            pl.BlockSpec((d, tn), lambda j: (0, j)),
            pl.BlockSpec((d, tn), lambda j: (0, j)),
            pl.BlockSpec((1, tn), lambda j: (0, j)),
        ],
        out_specs=pl.BlockSpec((rows, tn), lambda j: (0, j)),
        out_shape=jax.ShapeDtypeStruct((rows, n), F32),
        compiler_params=_params("parallel"),
        name="mod",
    )(c_pad, w_hi, w_lo, b_ada)


def _inproj_kernel(x_ref, mod_ref, g_ref, w_ref, wvt_ref, wsh_ref, wsl_ref, proj_ref, vt_ref, small_ref):
    h = _rms(x_ref[...]) * g_ref[...]
    h = h * (1.0 + mod_ref[0, 1:2, :]) + mod_ref[0, 0:1, :]
    hh, hl = _split2(h)
    cw = 512
    for j in range(MAIN_COLS // cw):
        r = _dot(hh, w_ref[:, j * cw:(j + 1) * cw])
        if j * cw == COL_QF:
            r = r * FOX_Q_SCALE
        proj_ref[:, j * cw:(j + 1) * cw] = r.astype(BF16)
    vt = _dot_nt(wvt_ref[...], hh).astype(BF16)
    ones = jnp.ones((FOX_VROWS - FOX_DH, vt.shape[1]), BF16)
    for h in range(FOX_HEADS):
        vt_ref[h * FOX_VROWS:h * FOX_VROWS + FOX_DH, :] = vt[h * FOX_DH:(h + 1) * FOX_DH, :]
        vt_ref[h * FOX_VROWS + FOX_DH:(h + 1) * FOX_VROWS, :] = ones
    a = _dot(hh, wsl_ref[...])
    small_ref[...] = a[:, :LANES] + (a[:, LANES:] + _dot(hl, wsh_ref[...]))


def _inproj_call(x2, mod3, gain, w_main, w_vt, ws_hi, ws_lo, seq):
    ws_hilo = jnp.concatenate([ws_hi, ws_lo], axis=1)
    t, d = x2.shape
    tm = 512
    per_b = seq // tm
    return pl.pallas_call(
        _inproj_kernel,
        grid=(t // tm,),
        in_specs=[
            pl.BlockSpec((tm, d), lambda i: (i, 0)),
            pl.BlockSpec((1, N_MOD, d), lambda i: (i // per_b, 0, 0)),
            pl.BlockSpec((1, d), lambda i: (0, 0)),
            pl.BlockSpec((d, MAIN_COLS), lambda i: (0, 0)),
            pl.BlockSpec((FOX_WIDTH, d), lambda i: (0, 0)),
            pl.BlockSpec((d, LANES), lambda i: (0, 0)),
            pl.BlockSpec((d, 2 * LANES), lambda i: (0, 0)),
        ],
        out_specs=[
            pl.BlockSpec((tm, MAIN_COLS), lambda i: (i, 0)),
            pl.BlockSpec((FOX_HEADS * FOX_VROWS, tm), lambda i: (0, i)),
            pl.BlockSpec((tm, LANES), lambda i: (i, 0)),
        ],
        out_shape=[
            jax.ShapeDtypeStruct((t, MAIN_COLS), BF16),
            jax.ShapeDtypeStruct((FOX_HEADS * FOX_VROWS, t), BF16),
            jax.ShapeDtypeStruct((t, LANES), F32),
        ],
        compiler_params=_params("parallel"),
        name="inproj",
    )(x2, mod3, gain, w_main, w_vt, ws_hi, ws_hilo)


GLA_TM = 512
GLA_SUB = 2
GLA_BLK = 256


def _gla_kernel(q_ref, k_ref, v_ref, g_ref, small_ref, wgh_ref, wgl_ref, bgk_ref, gain_ref, bf_ref,
                sel_ref, ones_ref, o_ref, kx_ref, frow_ref, st_ref, fcar_ref, oacc_ref):
    @pl.when(pl.program_id(1) == 0)
    def _():
        st_ref[...] = jnp.zeros_like(st_ref)
        fcar_ref[...] = jnp.zeros_like(fcar_ref)

    subs = [pl.ds(s * GLA_TM, GLA_TM) for s in range(GLA_SUB)]
    fronts = [_gla_sub_front(q_ref.at[r], k_ref.at[r], v_ref.at[r], small_ref.at[r],
                             wgh_ref, wgl_ref, bgk_ref, oacc_ref.at[r]) for r in subs]
    for r, front in zip(subs, fronts):
        _gla_sub_chain(front, v_ref.at[r], g_ref.at[r], gain_ref, bf_ref, sel_ref, ones_ref,
                       o_ref.at[r], kx_ref.at[r], frow_ref.at[:, :, :, r], st_ref, fcar_ref,
                       oacc_ref.at[r])


def _gla_sub_front(q_ref, k_ref, v_ref, small_ref, wgh_ref, wgl_ref, bgk_ref, oacc_ref):
    tm = GLA_TM
    nc = tm // GLA_CHUNK
    small = small_ref[...]
    sh, sl = _split2(small)
    gk = _dot(sh, wgh_ref[...]) + (_dot(sh, wgl_ref[...]) + _dot(sl, wgh_ref[...])) + bgk_ref[...]
    log_a = _log_sigmoid(gk) * (1.0 / 16.0)

    blk = GLA_BLK
    blocks = [slice(j * blk, (j + 1) * blk) for j in range(tm // blk)]
    row = lax.broadcasted_iota(jnp.int32, (blk, blk), 0)
    col = lax.broadcasted_iota(jnp.int32, (blk, blk), 1)
    same = (row >> 6) == (col >> 6)
    lower = col <= row
    tri = jnp.logical_and(same, lower)
    tri_b = jnp.where(tri, 1.0, 0.0).astype(BF16)
    lower_b = jnp.where(lower, 1.0, 0.0).astype(BF16)

    lh, ll = _split2(log_a)
    b = jnp.concatenate([_dot(tri_b, lh[r]) + _dot(tri_b, ll[r]) for r in blocks], axis=0)
    btot = jnp.concatenate(
        [jnp.broadcast_to(b[(c + 1) * GLA_CHUNK - 1:(c + 1) * GLA_CHUNK, :], (GLA_CHUNK, GLA_QK))
         for c in range(nc)], axis=0)

    q = q_ref[...].astype(F32)
    k = k_ref[...].astype(F32)
    qt = q * jnp.exp(b) * (GLA_DK ** -0.5)
    kt = (k * jnp.exp(-b)).astype(BF16)
    kd = (k * jnp.exp(btot - b)).astype(BF16)

    lane_qk = lax.broadcasted_iota(jnp.int32, (1, GLA_QK), 1)
    head_masks = [jnp.logical_and(lane_qk >= h * GLA_DK, lane_qk < (h + 1) * GLA_DK)
                  for h in range(GLA_HEADS)]

    for h in range(GLA_HEADS):
        cs = slice(h * GLA_DV, (h + 1) * GLA_DV)
        qm = jnp.where(head_masks[h], qt, 0.0).astype(BF16)
        for r in blocks:
            att = jnp.where(tri, _dot_nt(qm[r], kt[r]), 0.0).astype(BF16)
            oacc_ref[r, cs] = _dot(att, v_ref[r, cs])
    return small, qt, kd, btot, head_masks, lower_b


def _gla_sub_chain(front, v_ref, g_ref, gain_ref, bf_ref, sel_ref, ones_ref, o_ref, kx_ref, frow_ref,
                   st_ref, fcar_ref, oacc_ref):
    tm = GLA_TM
    nc = tm // GLA_CHUNK
    small, qt, kd, btot, head_masks, lower_b = front

    chunks = [slice(c * GLA_CHUNK, (c + 1) * GLA_CHUNK) for c in range(nc)]
    kvts = [_dot_tn(v_ref[rs, :], kd[rs]) for rs in chunks]
    states = []
    st = st_ref[...]
    for c, rs in enumerate(chunks):
        states.append(st.astype(BF16))
        decay = jnp.exp(btot[c * GLA_CHUNK:c * GLA_CHUNK + 1, :])
        new = decay * st
        for h in range(GLA_HEADS):
            new = new + jnp.where(head_masks[h], kvts[c][h * GLA_DV:(h + 1) * GLA_DV, :], 0.0)
        st = new
    st_ref[...] = st
    for c, rs in enumerate(chunks):
        qc = qt[rs]
        qs = jnp.concatenate([jnp.where(head_masks[h], qc, 0.0) for h in range(GLA_HEADS)],
                             axis=0).astype(BF16)
        oi = _dot_nt(qs, states[c])
        for h in range(GLA_HEADS):
            oacc_ref[rs, h * GLA_DV:(h + 1) * GLA_DV] += oi[h * GLA_CHUNK:(h + 1) * GLA_CHUNK, :]

    for h in range(GLA_HEADS):
        cs = slice(h * GLA_DV, (h + 1) * GLA_DV)
        y = _rms(oacc_ref[:, cs]) * gain_ref[:, cs]
        g = g_ref[:, cs].astype(F32)
        o_ref[:, cs] = (y * (g * jax.nn.sigmoid(g))).astype(BF16)

    lane = lax.broadcasted_iota(jnp.int32, (1, LANES), 1)
    lf = jnp.where(lane < FOX_HEADS, _log_sigmoid(small + bf_ref[...]), 0.0)
    l1, l2, l3 = _split3(lf)
    carry = fcar_ref[...]
    parts = []
    for j in range(tm // GLA_BLK):
        r = slice(j * GLA_BLK, (j + 1) * GLA_BLK)
        part = _dot(lower_b, l1[r]) + (_dot(lower_b, l2[r]) + _dot(lower_b, l3[r])) + carry
        carry = part[GLA_BLK - 1:GLA_BLK, :]
        parts.append(part)
    fl = jnp.concatenate(parts, axis=0)
    fcar_ref[...] = carry
    fs = fl * LOG2E
    fst = fs.T
    for h in range(FOX_HEADS):
        frow_ref[0, h] = fst[h:h + 1, :]
    n1, n2, n3 = _split3(-fs)
    kx = _dot(n1, sel_ref[0]) + (_dot(n2, sel_ref[1]) + _dot(n3, sel_ref[2])) + ones_ref[...]
    kx_ref[...] = kx.astype(BF16)


def _fox_bias_selectors():
    import numpy as np
    sel = np.zeros((3, LANES, FOX_WIDTH), np.float32)
    ones = np.zeros((1, FOX_WIDTH), np.float32)
    for h in range(FOX_HEADS):
        base = (h // 2) * LANES + (h % 2) * FOX_BIAS_LANES
        for i in range(3):
            sel[i, h, base + i] = 1.0
            ones[0, base + 3 + i] = 1.0
    return jnp.asarray(sel, BF16), jnp.asarray(ones, F32)


def _gla_call(proj, small, wg_hi, wg_lo, b_gk, gain, bf_pad, batch, seq):
    t = proj.shape[0]
    tm = GLA_TM * GLA_SUB
    nb = seq // tm

    def rows(b, i):
        return b * nb + i

    sel, ones = _fox_bias_selectors()
    return pl.pallas_call(
        _gla_kernel,
        grid=(batch, nb),
        in_specs=[
            pl.BlockSpec((tm, GLA_QK), lambda b, i: (rows(b, i), COL_QG // GLA_QK)),
            pl.BlockSpec((tm, GLA_QK), lambda b, i: (rows(b, i), COL_KG // GLA_QK)),
            pl.BlockSpec((tm, GLA_WIDTH), lambda b, i: (rows(b, i), COL_VG // GLA_WIDTH)),
            pl.BlockSpec((tm, GLA_WIDTH), lambda b, i: (rows(b, i), COL_GG // GLA_WIDTH)),
            pl.BlockSpec((tm, LANES), lambda b, i: (rows(b, i), 0)),
            pl.BlockSpec((LANES, GLA_QK), lambda b, i: (0, 0)),
            pl.BlockSpec((LANES, GLA_QK), lambda b, i: (0, 0)),
            pl.BlockSpec((1, GLA_QK), lambda b, i: (0, 0)),
            pl.BlockSpec((1, GLA_WIDTH), lambda b, i: (0, 0)),
            pl.BlockSpec((1, LANES), lambda b, i: (0, 0)),
            pl.BlockSpec((3, LANES, FOX_WIDTH), lambda b, i: (0, 0, 0)),
            pl.BlockSpec((1, FOX_WIDTH), lambda b, i: (0, 0)),
        ],
        out_specs=[
            pl.BlockSpec((tm, GLA_WIDTH), lambda b, i: (rows(b, i), 0)),
            pl.BlockSpec((tm, FOX_WIDTH), lambda b, i: (rows(b, i), 0)),
            pl.BlockSpec((1, FOX_HEADS, 1, tm), lambda b, i: (b, 0, 0, i)),
        ],
        out_shape=[
            jax.ShapeDtypeStruct((t, GLA_WIDTH), BF16),
            jax.ShapeDtypeStruct((t, FOX_WIDTH), BF16),
            jax.ShapeDtypeStruct((batch, FOX_HEADS, 1, seq), F32),
        ],
        scratch_shapes=[
            pltpu.VMEM((GLA_DV, GLA_QK), F32),
            pltpu.VMEM((1, LANES), F32),
            pltpu.VMEM((tm, GLA_WIDTH), F32),
        ],
        compiler_params=_params("parallel", "arbitrary"),
        name="gla",
    )(proj, proj, proj, proj, small, wg_hi, wg_lo, b_gk, gain, bf_pad, sel, ones)


FOX_TQ = 512
FOX_TK = 512
FOX_PAIRS = 2
FOX_SPLIT = 1


def _fox_kernel(q_ref, k_ref, kx_ref, vt_ref, frow_ref, o_ref, m_ref, acc_ref, sa_ref, sb_ref,
                ma_ref, mb_ref):
    tq, tk = FOX_TQ, FOX_TK
    i = pl.program_id(2)
    q0 = pl.multiple_of(i * tq, tq)
    lane = lax.broadcasted_iota(jnp.int32, (1, LANES), 1)
    heads = range(2 * FOX_PAIRS)

    def pair_lanes(hh):
        return slice((hh // 2) * LANES, (hh // 2 + 1) * LANES)

    q_aug = []
    for hh in heads:
        head_lanes = (lane < FOX_DH) if hh % 2 == 0 else (lane >= FOX_DH)
        qm = jnp.where(head_lanes, q_ref[:, pair_lanes(hh)], 0.0).astype(BF16)
        fr = jnp.broadcast_to(frow_ref[0, hh, :, pl.ds(q0, LANES)][:, 0:1], (1, LANES))
        f1, f2, f3 = _split3(fr)
        base = (hh % 2) * FOX_BIAS_LANES
        qx = jnp.where(jnp.logical_and(lane >= base, lane < base + 3), 1.0, 0.0).astype(BF16)
        qx = jnp.where(lane == base + 3, f1, qx)
        qx = jnp.where(lane == base + 4, f2, qx)
        qx = jnp.where(lane == base + 5, f3, qx)
        q_aug.append(jnp.concatenate([qm, jnp.broadcast_to(qx, (tq, LANES))], axis=1))

    m_ref[...] = jnp.full(m_ref.shape, NEG, F32)
    acc_ref[...] = jnp.zeros_like(acc_ref)

    bufs = {"a": (sa_ref, ma_ref), "b": (sb_ref, mb_ref)}

    th = tk // FOX_SPLIT

    def produce(which, blk, hh, diagonal, part):
        buf, mx = bufs[which]
        k0 = pl.multiple_of(blk * tk + part * th, th)
        k_aug = jnp.concatenate([k_ref[pl.ds(k0, th), pair_lanes(hh)],
                                 kx_ref[pl.ds(k0, th), pair_lanes(hh)]], axis=1)
        st = _dot_nt(k_aug, q_aug[hh])
        if diagonal:
            key = lax.broadcasted_iota(jnp.int32, (th, tq), 0) + part * th
            qry = lax.broadcasted_iota(jnp.int32, (th, tq), 1)
            st = jnp.where(key <= qry, st, NEG)
        buf[hh, part * th:(part + 1) * th, :] = st
        part_max = jnp.max(st, axis=0, keepdims=True)
        mx[hh] = part_max if part == 0 else jnp.maximum(mx[hh], part_max)

    def consume(which, blk, hh, part):
        buf, mx = bufs[which]
        k0 = pl.multiple_of(blk * tk + part * th, th)
        m_old = m_ref[hh]
        m_new = jnp.maximum(m_old, mx[hh])
        p = jnp.exp2(buf[hh, part * th:(part + 1) * th, :] - m_new).astype(BF16)
        vt = vt_ref[hh * FOX_VROWS:(hh + 1) * FOX_VROWS, pl.ds(k0, th)]
        pv = _dot(vt, p)
        if part == 0:
            acc_ref[hh] = jnp.exp2(m_old - m_new) * acc_ref[hh] + pv
        else:
            acc_ref[hh] += pv
        if part == FOX_SPLIT - 1:
            m_ref[hh] = m_new

    def stage(cur, cur_blk, nxt=None, nxt_blk=None, nxt_diagonal=False):
        for hh in heads:
            for part in range(FOX_SPLIT):
                if nxt is not None:
                    produce(nxt, nxt_blk, hh, nxt_diagonal, part)
                if hh > 0:
                    consume(cur, cur_blk, hh - 1, part)
        for part in range(FOX_SPLIT):
            consume(cur, cur_blk, heads[-1], part)

    @pl.when(i == 0)
    def _():
        for hh in heads:
            for part in range(FOX_SPLIT):
                produce("a", 0, hh, True, part)
        stage("a", 0)

    @pl.when(i > 0)
    def _():
        for hh in heads:
            for part in range(FOX_SPLIT):
                produce("a", 0, hh, False, part)

    def pair(t, carry):
        b0 = 2 * t
        stage("a", b0, "b", b0 + 1)
        stage("b", b0 + 1, "a", b0 + 2)
        return carry

    n_pairs = lax.shift_right_logical(jnp.maximum(i - 1, 0), 1)
    lax.fori_loop(0, n_pairs, pair, 0)

    @pl.when((i & 1) == 1)
    def _():
        stage("a", i - 1, "b", i, True)
        stage("b", i)

    @pl.when(jnp.logical_and((i & 1) == 0, i > 0))
    def _():
        stage("a", i - 2, "b", i - 1)
        stage("b", i - 1, "a", i, True)
        stage("a", i)

    for pr in range(FOX_PAIRS):
        outs = []
        for hh in (2 * pr, 2 * pr + 1):
            acc = acc_ref[hh]
            outs.append(acc[:FOX_DH] * (1.0 / acc[FOX_DH:FOX_DH + 1]))
        o_ref[:, pr * LANES:(pr + 1) * LANES] = jnp.concatenate(outs, axis=0).T.astype(BF16)


def _fox_call(proj, kx, vt, frow, batch, seq):
    t = proj.shape[0]
    tq = FOX_TQ
    nq = seq // tq
    width = FOX_PAIRS * LANES
    groups = FOX_WIDTH // width
    nh = 2 * FOX_PAIRS
    return pl.pallas_call(
        _fox_kernel,
        grid=(batch, groups, nq),
        in_specs=[
            pl.BlockSpec((tq, width), lambda b, hp, i: (b * nq + i, COL_QF // width + hp)),
            pl.BlockSpec((seq, width), lambda b, hp, i: (b, COL_KF // width + hp)),
            pl.BlockSpec((seq, width), lambda b, hp, i: (b, hp)),
            pl.BlockSpec((nh * FOX_VROWS, seq), lambda b, hp, i: (hp, b)),
            pl.BlockSpec((1, nh, 1, seq), lambda b, hp, i: (b, hp, 0, 0)),
        ],
        out_specs=pl.BlockSpec((tq, width), lambda b, hp, i: (b * nq + i, hp)),
        out_shape=jax.ShapeDtypeStruct((t, FOX_WIDTH), BF16),
        scratch_shapes=[
            pltpu.VMEM((nh, 1, tq), F32),
            pltpu.VMEM((nh, FOX_VROWS, tq), F32),
            pltpu.VMEM((nh, FOX_TK, tq), F32),
            pltpu.VMEM((nh, FOX_TK, tq), F32),
            pltpu.VMEM((nh, 1, tq), F32),
            pltpu.VMEM((nh, 1, tq), F32),
        ],
        compiler_params=_params("parallel", "parallel", "arbitrary"),
        name="fox",
    )(proj, proj, kx, vt, frow)


POST_TM = 512
POST_SUB = 2


def _post_kernel(og_ref, of_ref, x_ref, mod_ref, pg_ref, mg_ref, wo_ref, wrh_ref, wrl_ref, br_ref,
                 x1_ref, h2_ref, idx_ref, rank_ref, gate_ref, cnt_ref, car_ref):
    @pl.when(pl.program_id(0) == 0)
    def _():
        car_ref[...] = jnp.zeros_like(car_ref)

    ys = []
    for s in range(POST_SUB):
        rows = slice(s * POST_TM, (s + 1) * POST_TM)
        ys.append(_dot(og_ref[rows, :], wo_ref[:GLA_WIDTH, :])
                  + _dot(of_ref[rows, :], wo_ref[GLA_WIDTH:, :]))
    for s in range(POST_SUB):
        _post_route(s, ys[s], x_ref, mod_ref, pg_ref, mg_ref, wrh_ref, wrl_ref, br_ref,
                    x1_ref, h2_ref, idx_ref, rank_ref, gate_ref, cnt_ref, car_ref)


def _post_route(s, y, x_ref, mod_ref, pg_ref, mg_ref, wrh_ref, wrl_ref, br_ref,
                x1_ref, h2_ref, idx_ref, rank_ref, gate_ref, cnt_ref, car_ref):
    tm = POST_TM
    rows = slice(s * tm, (s + 1) * tm)
    x1 = x_ref[rows, :] + mod_ref[0, 2:3, :] * (_rms(y) * pg_ref[...])
    x1_ref[rows, :] = x1
    h2 = _rms(x1) * mg_ref[...]
    h2 = h2 * (1.0 + mod_ref[0, 4:5, :]) + mod_ref[0, 3:4, :]
    for j in range(ROW_TILES):
        h2_ref[pl.ds(s * tm * ROW_TILES + j, tm, stride=ROW_TILES), :] = h2[:, j * LANES:(j + 1) * LANES]

    hh, hl = _split2(h2)
    logits = _dot(hh, wrh_ref[...]) + (_dot(hh, wrl_ref[...]) + _dot(hl, wrh_ref[...])) + br_ref[...]
    lane = lax.broadcasted_iota(jnp.int32, (tm, LANES), 1).astype(F32)
    work = jnp.where(lane < N_EXPERTS, logits, -jnp.inf)
    vals, idxs = [], []
    for _ in range(TOP_K):
        mx = jnp.max(work, axis=-1, keepdims=True)
        ix = jnp.min(jnp.where(work == mx, lane, float(LANES)), axis=-1, keepdims=True)
        vals.append(mx)
        idxs.append(ix)
        work = jnp.where(lane == ix, -jnp.inf, work)
    es = [jnp.exp(v - vals[0]) for v in vals]
    den = es[0] + es[1] + es[2] + es[3]

    onehot = jnp.zeros((tm, LANES), F32)
    for ix in idxs:
        onehot = onehot + jnp.where(lane == ix, 1.0, 0.0)
    row = lax.broadcasted_iota(jnp.int32, (tm, tm), 0)
    col = lax.broadcasted_iota(jnp.int32, (tm, tm), 1)
    strict_b = jnp.where(col < row, 1.0, 0.0).astype(BF16)
    prefix = _dot(strict_b, onehot.astype(BF16)) + car_ref[...]
    car_new = car_ref[...] + jnp.sum(onehot, axis=0, keepdims=True)
    car_ref[...] = car_new
    cnt_ref[...] = car_new

    idx_o = jnp.zeros((tm, LANES), F32)
    rank_o = jnp.zeros((tm, LANES), F32)
    gate_o = jnp.zeros((tm, LANES), F32)
    for kk in range(TOP_K):
        rk = jnp.sum(jnp.where(lane == idxs[kk], prefix, 0.0), axis=-1, keepdims=True)
        sel = lane == float(kk)
        idx_o = jnp.where(sel, idxs[kk], idx_o)
        rank_o = jnp.where(sel, rk, rank_o)
        gate_o = jnp.where(sel, es[kk] / den, gate_o)
    idx_ref[rows, :] = idx_o.astype(jnp.int32)
    rank_ref[rows, :] = rank_o.astype(jnp.int32)
    gate_ref[rows, :] = gate_o


def _post_call(o_gla, o_fox, x2, mod3, post_gain, mlp_gain, w_o, wr_hi, wr_lo, br_pad, seq):
    t, d = x2.shape
    tm = POST_TM * POST_SUB
    per_b = seq // tm
    return pl.pallas_call(
        _post_kernel,
        grid=(t // tm,),
        in_specs=[
            pl.BlockSpec((tm, GLA_WIDTH), lambda i: (i, 0)),
            pl.BlockSpec((tm, FOX_WIDTH), lambda i: (i, 0)),
            pl.BlockSpec((tm, d), lambda i: (i, 0)),
            pl.BlockSpec((1, N_MOD, d), lambda i: (i // per_b, 0, 0)),
            pl.BlockSpec((1, d), lambda i: (0, 0)),
            pl.BlockSpec((1, d), lambda i: (0, 0)),
            pl.BlockSpec((GLA_WIDTH + FOX_WIDTH, d), lambda i: (0, 0)),
            pl.BlockSpec((d, LANES), lambda i: (0, 0)),
            pl.BlockSpec((d, LANES), lambda i: (0, 0)),
            pl.BlockSpec((1, LANES), lambda i: (0, 0)),
        ],
        out_specs=[
            pl.BlockSpec((tm, d), lambda i: (i, 0)),
            pl.BlockSpec((tm * ROW_TILES, LANES), lambda i: (i, 0)),
            pl.BlockSpec((tm, LANES), lambda i: (i, 0)),
            pl.BlockSpec((tm, LANES), lambda i: (i, 0)),
            pl.BlockSpec((tm, LANES), lambda i: (i, 0)),
            pl.BlockSpec((1, LANES), lambda i: (0, 0)),
        ],
        out_shape=[
            jax.ShapeDtypeStruct((t, d), F32),
            jax.ShapeDtypeStruct((t * ROW_TILES, LANES), F32),
            jax.ShapeDtypeStruct((t, LANES), jnp.int32),
            jax.ShapeDtypeStruct((t, LANES), jnp.int32),
            jax.ShapeDtypeStruct((t, LANES), F32),
            jax.ShapeDtypeStruct((1, LANES), F32),
        ],
        scratch_shapes=[pltpu.VMEM((1, LANES), F32)],
        compiler_params=_params("arbitrary"),
        name="post",
    )(o_gla, o_fox, x2, mod3, post_gain, mlp_gain, w_o, wr_hi, wr_lo, br_pad)


DISP_TM = 256
DISP_SLOTS = 3
ISSUE_UNROLL = 8


def _row_copy(src_ref, src_row, dst_ref, dst_row, sem):
    return pltpu.make_async_copy(
        src_ref.at[pl.ds(pl.multiple_of(src_row * ROW_TILES, ROW_TILES), ROW_TILES)],
        dst_ref.at[pl.ds(pl.multiple_of(dst_row * ROW_TILES, ROW_TILES), ROW_TILES)],
        sem)


def _dispatch_kernel(pend_ref, pos_ref, h_hbm, xs_ref, zero_ref, hbuf_ref, sem, in_sem, out_sem):
    tm = DISP_TM
    rows = tm * ROW_TILES
    zrows = MOE_TM * ROW_TILES
    i = pl.program_id(0)
    last = pl.num_programs(0) - 1
    slot = lax.rem(i, DISP_SLOTS)
    nxt = lax.rem(i + 1, DISP_SLOTS)

    def load(step, s):
        return pltpu.make_async_copy(
            h_hbm.at[pl.ds(pl.multiple_of(step * rows, rows), rows)], hbuf_ref.at[s], in_sem.at[s])

    def drain(s):
        for _ in range(TOP_K):
            pltpu.make_async_copy(hbuf_ref.at[s], xs_ref.at[pl.ds(0, rows)], out_sem.at[s]).wait()

    @pl.when(i == 0)
    def _():
        load(0, 0).start()
        zero_ref[...] = jnp.zeros_like(zero_ref)

        def last_tile(e):
            start = pl.multiple_of((pend_ref[e] - MOE_TM) * ROW_TILES, zrows)
            return pltpu.make_async_copy(zero_ref, xs_ref.at[pl.ds(start, zrows)], sem)

        def nonempty(e):
            return pend_ref[e] > jnp.where(e == 0, 0, pend_ref[jnp.maximum(e - 1, 0)])

        def clear(e, carry):
            @pl.when(nonempty(e))
            def _():
                last_tile(e).start()
            return carry

        def clear_wait(e, carry):
            @pl.when(nonempty(e))
            def _():
                last_tile(e).wait()
            return carry

        lax.fori_loop(0, N_EXPERTS, clear, 0)
        lax.fori_loop(0, N_EXPERTS, clear_wait, 0)

        def tail_tile(j):
            return pltpu.make_async_copy(
                zero_ref, xs_ref.at[pl.ds(pl.multiple_of(j * zrows, zrows), zrows)], sem)

        def tail(j, carry):
            tail_tile(j).start()
            return carry

        def tail_wait(j, carry):
            tail_tile(j).wait()
            return carry

        first_unused = lax.div(pend_ref[N_EXPERTS - 1], MOE_TM)
        n_tiles = xs_ref.shape[0] // zrows
        lax.fori_loop(first_unused, n_tiles, tail, 0)
        lax.fori_loop(first_unused, n_tiles, tail_wait, 0)

    @pl.when(i >= DISP_SLOTS - 1)
    def _():
        drain(nxt)

    @pl.when(i < last)
    def _():
        load(i + 1, nxt).start()

    load(i, slot).wait()

    def issue(g, carry):
        for u in range(ISSUE_UNROLL):
            t = g * ISSUE_UNROLL + u
            for kk in range(TOP_K):
                _row_copy(hbuf_ref.at[slot], t, xs_ref, pos_ref[0, 0, t * TOP_K + kk],
                          out_sem.at[slot]).start(priority=kk % 2)
        return carry

    lax.fori_loop(0, tm // ISSUE_UNROLL, issue, 0)

    @pl.when(i == last)
    def _():
        for back in range(DISP_SLOTS - 1):
            @pl.when(i >= back)
            def _():
                drain(lax.rem(i - back + DISP_SLOTS, DISP_SLOTS))


def _dispatch_call(pend, pos3, h2, n_tiles):
    tm = DISP_TM
    n_steps = pos3.shape[0]
    grid_spec = pltpu.PrefetchScalarGridSpec(
        num_scalar_prefetch=1,
        grid=(n_steps,),
        in_specs=[
            pl.BlockSpec((1, 1, tm * TOP_K), lambda i, pend: (i, 0, 0), memory_space=pltpu.SMEM),
            pl.BlockSpec(memory_space=pl.ANY),
        ],
        out_specs=pl.BlockSpec(memory_space=pl.ANY),
        scratch_shapes=[
            pltpu.VMEM((MOE_TM * ROW_TILES, LANES), F32),
            pltpu.VMEM((DISP_SLOTS, tm * ROW_TILES, LANES), F32),
            pltpu.SemaphoreType.DMA,
            pltpu.SemaphoreType.DMA((DISP_SLOTS,)),
            pltpu.SemaphoreType.DMA((DISP_SLOTS,)),
        ],
    )
    return pl.pallas_call(
        _dispatch_kernel,
        grid_spec=grid_spec,
        out_shape=jax.ShapeDtypeStruct((n_tiles * MOE_TM * ROW_TILES, LANES), F32),
        compiler_params=_params("arbitrary"),
        name="dispatch",
    )(pend, pos3, h2)


MOE_TM = 512


def _experts_kernel(te_ref, nu_ref, x_ref, win_ref, bin_ref, wout_ref, bout_ref, y_ref,
                    wib_ref, wob_ref):
    tm = MOE_TM
    i = pl.program_id(0)
    used = i < nu_ref[0]

    @pl.when(jnp.logical_not(used))
    def _():
        y_ref[...] = jnp.zeros_like(y_ref)

    fresh = jnp.logical_or(i == 0, te_ref[i] != te_ref[jnp.maximum(i - 1, 0)])

    @pl.when(jnp.logical_and(used, fresh))
    def _():
        wib_ref[...] = win_ref[0].astype(BF16)
        wob_ref[...] = wout_ref[0].astype(BF16)

    @pl.when(used)
    def _():
        x = jnp.concatenate([x_ref[pl.ds(j, tm, stride=ROW_TILES), :] for j in range(ROW_TILES)],
                            axis=1).astype(BF16)
        u = _dot(x, wib_ref[...]) + bin_ref[0]
        glu = jnp.minimum(u[:, :D_FF], SWIGLU_LIMIT)
        lin = jnp.clip(u[:, D_FF:], -SWIGLU_LIMIT, SWIGLU_LIMIT)
        a = glu * jax.nn.sigmoid(SWIGLU_ALPHA * glu) * (lin + 1.0)
        y = _dot(a.astype(BF16), wob_ref[...]) + bout_ref[0]
        for j in range(ROW_TILES):
            y_ref[pl.ds(j, tm, stride=ROW_TILES), :] = y[:, j * LANES:(j + 1) * LANES]


def _experts_call(tile_expert, n_used, xs, w_in, b_in, w_out, b_out):
    tm = MOE_TM
    n_tiles = tile_expert.shape[0]
    d = D_MODEL

    def tile(i, te, nu):
        return (jnp.minimum(i, nu[0] - 1), 0)

    grid_spec = pltpu.PrefetchScalarGridSpec(
        num_scalar_prefetch=2,
        grid=(n_tiles,),
        in_specs=[
            pl.BlockSpec((tm * ROW_TILES, LANES), tile),
            pl.BlockSpec((1, d, 2 * D_FF), lambda i, te, nu: (te[i], 0, 0)),
            pl.BlockSpec((1, 1, 2 * D_FF), lambda i, te, nu: (te[i], 0, 0)),
            pl.BlockSpec((1, D_FF, d), lambda i, te, nu: (te[i], 0, 0)),
            pl.BlockSpec((1, 1, d), lambda i, te, nu: (te[i], 0, 0)),
        ],
        out_specs=pl.BlockSpec((tm * ROW_TILES, LANES), lambda i, te, nu: (i, 0)),
        scratch_shapes=[
            pltpu.VMEM((d, 2 * D_FF), BF16),
            pltpu.VMEM((D_FF, d), BF16),
        ],
    )
    return pl.pallas_call(
        _experts_kernel,
        grid_spec=grid_spec,
        out_shape=jax.ShapeDtypeStruct(xs.shape, F32),
        compiler_params=_params("arbitrary"),
        name="experts",
    )(tile_expert, n_used, xs, w_in, b_in, w_out, b_out)


COMB_TM = 256


def _combine_kernel(pos_ref, posn_ref, y_ref, gate_ref, x1_ref, mod_ref, pg_ref, o_ref, buf_ref, sem):
    tm = COMB_TM
    i = pl.program_id(0)
    slot = i & 1

    def gather(p_ref, s):
        def issue(g, carry):
            for u in range(ISSUE_UNROLL):
                t = g * ISSUE_UNROLL + u
                for kk in range(TOP_K):
                    _row_copy(y_ref, p_ref[0, 0, t * TOP_K + kk], buf_ref.at[s, kk], t,
                              sem.at[s]).start(priority=kk % 2)
            return carry

        lax.fori_loop(0, tm // ISSUE_UNROLL, issue, 0)

    @pl.when(i == 0)
    def _():
        gather(pos_ref, 0)

    @pl.when(i + 1 < pl.num_programs(0))
    def _():
        gather(posn_ref, 1 - slot)

    for kk in range(TOP_K):
        pltpu.make_async_copy(y_ref.at[pl.ds(0, tm * ROW_TILES)], buf_ref.at[slot, kk],
                              sem.at[slot]).wait()

    gates = gate_ref[...]
    acc = None
    for kk in range(TOP_K):
        rows = jnp.concatenate(
            [buf_ref[slot, kk, pl.ds(j, tm, stride=ROW_TILES), :] for j in range(ROW_TILES)], axis=1)
        term = rows * gates[:, kk:kk + 1]
        acc = term if acc is None else acc + term
    o_ref[...] = x1_ref[...] + mod_ref[0, 5:6, :] * (_rms(acc) * pg_ref[...])


def _combine_call(pos3, y, gates, x1, mod3, post_gain, seq):
    t, d = x1.shape
    tm = COMB_TM
    per_b = seq // tm
    n_steps = t // tm
    return pl.pallas_call(
        _combine_kernel,
        grid=(n_steps,),
        in_specs=[
            pl.BlockSpec((1, 1, tm * TOP_K), lambda i: (i, 0, 0), memory_space=pltpu.SMEM),
            pl.BlockSpec((1, 1, tm * TOP_K), lambda i: (jnp.minimum(i + 1, n_steps - 1), 0, 0),
                         memory_space=pltpu.SMEM),
            pl.BlockSpec(memory_space=pl.ANY),
            pl.BlockSpec((tm, LANES), lambda i: (i, 0)),
            pl.BlockSpec((tm, d), lambda i: (i, 0)),
            pl.BlockSpec((1, N_MOD, d), lambda i: (i // per_b, 0, 0)),
            pl.BlockSpec((1, d), lambda i: (0, 0)),
        ],
        out_specs=pl.BlockSpec((tm, d), lambda i: (i, 0)),
        out_shape=jax.ShapeDtypeStruct((t, d), F32),
        scratch_shapes=[
            pltpu.VMEM((2, TOP_K, tm * ROW_TILES, LANES), F32),
            pltpu.SemaphoreType.DMA((2,)),
        ],
        compiler_params=_params("arbitrary"),
        name="combine",
    )(pos3, pos3, y, gates, x1, mod3, post_gain)


def _pad_cols(w, n):
    return jnp.pad(w, ((0, 0), (0, n - w.shape[1])))


def kernel(x, c, w_ada, b_ada, attn_pre_gain, attn_post_gain, w_in, w_gk2, b_gk, gla_norm_gain, b_f,
           w_o, mlp_pre_gain, mlp_post_gain, w_router, b_router, w_e_in, b_e_in, w_e_out, b_e_out):
    batch, seq, d = x.shape
    t = batch * seq
    x2 = x.reshape(t, d)

    o = 0
    cols = {}
    for name, width in (("qg", GLA_QK), ("kg", GLA_QK), ("vg", GLA_WIDTH), ("gg", GLA_WIDTH),
                        ("zg", GLA_LOWRANK), ("qf", FOX_WIDTH), ("kf", FOX_WIDTH), ("vf", FOX_WIDTH),
                        ("ff", FOX_HEADS)):
        cols[name] = w_in[:, o:o + width]
        o += width
    gla_cols = 2 * GLA_QK + 2 * GLA_WIDTH
    fox_start = gla_cols + GLA_LOWRANK
    w_main = jnp.concatenate([w_in[:, :gla_cols], w_in[:, fox_start:fox_start + 2 * FOX_WIDTH]],
                             axis=1).astype(BF16)
    w_vt = cols["vf"].T.astype(BF16)
    w_small = _pad_cols(jnp.concatenate([cols["ff"], cols["zg"]], axis=1), LANES)
    ws_hi, ws_lo = _split2(w_small)
    wg = jnp.zeros((LANES, GLA_QK), F32).at[SMALL_ZG:SMALL_ZG + GLA_LOWRANK].set(w_gk2)
    wg_hi, wg_lo = _split2(wg)
    bf_pad = jnp.zeros((1, LANES), F32).at[0, SMALL_FF:SMALL_FF + FOX_HEADS].set(b_f)
    wr_hi, wr_lo = _split2(_pad_cols(w_router, LANES))
    br_pad = _pad_cols(b_router.reshape(1, N_EXPERTS), LANES)
    wa_hi, wa_lo = _split2(w_ada)

    c_pad = jnp.pad(c, ((0, SUBLANES - batch % SUBLANES if batch % SUBLANES else 0), (0, 0)))
    mod = _mod_call(c_pad, wa_hi, wa_lo, b_ada.reshape(1, -1))
    mod3 = mod[:batch].reshape(batch, N_MOD, d)

    proj, vt, small = _inproj_call(x2, mod3, attn_pre_gain.reshape(1, d), w_main, w_vt, ws_hi, ws_lo,
                                   seq)
    o_gla, kx, frow = _gla_call(proj, small, wg_hi, wg_lo, b_gk.reshape(1, -1),
                                gla_norm_gain.reshape(1, -1), bf_pad, batch, seq)
    o_fox = _fox_call(proj, kx, vt, frow, batch, seq)

    x1, h2, idx_w, rank_w, gate_w, cnt = _post_call(
        o_gla, o_fox, x2, mod3, attn_post_gain.reshape(1, d), mlp_pre_gain.reshape(1, d),
        w_o.astype(BF16), wr_hi, wr_lo, br_pad, seq)

    tm = MOE_TM
    n_tiles = (t * TOP_K) // tm + N_EXPERTS
    counts = cnt[0, :N_EXPERTS].astype(jnp.int32)
    padded = ((counts + tm - 1) // tm) * tm
    pend = jnp.cumsum(padded)
    pstart = pend - padded
    experts = jnp.arange(N_EXPERTS, dtype=jnp.int32)
    first_slot = jnp.sum(jnp.where(idx_w[:, :TOP_K, None] == experts, pstart.astype(jnp.int32), 0),
                         axis=-1)
    pos = first_slot + rank_w[:, :TOP_K]
    n_used = (pend[-1] // tm).astype(jnp.int32).reshape(1)
    tile_start = jnp.arange(n_tiles, dtype=jnp.int32) * tm
    tile_expert = jnp.minimum(
        jnp.sum((pend[None, :] <= tile_start[:, None]).astype(jnp.int32), axis=1), N_EXPERTS - 1)

    pos_d = pos.reshape(t // DISP_TM, 1, DISP_TM * TOP_K)
    xs = _dispatch_call(pend.astype(jnp.int32), pos_d, h2, n_tiles)
    y = _experts_call(tile_expert, n_used, xs, w_e_in, b_e_in.reshape(N_EXPERTS, 1, -1), w_e_out,
                      b_e_out.reshape(N_EXPERTS, 1, -1))
    pos_c = pos.reshape(t // COMB_TM, 1, COMB_TM * TOP_K)
    out = _combine_call(pos_c, y, gate_w, x1, mod3, mlp_post_gain.reshape(1, d), seq)
    return out.reshape(batch, seq, d)
```

```python
import functools

import jax
import jax.numpy as jnp
from jax import lax
from jax.experimental import pallas as pl
from jax.experimental.pallas import tpu as pltpu

F32 = jnp.float32
BF16 = jnp.bfloat16

D_MODEL = 1024
N_MOD = 6
GLA_HEADS = 4
GLA_DK = 64
GLA_DV = 128
GLA_QK = GLA_HEADS * GLA_DK
GLA_WIDTH = GLA_HEADS * GLA_DV
GLA_LOWRANK = 16
GLA_CHUNK = 64
FOX_HEADS = 8
FOX_DH = 64
FOX_WIDTH = FOX_HEADS * FOX_DH
N_EXPERTS = 32
TOP_K = 4
D_FF = D_MODEL
SWIGLU_ALPHA = 1.702
SWIGLU_LIMIT = 7.0
EPS = 1e-6

LANES = 128
SUBLANES = 8
ROW_TILES = D_MODEL // LANES

COL_QG, COL_KG, COL_VG, COL_GG = 0, 256, 512, 1024
COL_QF, COL_KF = 1536, 2048
MAIN_COLS = 2560
LOG2E = 1.4426950408889634
FOX_Q_SCALE = (FOX_DH ** -0.5) * LOG2E
FOX_BIAS_LANES = 6
BF16_SUBLANES = 16
FOX_VROWS = FOX_DH + BF16_SUBLANES
SMALL_FF, SMALL_ZG = 0, 8

NEG = -1e30
VMEM_LIMIT = 56 * 1024 * 1024


def _dot(a, b):
    return jnp.dot(a, b, preferred_element_type=F32)


def _dot_nt(a, b):
    return lax.dot_general(a, b, (((1,), (1,)), ((), ())), preferred_element_type=F32)


def _dot_tn(a, b):
    return lax.dot_general(a, b, (((0,), (0,)), ((), ())), preferred_element_type=F32)


def _split2(a):
    hi = a.astype(BF16)
    lo = (a - hi.astype(F32)).astype(BF16)
    return hi, lo


def _split3(a):
    hi = a.astype(BF16)
    r = a - hi.astype(F32)
    mid = r.astype(BF16)
    lo = (r - mid.astype(F32)).astype(BF16)
    return hi, mid, lo


def _log_sigmoid(x):
    return jnp.minimum(x, 0.0) - jnp.log1p(jnp.exp(-jnp.abs(x)))


def _rms(x):
    return x * lax.rsqrt(jnp.mean(x * x, axis=-1, keepdims=True) + EPS)


def _params(*sem, flags=None):
    return pltpu.CompilerParams(dimension_semantics=sem, vmem_limit_bytes=VMEM_LIMIT, flags=flags)


def _mod_kernel(c_ref, w_ref, b_ref, o_ref):
    c = c_ref[...]
    s = c * jax.nn.sigmoid(c)
    sh, sl = _split2(s)
    wh, wl = _split2(w_ref[...])
    o_ref[...] = _dot(sh, wh) + (_dot(sh, wl) + _dot(sl, wh)) + b_ref[...]


def _mod_call(c_pad, w_ada, b_ada):
    rows, d = c_pad.shape
    n = w_ada.shape[1]
    tn = 1024
    return pl.pallas_call(
        _mod_kernel,
        grid=(n // tn,),
        in_specs=[
            pl.BlockSpec((rows, d), lambda j: (0, 0)),
            pl.BlockSpec((d, tn), lambda j: (0, j)),
            pl.BlockSpec((1, tn), lambda j: (0, j)),
        ],
        out_specs=pl.BlockSpec((rows, tn), lambda j: (0, j)),
        out_shape=jax.ShapeDtypeStruct((rows, n), F32),
        compiler_params=_params("parallel"),
        name="mod",
    )(c_pad, w_ada, b_ada)


INPROJ_TM = 512
INPROJ_SUB = 2


def _inproj_kernel(x_ref, mod_ref, g_ref, w_ref, wvt_ref, wsh_ref, wsl_ref, proj_ref, vt_ref, small_ref):
    subs = [pl.ds(s * INPROJ_TM, INPROJ_TM) for s in range(INPROJ_SUB)]
    hs = []
    for r in subs:
        h = _rms(x_ref[r, :]) * g_ref[...]
        h = h * (1.0 + mod_ref[0, 1:2, :]) + mod_ref[0, 0:1, :]
        hs.append(_split2(h))
    for r, (hh, hl) in zip(subs, hs):
        _inproj_sub(hh, hl, w_ref, wvt_ref, wsh_ref, wsl_ref, proj_ref.at[r], vt_ref.at[:, r],
                    small_ref.at[r])


def _inproj_sub(hh, hl, w_ref, wvt_ref, wsh_ref, wsl_ref, proj_ref, vt_ref, small_ref):
    cw = 512
    for j in range(MAIN_COLS // cw):
        r = _dot(hh, w_ref[:, j * cw:(j + 1) * cw])
        if j * cw == COL_QF:
            r = r * FOX_Q_SCALE
        proj_ref[:, j * cw:(j + 1) * cw] = r.astype(BF16)
    vt = _dot_nt(wvt_ref[...], hh).astype(BF16)
    ones = jnp.ones((FOX_VROWS - FOX_DH, vt.shape[1]), BF16)
    for h in range(FOX_HEADS):
        vt_ref[h * FOX_VROWS:h * FOX_VROWS + FOX_DH, :] = vt[h * FOX_DH:(h + 1) * FOX_DH, :]
        vt_ref[h * FOX_VROWS + FOX_DH:(h + 1) * FOX_VROWS, :] = ones
    a = _dot(hh, wsl_ref[...])
    small_ref[...] = a[:, :LANES] + (a[:, LANES:] + _dot(hl, wsh_ref[...]))


def _inproj_call(x2, mod3, gain, w_main, w_vt, ws_hi, ws_lo, seq):
    ws_hilo = jnp.concatenate([ws_hi, ws_lo], axis=1)
    t, d = x2.shape
    tm = INPROJ_TM * INPROJ_SUB
    per_b = seq // tm
    return pl.pallas_call(
        _inproj_kernel,
        grid=(t // tm,),
        in_specs=[
            pl.BlockSpec((tm, d), lambda i: (i, 0)),
            pl.BlockSpec((1, N_MOD, d), lambda i: (i // per_b, 0, 0)),
            pl.BlockSpec((1, d), lambda i: (0, 0)),
            pl.BlockSpec((d, MAIN_COLS), lambda i: (0, 0)),
            pl.BlockSpec((FOX_WIDTH, d), lambda i: (0, 0)),
            pl.BlockSpec((d, LANES), lambda i: (0, 0)),
            pl.BlockSpec((d, 2 * LANES), lambda i: (0, 0)),
        ],
        out_specs=[
            pl.BlockSpec((tm, MAIN_COLS), lambda i: (i, 0)),
            pl.BlockSpec((FOX_HEADS * FOX_VROWS, tm), lambda i: (0, i)),
            pl.BlockSpec((tm, LANES), lambda i: (i, 0)),
        ],
        out_shape=[
            jax.ShapeDtypeStruct((t, MAIN_COLS), BF16),
            jax.ShapeDtypeStruct((FOX_HEADS * FOX_VROWS, t), BF16),
            jax.ShapeDtypeStruct((t, LANES), F32),
        ],
        compiler_params=_params("parallel"),
        name="inproj",
    )(x2, mod3, gain, w_main, w_vt, ws_hi, ws_hilo)


GLA_TM = 512
GLA_SUB = 2
GLA_BLK = 256


def _gla_kernel(q_ref, k_ref, v_ref, g_ref, small_ref, wgh_ref, wgl_ref, bgk_ref, gain_ref, bf_ref,
                sel_ref, ones_ref, o_ref, kx_ref, frow_ref, st_ref, fcar_ref, oacc_ref):
    @pl.when(pl.program_id(1) == 0)
    def _():
        st_ref[...] = jnp.zeros_like(st_ref)
        fcar_ref[...] = jnp.zeros_like(fcar_ref)

    subs = [pl.ds(s * GLA_TM, GLA_TM) for s in range(GLA_SUB)]
    fronts = [_gla_sub_front(q_ref.at[r], k_ref.at[r], v_ref.at[r], small_ref.at[r],
                             wgh_ref, wgl_ref, bgk_ref, oacc_ref.at[r]) for r in subs]
    for r, front in zip(subs, fronts):
        _gla_sub_chain(front, v_ref.at[r], g_ref.at[r], gain_ref, bf_ref, sel_ref, ones_ref,
                       o_ref.at[r], kx_ref.at[r], frow_ref.at[:, :, :, r], st_ref, fcar_ref,
                       oacc_ref.at[r])


def _gla_sub_front(q_ref, k_ref, v_ref, small_ref, wgh_ref, wgl_ref, bgk_ref, oacc_ref):
    tm = GLA_TM
    nc = tm // GLA_CHUNK
    small = small_ref[...]
    sh, sl = _split2(small)
    gk = _dot(sh, wgh_ref[...]) + (_dot(sh, wgl_ref[...]) + _dot(sl, wgh_ref[...])) + bgk_ref[...]
    log_a = _log_sigmoid(gk) * (1.0 / 16.0)

    blk = GLA_BLK
    blocks = [slice(j * blk, (j + 1) * blk) for j in range(tm // blk)]
    row = lax.broadcasted_iota(jnp.int32, (blk, blk), 0)
    col = lax.broadcasted_iota(jnp.int32, (blk, blk), 1)
    same = (row >> 6) == (col >> 6)
    lower = col <= row
    tri = jnp.logical_and(same, lower)
    tri_b = jnp.where(tri, 1.0, 0.0).astype(BF16)
    lower_b = jnp.where(lower, 1.0, 0.0).astype(BF16)

    lh, ll = _split2(log_a)
    b = jnp.concatenate([_dot(tri_b, lh[r]) + _dot(tri_b, ll[r]) for r in blocks], axis=0)
    btot = jnp.concatenate(
        [jnp.broadcast_to(b[(c + 1) * GLA_CHUNK - 1:(c + 1) * GLA_CHUNK, :], (GLA_CHUNK, GLA_QK))
         for c in range(nc)], axis=0)

    q = q_ref[...].astype(F32)
    k = k_ref[...].astype(F32)
    qt = q * jnp.exp(b) * (GLA_DK ** -0.5)
    kt = (k * jnp.exp(-b)).astype(BF16)
    kd = (k * jnp.exp(btot - b)).astype(BF16)

    lane_qk = lax.broadcasted_iota(jnp.int32, (1, GLA_QK), 1)
    head_masks = [jnp.logical_and(lane_qk >= h * GLA_DK, lane_qk < (h + 1) * GLA_DK)
                  for h in range(GLA_HEADS)]

    for h in range(GLA_HEADS):
        cs = slice(h * GLA_DV, (h + 1) * GLA_DV)
        qm = jnp.where(head_masks[h], qt, 0.0).astype(BF16)
        for r in blocks:
            att = jnp.where(tri, _dot_nt(qm[r], kt[r]), 0.0).astype(BF16)
            oacc_ref[r, cs] = _dot(att, v_ref[r, cs])
    return small, qt, kd, btot, head_masks, lower_b


def _gla_sub_chain(front, v_ref, g_ref, gain_ref, bf_ref, sel_ref, ones_ref, o_ref, kx_ref, frow_ref,
                   st_ref, fcar_ref, oacc_ref):
    tm = GLA_TM
    nc = tm // GLA_CHUNK
    small, qt, kd, btot, head_masks, lower_b = front

    chunks = [slice(c * GLA_CHUNK, (c + 1) * GLA_CHUNK) for c in range(nc)]
    kvts = [_dot_tn(v_ref[rs, :], kd[rs]) for rs in chunks]
    states = []
    st = st_ref[...]
    for c, rs in enumerate(chunks):
        states.append(st.astype(BF16))
        decay = jnp.exp(btot[c * GLA_CHUNK:c * GLA_CHUNK + 1, :])
        new = decay * st
        for h in range(GLA_HEADS):
            new = new + jnp.where(head_masks[h], kvts[c][h * GLA_DV:(h + 1) * GLA_DV, :], 0.0)
        st = new
    st_ref[...] = st
    for c, rs in enumerate(chunks):
        qc = qt[rs]
        qs = jnp.concatenate([jnp.where(head_masks[h], qc, 0.0) for h in range(GLA_HEADS)],
                             axis=0).astype(BF16)
        oi = _dot_nt(qs, states[c])
        for h in range(GLA_HEADS):
            oacc_ref[rs, h * GLA_DV:(h + 1) * GLA_DV] += oi[h * GLA_CHUNK:(h + 1) * GLA_CHUNK, :]

    for h in range(GLA_HEADS):
        cs = slice(h * GLA_DV, (h + 1) * GLA_DV)
        y = _rms(oacc_ref[:, cs]) * gain_ref[:, cs]
        g = g_ref[:, cs].astype(F32)
        o_ref[:, cs] = (y * (g * jax.nn.sigmoid(g))).astype(BF16)

    lane = lax.broadcasted_iota(jnp.int32, (1, LANES), 1)
    lf = jnp.where(lane < FOX_HEADS, _log_sigmoid(small + bf_ref[...]), 0.0)
    l1, l2, l3 = _split3(lf)
    carry = fcar_ref[...]
    parts = []
    for j in range(tm // GLA_BLK):
        r = slice(j * GLA_BLK, (j + 1) * GLA_BLK)
        part = _dot(lower_b, l1[r]) + (_dot(lower_b, l2[r]) + _dot(lower_b, l3[r])) + carry
        carry = part[GLA_BLK - 1:GLA_BLK, :]
        parts.append(part)
    fl = jnp.concatenate(parts, axis=0)
    fcar_ref[...] = carry
    fs = fl * LOG2E
    fst = fs.T
    for h in range(FOX_HEADS):
        frow_ref[0, h] = fst[h:h + 1, :]
    n1, n2, n3 = _split3(-fs)
    kx = _dot(n1, sel_ref[0]) + (_dot(n2, sel_ref[1]) + _dot(n3, sel_ref[2])) + ones_ref[...]
    kx_ref[...] = kx.astype(BF16)


def _fox_bias_selectors():
    import numpy as np
    sel = np.zeros((3, LANES, FOX_WIDTH), np.float32)
    ones = np.zeros((1, FOX_WIDTH), np.float32)
    for h in range(FOX_HEADS):
        base = (h // 2) * LANES + (h % 2) * FOX_BIAS_LANES
        for i in range(3):
            sel[i, h, base + i] = 1.0
            ones[0, base + 3 + i] = 1.0
    return jnp.asarray(sel, BF16), jnp.asarray(ones, F32)


def _gla_call(proj, small, wg_hi, wg_lo, b_gk, gain, bf_pad, batch, seq):
    t = proj.shape[0]
    tm = GLA_TM * GLA_SUB
    nb = seq // tm

    def rows(b, i):
        return b * nb + i

    sel, ones = _fox_bias_selectors()
    return pl.pallas_call(
        _gla_kernel,
        grid=(batch, nb),
        in_specs=[
            pl.BlockSpec((tm, GLA_QK), lambda b, i: (rows(b, i), COL_QG // GLA_QK)),
            pl.BlockSpec((tm, GLA_QK), lambda b, i: (rows(b, i), COL_KG // GLA_QK)),
            pl.BlockSpec((tm, GLA_WIDTH), lambda b, i: (rows(b, i), COL_VG // GLA_WIDTH)),
            pl.BlockSpec((tm, GLA_WIDTH), lambda b, i: (rows(b, i), COL_GG // GLA_WIDTH)),
            pl.BlockSpec((tm, LANES), lambda b, i: (rows(b, i), 0)),
            pl.BlockSpec((LANES, GLA_QK), lambda b, i: (0, 0)),
            pl.BlockSpec((LANES, GLA_QK), lambda b, i: (0, 0)),
            pl.BlockSpec((1, GLA_QK), lambda b, i: (0, 0)),
            pl.BlockSpec((1, GLA_WIDTH), lambda b, i: (0, 0)),
            pl.BlockSpec((1, LANES), lambda b, i: (0, 0)),
            pl.BlockSpec((3, LANES, FOX_WIDTH), lambda b, i: (0, 0, 0)),
            pl.BlockSpec((1, FOX_WIDTH), lambda b, i: (0, 0)),
        ],
        out_specs=[
            pl.BlockSpec((tm, GLA_WIDTH), lambda b, i: (rows(b, i), 0)),
            pl.BlockSpec((tm, FOX_WIDTH), lambda b, i: (rows(b, i), 0)),
            pl.BlockSpec((1, FOX_HEADS, 1, tm), lambda b, i: (b, 0, 0, i)),
        ],
        out_shape=[
            jax.ShapeDtypeStruct((t, GLA_WIDTH), BF16),
            jax.ShapeDtypeStruct((t, FOX_WIDTH), BF16),
            jax.ShapeDtypeStruct((batch, FOX_HEADS, 1, seq), F32),
        ],
        scratch_shapes=[
            pltpu.VMEM((GLA_DV, GLA_QK), F32),
            pltpu.VMEM((1, LANES), F32),
            pltpu.VMEM((tm, GLA_WIDTH), F32),
        ],
        compiler_params=_params("parallel", "arbitrary"),
        name="gla",
    )(proj, proj, proj, proj, small, wg_hi, wg_lo, b_gk, gain, bf_pad, sel, ones)


FOX_TQ = 512
FOX_TK = 512
FOX_PAIRS = 2
FOX_SPLIT = 1


def _fox_kernel(q_ref, k_ref, kx_ref, vt_ref, frow_ref, o_ref, m_ref, acc_ref, sa_ref, sb_ref,
                ma_ref, mb_ref):
    tq, tk = FOX_TQ, FOX_TK
    i = pl.program_id(2)
    q0 = pl.multiple_of(i * tq, tq)
    lane = lax.broadcasted_iota(jnp.int32, (1, LANES), 1)
    heads = range(2 * FOX_PAIRS)

    def pair_lanes(hh):
        return slice((hh // 2) * LANES, (hh // 2 + 1) * LANES)

    q_aug = []
    for hh in heads:
        head_lanes = (lane < FOX_DH) if hh % 2 == 0 else (lane >= FOX_DH)
        qm = jnp.where(head_lanes, q_ref[:, pair_lanes(hh)], 0.0).astype(BF16)
        fr = jnp.broadcast_to(frow_ref[0, hh, :, pl.ds(q0, LANES)][:, 0:1], (1, LANES))
        f1, f2, f3 = _split3(fr)
        base = (hh % 2) * FOX_BIAS_LANES
        qx = jnp.where(jnp.logical_and(lane >= base, lane < base + 3), 1.0, 0.0).astype(BF16)
        qx = jnp.where(lane == base + 3, f1, qx)
        qx = jnp.where(lane == base + 4, f2, qx)
        qx = jnp.where(lane == base + 5, f3, qx)
        q_aug.append(jnp.concatenate([qm, jnp.broadcast_to(qx, (tq, LANES))], axis=1))

    m_ref[...] = jnp.full(m_ref.shape, NEG, F32)
    acc_ref[...] = jnp.zeros_like(acc_ref)

    bufs = {"a": (sa_ref, ma_ref), "b": (sb_ref, mb_ref)}

    th = tk // FOX_SPLIT

    def produce(which, blk, hh, diagonal, part):
        buf, mx = bufs[which]
        k0 = pl.multiple_of(blk * tk + part * th, th)
        k_aug = jnp.concatenate([k_ref[pl.ds(k0, th), pair_lanes(hh)],
                                 kx_ref[pl.ds(k0, th), pair_lanes(hh)]], axis=1)
        st = _dot_nt(k_aug, q_aug[hh])
        if diagonal:
            key = lax.broadcasted_iota(jnp.int32, (th, tq), 0) + part * th
            qry = lax.broadcasted_iota(jnp.int32, (th, tq), 1)
            st = jnp.where(key <= qry, st, NEG)
        buf[hh, part * th:(part + 1) * th, :] = st
        part_max = jnp.max(st, axis=0, keepdims=True)
        mx[hh] = part_max if part == 0 else jnp.maximum(mx[hh], part_max)

    def consume(which, blk, hh, part):
        buf, mx = bufs[which]
        k0 = pl.multiple_of(blk * tk + part * th, th)
        m_old = m_ref[hh]
        m_new = jnp.maximum(m_old, mx[hh])
        p = jnp.exp2(buf[hh, part * th:(part + 1) * th, :] - m_new).astype(BF16)
        vt = vt_ref[hh * FOX_VROWS:(hh + 1) * FOX_VROWS, pl.ds(k0, th)]
        pv = _dot(vt, p)
        if part == 0:
            acc_ref[hh] = jnp.exp2(m_old - m_new) * acc_ref[hh] + pv
        else:
            acc_ref[hh] += pv
        if part == FOX_SPLIT - 1:
            m_ref[hh] = m_new

    def stage(cur, cur_blk, nxt=None, nxt_blk=None, nxt_diagonal=False):
        for hh in heads:
            for part in range(FOX_SPLIT):
                if nxt is not None:
                    produce(nxt, nxt_blk, hh, nxt_diagonal, part)
                if hh > 0:
                    consume(cur, cur_blk, hh - 1, part)
        for part in range(FOX_SPLIT):
            consume(cur, cur_blk, heads[-1], part)

    @pl.when(i == 0)
    def _():
        for hh in heads:
            for part in range(FOX_SPLIT):
                produce("a", 0, hh, True, part)
        stage("a", 0)

    @pl.when(i > 0)
    def _():
        for hh in heads:
            for part in range(FOX_SPLIT):
                produce("a", 0, hh, False, part)

    def pair(t, carry):
        b0 = 2 * t
        stage("a", b0, "b", b0 + 1)
        stage("b", b0 + 1, "a", b0 + 2)
        return carry

    n_pairs = lax.shift_right_logical(jnp.maximum(i - 1, 0), 1)
    lax.fori_loop(0, n_pairs, pair, 0)

    @pl.when((i & 1) == 1)
    def _():
        stage("a", i - 1, "b", i, True)
        stage("b", i)

    @pl.when(jnp.logical_and((i & 1) == 0, i > 0))
    def _():
        stage("a", i - 2, "b", i - 1)
        stage("b", i - 1, "a", i, True)
        stage("a", i)

    for pr in range(FOX_PAIRS):
        outs = []
        for hh in (2 * pr, 2 * pr + 1):
            acc = acc_ref[hh]
            outs.append(acc[:FOX_DH] * (1.0 / acc[FOX_DH:FOX_DH + 1]))
        o_ref[:, pr * LANES:(pr + 1) * LANES] = jnp.concatenate(outs, axis=0).T.astype(BF16)


def _fox_call(proj, kx, vt, frow, batch, seq):
    t = proj.shape[0]
    tq = FOX_TQ
    nq = seq // tq
    width = FOX_PAIRS * LANES
    groups = FOX_WIDTH // width
    nh = 2 * FOX_PAIRS
    return pl.pallas_call(
        _fox_kernel,
        grid=(batch, groups, nq),
        in_specs=[
            pl.BlockSpec((tq, width), lambda b, hp, i: (b * nq + i, COL_QF // width + hp)),
            pl.BlockSpec((seq, width), lambda b, hp, i: (b, COL_KF // width + hp)),
            pl.BlockSpec((seq, width), lambda b, hp, i: (b, hp)),
            pl.BlockSpec((nh * FOX_VROWS, seq), lambda b, hp, i: (hp, b)),
            pl.BlockSpec((1, nh, 1, seq), lambda b, hp, i: (b, hp, 0, 0)),
        ],
        out_specs=pl.BlockSpec((tq, width), lambda b, hp, i: (b * nq + i, hp)),
        out_shape=jax.ShapeDtypeStruct((t, FOX_WIDTH), BF16),
        scratch_shapes=[
            pltpu.VMEM((nh, 1, tq), F32),
            pltpu.VMEM((nh, FOX_VROWS, tq), F32),
            pltpu.VMEM((nh, FOX_TK, tq), F32),
            pltpu.VMEM((nh, FOX_TK, tq), F32),
            pltpu.VMEM((nh, 1, tq), F32),
            pltpu.VMEM((nh, 1, tq), F32),
        ],
        compiler_params=_params("parallel", "parallel", "arbitrary"),
        name="fox",
    )(proj, proj, kx, vt, frow)


POST_TM = 512
POST_SUB = 2


def _post_kernel(og_ref, of_ref, x_ref, mod_ref, pg_ref, mg_ref, wo_ref, wrh_ref, wrl_ref, br_ref,
                 x1_ref, h2_ref, idx_ref, rank_ref, gate_ref, cnt_ref, car_ref):
    @pl.when(pl.program_id(0) == 0)
    def _():
        car_ref[...] = jnp.zeros_like(car_ref)

    ys = []
    for s in range(POST_SUB):
        rows = slice(s * POST_TM, (s + 1) * POST_TM)
        ys.append(_dot(og_ref[rows, :], wo_ref[:GLA_WIDTH, :])
                  + _dot(of_ref[rows, :], wo_ref[GLA_WIDTH:, :]))
    for s in range(POST_SUB):
        _post_route(s, ys[s], x_ref, mod_ref, pg_ref, mg_ref, wrh_ref, wrl_ref, br_ref,
                    x1_ref, h2_ref, idx_ref, rank_ref, gate_ref, cnt_ref, car_ref)


def _post_route(s, y, x_ref, mod_ref, pg_ref, mg_ref, wrh_ref, wrl_ref, br_ref,
                x1_ref, h2_ref, idx_ref, rank_ref, gate_ref, cnt_ref, car_ref):
    tm = POST_TM
    rows = slice(s * tm, (s + 1) * tm)
    x1 = x_ref[rows, :] + mod_ref[0, 2:3, :] * (_rms(y) * pg_ref[...])
    x1_ref[rows, :] = x1
    h2 = _rms(x1) * mg_ref[...]
    h2 = h2 * (1.0 + mod_ref[0, 4:5, :]) + mod_ref[0, 3:4, :]
    for j in range(ROW_TILES):
        h2_ref[pl.ds(s * tm * ROW_TILES + j, tm, stride=ROW_TILES), :] = h2[:, j * LANES:(j + 1) * LANES]

    hh, hl = _split2(h2)
    logits = _dot(hh, wrh_ref[...]) + (_dot(hh, wrl_ref[...]) + _dot(hl, wrh_ref[...])) + br_ref[...]
    lane = lax.broadcasted_iota(jnp.int32, (tm, LANES), 1).astype(F32)
    work = jnp.where(lane < N_EXPERTS, logits, -jnp.inf)
    vals, idxs = [], []
    for _ in range(TOP_K):
        mx = jnp.max(work, axis=-1, keepdims=True)
        ix = jnp.min(jnp.where(work == mx, lane, float(LANES)), axis=-1, keepdims=True)
        vals.append(mx)
        idxs.append(ix)
        work = jnp.where(lane == ix, -jnp.inf, work)
    es = [jnp.exp(v - vals[0]) for v in vals]
    den = es[0] + es[1] + es[2] + es[3]

    onehot = jnp.zeros((tm, LANES), F32)
    for ix in idxs:
        onehot = onehot + jnp.where(lane == ix, 1.0, 0.0)
    row = lax.broadcasted_iota(jnp.int32, (tm, tm), 0)
    col = lax.broadcasted_iota(jnp.int32, (tm, tm), 1)
    strict_b = jnp.where(col < row, 1.0, 0.0).astype(BF16)
    prefix = _dot(strict_b, onehot.astype(BF16)) + car_ref[...]
    car_new = car_ref[...] + jnp.sum(onehot, axis=0, keepdims=True)
    car_ref[...] = car_new
    cnt_ref[...] = car_new

    idx_o = jnp.zeros((tm, LANES), F32)
    rank_o = jnp.zeros((tm, LANES), F32)
    gate_o = jnp.zeros((tm, LANES), F32)
    for kk in range(TOP_K):
        rk = jnp.sum(jnp.where(lane == idxs[kk], prefix, 0.0), axis=-1, keepdims=True)
        sel = lane == float(kk)
        idx_o = jnp.where(sel, idxs[kk], idx_o)
        rank_o = jnp.where(sel, rk, rank_o)
        gate_o = jnp.where(sel, es[kk] / den, gate_o)
    idx_ref[rows, :] = idx_o.astype(jnp.int32)
    rank_ref[rows, :] = rank_o.astype(jnp.int32)
    gate_ref[rows, :] = gate_o


def _post_call(o_gla, o_fox, x2, mod3, post_gain, mlp_gain, w_o, wr_hi, wr_lo, br_pad, seq):
    t, d = x2.shape
    tm = POST_TM * POST_SUB
    per_b = seq // tm
    return pl.pallas_call(
        _post_kernel,
        grid=(t // tm,),
        in_specs=[
            pl.BlockSpec((tm, GLA_WIDTH), lambda i: (i, 0)),
            pl.BlockSpec((tm, FOX_WIDTH), lambda i: (i, 0)),
            pl.BlockSpec((tm, d), lambda i: (i, 0)),
            pl.BlockSpec((1, N_MOD, d), lambda i: (i // per_b, 0, 0)),
            pl.BlockSpec((1, d), lambda i: (0, 0)),
            pl.BlockSpec((1, d), lambda i: (0, 0)),
            pl.BlockSpec((GLA_WIDTH + FOX_WIDTH, d), lambda i: (0, 0)),
            pl.BlockSpec((d, LANES), lambda i: (0, 0)),
            pl.BlockSpec((d, LANES), lambda i: (0, 0)),
            pl.BlockSpec((1, LANES), lambda i: (0, 0)),
        ],
        out_specs=[
            pl.BlockSpec((tm, d), lambda i: (i, 0)),
            pl.BlockSpec((tm * ROW_TILES, LANES), lambda i: (i, 0)),
            pl.BlockSpec((tm, LANES), lambda i: (i, 0)),
            pl.BlockSpec((tm, LANES), lambda i: (i, 0)),
            pl.BlockSpec((tm, LANES), lambda i: (i, 0)),
            pl.BlockSpec((1, LANES), lambda i: (0, 0)),
        ],
        out_shape=[
            jax.ShapeDtypeStruct((t, d), F32),
            jax.ShapeDtypeStruct((t * ROW_TILES, LANES), F32),
            jax.ShapeDtypeStruct((t, LANES), jnp.int32),
            jax.ShapeDtypeStruct((t, LANES), jnp.int32),
            jax.ShapeDtypeStruct((t, LANES), F32),
            jax.ShapeDtypeStruct((1, LANES), F32),
        ],
        scratch_shapes=[pltpu.VMEM((1, LANES), F32)],
        compiler_params=_params("arbitrary"),
        name="post",
    )(o_gla, o_fox, x2, mod3, post_gain, mlp_gain, w_o, wr_hi, wr_lo, br_pad)


DISP_TM = 512
DISP_SLOTS = 3
ISSUE_UNROLL = 8


def _row_copy(src_ref, src_row, dst_ref, dst_row, sem):
    return pltpu.make_async_copy(
        src_ref.at[pl.ds(pl.multiple_of(src_row * ROW_TILES, ROW_TILES), ROW_TILES)],
        dst_ref.at[pl.ds(pl.multiple_of(dst_row * ROW_TILES, ROW_TILES), ROW_TILES)],
        sem)


def _dispatch_kernel(pend_ref, pos_ref, h_hbm, xs_ref, zero_ref, hbuf_ref, sem, in_sem, out_sem):
    tm = DISP_TM
    rows = tm * ROW_TILES
    zrows = MOE_TM * ROW_TILES
    i = pl.program_id(0)
    last = pl.num_programs(0) - 1
    slot = lax.rem(i, DISP_SLOTS)
    nxt = lax.rem(i + 1, DISP_SLOTS)

    def load(step, s):
        return pltpu.make_async_copy(
            h_hbm.at[pl.ds(pl.multiple_of(step * rows, rows), rows)], hbuf_ref.at[s], in_sem.at[s])

    def drain(s):
        for _ in range(TOP_K):
            pltpu.make_async_copy(hbuf_ref.at[s], xs_ref.at[pl.ds(0, rows)], out_sem.at[s]).wait()

    @pl.when(i == 0)
    def _():
        load(0, 0).start()
        zero_ref[...] = jnp.zeros_like(zero_ref)

        def last_tile(e):
            start = pl.multiple_of((pend_ref[e] - MOE_TM) * ROW_TILES, zrows)
            return pltpu.make_async_copy(zero_ref, xs_ref.at[pl.ds(start, zrows)], sem)

        def nonempty(e):
            return pend_ref[e] > jnp.where(e == 0, 0, pend_ref[jnp.maximum(e - 1, 0)])

        def clear(e, carry):
            @pl.when(nonempty(e))
            def _():
                last_tile(e).start()
            return carry

        def clear_wait(e, carry):
            @pl.when(nonempty(e))
            def _():
                last_tile(e).wait()
            return carry

        lax.fori_loop(0, N_EXPERTS, clear, 0)
        lax.fori_loop(0, N_EXPERTS, clear_wait, 0)

        def tail_tile(j):
            return pltpu.make_async_copy(
                zero_ref, xs_ref.at[pl.ds(pl.multiple_of(j * zrows, zrows), zrows)], sem)

        def tail(j, carry):
            tail_tile(j).start()
            return carry

        def tail_wait(j, carry):
            tail_tile(j).wait()
            return carry

        first_unused = lax.div(pend_ref[N_EXPERTS - 1], MOE_TM)
        n_tiles = xs_ref.shape[0] // zrows
        lax.fori_loop(first_unused, n_tiles, tail, 0)
        lax.fori_loop(first_unused, n_tiles, tail_wait, 0)

    @pl.when(i >= DISP_SLOTS - 1)
    def _():
        drain(nxt)

    @pl.when(i < last)
    def _():
        load(i + 1, nxt).start()

    load(i, slot).wait()

    def issue(g, carry):
        for u in range(ISSUE_UNROLL):
            t = g * ISSUE_UNROLL + u
            for kk in range(TOP_K):
                _row_copy(hbuf_ref.at[slot], t, xs_ref, pos_ref[0, 0, t * TOP_K + kk],
                          out_sem.at[slot]).start(priority=kk % 2)
        return carry

    lax.fori_loop(0, tm // ISSUE_UNROLL, issue, 0)

    @pl.when(i == last)
    def _():
        for back in range(DISP_SLOTS - 1):
            @pl.when(i >= back)
            def _():
                drain(lax.rem(i - back + DISP_SLOTS, DISP_SLOTS))


def _dispatch_call(pend, pos3, h2, n_tiles):
    tm = DISP_TM
    n_steps = pos3.shape[0]
    grid_spec = pltpu.PrefetchScalarGridSpec(
        num_scalar_prefetch=1,
        grid=(n_steps,),
        in_specs=[
            pl.BlockSpec((1, 1, tm * TOP_K), lambda i, pend: (i, 0, 0), memory_space=pltpu.SMEM),
            pl.BlockSpec(memory_space=pl.ANY),
        ],
        out_specs=pl.BlockSpec(memory_space=pl.ANY),
        scratch_shapes=[
            pltpu.VMEM((MOE_TM * ROW_TILES, LANES), F32),
            pltpu.VMEM((DISP_SLOTS, tm * ROW_TILES, LANES), F32),
            pltpu.SemaphoreType.DMA,
            pltpu.SemaphoreType.DMA((DISP_SLOTS,)),
            pltpu.SemaphoreType.DMA((DISP_SLOTS,)),
        ],
    )
    return pl.pallas_call(
        _dispatch_kernel,
        grid_spec=grid_spec,
        out_shape=jax.ShapeDtypeStruct((n_tiles * MOE_TM * ROW_TILES, LANES), F32),
        compiler_params=_params("arbitrary"),
        name="dispatch",
    )(pend, pos3, h2)


MOE_TM = 512


def _experts_kernel(te_ref, nu_ref, x_ref, win_ref, bin_ref, wout_ref, bout_ref, y_ref,
                    wib_ref, wob_ref):
    tm = MOE_TM
    i = pl.program_id(0)
    used = i < nu_ref[0]

    @pl.when(jnp.logical_not(used))
    def _():
        y_ref[...] = jnp.zeros_like(y_ref)

    fresh = jnp.logical_or(i == 0, te_ref[i] != te_ref[jnp.maximum(i - 1, 0)])

    @pl.when(jnp.logical_and(used, fresh))
    def _():
        wib_ref[...] = win_ref[0].astype(BF16)
        wob_ref[...] = wout_ref[0].astype(BF16)

    @pl.when(used)
    def _():
        x = jnp.concatenate([x_ref[pl.ds(j, tm, stride=ROW_TILES), :] for j in range(ROW_TILES)],
                            axis=1).astype(BF16)
        u = _dot(x, wib_ref[...]) + bin_ref[0]
        glu = jnp.minimum(u[:, :D_FF], SWIGLU_LIMIT)
        lin = jnp.clip(u[:, D_FF:], -SWIGLU_LIMIT, SWIGLU_LIMIT)
        a = glu * jax.nn.sigmoid(SWIGLU_ALPHA * glu) * (lin + 1.0)
        y = _dot(a.astype(BF16), wob_ref[...]) + bout_ref[0]
        for j in range(ROW_TILES):
            y_ref[pl.ds(j, tm, stride=ROW_TILES), :] = y[:, j * LANES:(j + 1) * LANES]


def _experts_call(tile_expert, n_used, xs, w_in, b_in, w_out, b_out):
    tm = MOE_TM
    n_tiles = tile_expert.shape[0]
    d = D_MODEL

    def tile(i, te, nu):
        return (jnp.minimum(i, nu[0] - 1), 0)

    grid_spec = pltpu.PrefetchScalarGridSpec(
        num_scalar_prefetch=2,
        grid=(n_tiles,),
        in_specs=[
            pl.BlockSpec((tm * ROW_TILES, LANES), tile),
            pl.BlockSpec((1, d, 2 * D_FF), lambda i, te, nu: (te[i], 0, 0)),
            pl.BlockSpec((1, 1, 2 * D_FF), lambda i, te, nu: (te[i], 0, 0)),
            pl.BlockSpec((1, D_FF, d), lambda i, te, nu: (te[i], 0, 0)),
            pl.BlockSpec((1, 1, d), lambda i, te, nu: (te[i], 0, 0)),
        ],
        out_specs=pl.BlockSpec((tm * ROW_TILES, LANES), lambda i, te, nu: (i, 0)),
        scratch_shapes=[
            pltpu.VMEM((d, 2 * D_FF), BF16),
            pltpu.VMEM((D_FF, d), BF16),
        ],
    )
    return pl.pallas_call(
        _experts_kernel,
        grid_spec=grid_spec,
        out_shape=jax.ShapeDtypeStruct(xs.shape, F32),
        compiler_params=_params("arbitrary"),
        name="experts",
    )(tile_expert, n_used, xs, w_in, b_in, w_out, b_out)


COMB_TM = 512


def _combine_kernel(pos_ref, posn_ref, y_ref, gate_ref, x1_ref, mod_ref, pg_ref, o_ref, buf_ref, sem):
    tm = COMB_TM
    i = pl.program_id(0)
    slot = i & 1

    def gather(p_ref, s):
        def issue(g, carry):
            for u in range(ISSUE_UNROLL):
                t = g * ISSUE_UNROLL + u
                for kk in range(TOP_K):
                    _row_copy(y_ref, p_ref[0, 0, t * TOP_K + kk], buf_ref.at[s, kk], t,
                              sem.at[s]).start(priority=kk % 2)
            return carry

        lax.fori_loop(0, tm // ISSUE_UNROLL, issue, 0)

    @pl.when(i == 0)
    def _():
        gather(pos_ref, 0)

    @pl.when(i + 1 < pl.num_programs(0))
    def _():
        gather(posn_ref, 1 - slot)

    for kk in range(TOP_K):
        pltpu.make_async_copy(y_ref.at[pl.ds(0, tm * ROW_TILES)], buf_ref.at[slot, kk],
                              sem.at[slot]).wait()

    gates = gate_ref[...]
    acc = None
    for kk in range(TOP_K):
        rows = jnp.concatenate(
            [buf_ref[slot, kk, pl.ds(j, tm, stride=ROW_TILES), :] for j in range(ROW_TILES)], axis=1)
        term = rows * gates[:, kk:kk + 1]
        acc = term if acc is None else acc + term
    o_ref[...] = x1_ref[...] + mod_ref[0, 5:6, :] * (_rms(acc) * pg_ref[...])


def _combine_call(pos3, y, gates, x1, mod3, post_gain, seq):
    t, d = x1.shape
    tm = COMB_TM
    per_b = seq // tm
    n_steps = t // tm
    return pl.pallas_call(
        _combine_kernel,
        grid=(n_steps,),
        in_specs=[
            pl.BlockSpec((1, 1, tm * TOP_K), lambda i: (i, 0, 0), memory_space=pltpu.SMEM),
            pl.BlockSpec((1, 1, tm * TOP_K), lambda i: (jnp.minimum(i + 1, n_steps - 1), 0, 0),
                         memory_space=pltpu.SMEM),
            pl.BlockSpec(memory_space=pl.ANY),
            pl.BlockSpec((tm, LANES), lambda i: (i, 0)),
            pl.BlockSpec((tm, d), lambda i: (i, 0)),
            pl.BlockSpec((1, N_MOD, d), lambda i: (i // per_b, 0, 0)),
            pl.BlockSpec((1, d), lambda i: (0, 0)),
        ],
        out_specs=pl.BlockSpec((tm, d), lambda i: (i, 0)),
        out_shape=jax.ShapeDtypeStruct((t, d), F32),
        scratch_shapes=[
            pltpu.VMEM((2, TOP_K, tm * ROW_TILES, LANES), F32),
            pltpu.SemaphoreType.DMA((2,)),
        ],
        compiler_params=_params("arbitrary"),
        name="combine",
    )(pos3, pos3, y, gates, x1, mod3, post_gain)


def _pad_cols(w, n):
    return jnp.pad(w, ((0, 0), (0, n - w.shape[1])))


def kernel(x, c, w_ada, b_ada, attn_pre_gain, attn_post_gain, w_in, w_gk2, b_gk, gla_norm_gain, b_f,
           w_o, mlp_pre_gain, mlp_post_gain, w_router, b_router, w_e_in, b_e_in, w_e_out, b_e_out):
    batch, seq, d = x.shape
    t = batch * seq
    x2 = x.reshape(t, d)

    o = 0
    cols = {}
    for name, width in (("qg", GLA_QK), ("kg", GLA_QK), ("vg", GLA_WIDTH), ("gg", GLA_WIDTH),
                        ("zg", GLA_LOWRANK), ("qf", FOX_WIDTH), ("kf", FOX_WIDTH), ("vf", FOX_WIDTH),
                        ("ff", FOX_HEADS)):
        cols[name] = w_in[:, o:o + width]
        o += width
    gla_cols = 2 * GLA_QK + 2 * GLA_WIDTH
    fox_start = gla_cols + GLA_LOWRANK
    w_main = jnp.concatenate([w_in[:, :gla_cols], w_in[:, fox_start:fox_start + 2 * FOX_WIDTH]],
                             axis=1).astype(BF16)
    w_vt = cols["vf"].T.astype(BF16)
    w_small = _pad_cols(jnp.concatenate([cols["ff"], cols["zg"]], axis=1), LANES)
    ws_hi, ws_lo = _split2(w_small)
    wg = jnp.zeros((LANES, GLA_QK), F32).at[SMALL_ZG:SMALL_ZG + GLA_LOWRANK].set(w_gk2)
    wg_hi, wg_lo = _split2(wg)
    bf_pad = jnp.zeros((1, LANES), F32).at[0, SMALL_FF:SMALL_FF + FOX_HEADS].set(b_f)
    wr_hi, wr_lo = _split2(_pad_cols(w_router, LANES))
    br_pad = _pad_cols(b_router.reshape(1, N_EXPERTS), LANES)

    c_pad = jnp.pad(c, ((0, SUBLANES - batch % SUBLANES if batch % SUBLANES else 0), (0, 0)))
    mod = _mod_call(c_pad, w_ada, b_ada.reshape(1, -1))
    mod3 = mod[:batch].reshape(batch, N_MOD, d)

    proj, vt, small = _inproj_call(x2, mod3, attn_pre_gain.reshape(1, d), w_main, w_vt, ws_hi, ws_lo,
                                   seq)
    o_gla, kx, frow = _gla_call(proj, small, wg_hi, wg_lo, b_gk.reshape(1, -1),
                                gla_norm_gain.reshape(1, -1), bf_pad, batch, seq)
    o_fox = _fox_call(proj, kx, vt, frow, batch, seq)

    x1, h2, idx_w, rank_w, gate_w, cnt = _post_call(
        o_gla, o_fox, x2, mod3, attn_post_gain.reshape(1, d), mlp_pre_gain.reshape(1, d),
        w_o.astype(BF16), wr_hi, wr_lo, br_pad, seq)

    tm = MOE_TM
    n_tiles = (t * TOP_K) // tm + N_EXPERTS
    counts = cnt[0, :N_EXPERTS].astype(jnp.int32)
    padded = ((counts + tm - 1) // tm) * tm
    pend = jnp.cumsum(padded)
    pstart = pend - padded
    experts = jnp.arange(N_EXPERTS, dtype=jnp.int32)
    first_slot = jnp.sum(jnp.where(idx_w[:, :TOP_K, None] == experts, pstart.astype(jnp.int32), 0),
                         axis=-1)
    pos = first_slot + rank_w[:, :TOP_K]
    n_used = (pend[-1] // tm).astype(jnp.int32).reshape(1)
    tile_start = jnp.arange(n_tiles, dtype=jnp.int32) * tm
    tile_expert = jnp.minimum(
        jnp.sum((pend[None, :] <= tile_start[:, None]).astype(jnp.int32), axis=1), N_EXPERTS - 1)

    pos_d = pos.reshape(t // DISP_TM, 1, DISP_TM * TOP_K)
    xs = _dispatch_call(pend.astype(jnp.int32), pos_d, h2, n_tiles)
    y = _experts_call(tile_expert, n_used, xs, w_e_in, b_e_in.reshape(N_EXPERTS, 1, -1), w_e_out,
                      b_e_out.reshape(N_EXPERTS, 1, -1))
    pos_c = pos.reshape(t // COMB_TM, 1, COMB_TM * TOP_K)
    out = _combine_call(pos_c, y, gate_w, x1, mod3, mlp_post_gain.reshape(1, d), seq)
    return out.reshape(batch, seq, d)
```

```python
import jax
import jax.numpy as jnp
from jax import lax
from jax.experimental import pallas as pl
from jax.experimental.pallas import tpu as pltpu

F32 = jnp.float32
BF16 = jnp.bfloat16

D_MODEL = 1024
N_MOD = 6
GLA_HEADS = 4
GLA_DK = 64
GLA_DV = 128
GLA_QK = GLA_HEADS * GLA_DK
GLA_WIDTH = GLA_HEADS * GLA_DV
GLA_LOWRANK = 16
GLA_CHUNK = 64
GLA_GATE_NORM = 16.0
FOX_HEADS = 8
FOX_DH = 64
FOX_WIDTH = FOX_HEADS * FOX_DH
N_EXPERTS = 32
TOP_K = 4
D_FF = D_MODEL
SWIGLU_ALPHA = 1.702
SWIGLU_LIMIT = 7.0
EPS = 1e-6

LANES = 128
SUBLANES = 8
ROW_TILES = D_MODEL // LANES

COL_QG, COL_KG, COL_VG, COL_GG = 0, 256, 512, 1024
COL_QF, COL_KF = 1536, 2048
MAIN_COLS = 2560
LOG2E = 1.4426950408889634
FOX_Q_SCALE = (FOX_DH ** -0.5) * LOG2E
FOX_BIAS_LANES = 6
BF16_SUBLANES = 16
FOX_VROWS = FOX_DH + BF16_SUBLANES
SMALL_FF, SMALL_ZG = 0, 8

NEG = -1e30
VMEM_LIMIT = 56 * 1024 * 1024


def _dot(a, b):
    return jnp.dot(a, b, preferred_element_type=F32)


def _dot_nt(a, b):
    return lax.dot_general(a, b, (((1,), (1,)), ((), ())), preferred_element_type=F32)


def _dot_tn(a, b):
    return lax.dot_general(a, b, (((0,), (0,)), ((), ())), preferred_element_type=F32)


def _split2(a):
    hi = a.astype(BF16)
    lo = (a - hi.astype(F32)).astype(BF16)
    return hi, lo


def _split3(a):
    hi = a.astype(BF16)
    r = a - hi.astype(F32)
    mid = r.astype(BF16)
    lo = (r - mid.astype(F32)).astype(BF16)
    return hi, mid, lo


def _log_sigmoid(x):
    return jnp.minimum(x, 0.0) - jnp.log1p(jnp.exp(-jnp.abs(x)))


def _rms(x):
    return x * lax.rsqrt(jnp.mean(x * x, axis=-1, keepdims=True) + EPS)


def _params(*sem, flags=None):
    return pltpu.CompilerParams(dimension_semantics=sem, vmem_limit_bytes=VMEM_LIMIT, flags=flags)


def _mod_kernel(c_ref, w_ref, b_ref, o_ref):
    c = c_ref[...]
    s = c * jax.nn.sigmoid(c)
    sh, sl = _split2(s)
    wh, wl = _split2(w_ref[...])
    o_ref[...] = _dot(sh, wh) + (_dot(sh, wl) + _dot(sl, wh)) + b_ref[...]


def _mod_call(c_pad, w_ada, b_ada):
    rows, d = c_pad.shape
    n = w_ada.shape[1]
    tn = D_MODEL
    return pl.pallas_call(
        _mod_kernel,
        grid=(n // tn,),
        in_specs=[
            pl.BlockSpec((rows, d), lambda j: (0, 0)),
            pl.BlockSpec((d, tn), lambda j: (0, j)),
            pl.BlockSpec((1, tn), lambda j: (0, j)),
        ],
        out_specs=pl.BlockSpec((rows, tn), lambda j: (0, j)),
        out_shape=jax.ShapeDtypeStruct((rows, n), F32),
        compiler_params=_params("parallel"),
        name="mod",
    )(c_pad, w_ada, b_ada)


INPROJ_TM = 512
INPROJ_SUB = 2


def _inproj_kernel(x_ref, mod_ref, g_ref, w_ref, wvt_ref, wsh_ref, wsl_ref, proj_ref, vt_ref, small_ref):
    subs = [pl.ds(s * INPROJ_TM, INPROJ_TM) for s in range(INPROJ_SUB)]
    hs = []
    for r in subs:
        h = _rms(x_ref[r, :]) * g_ref[...]
        h = h * (1.0 + mod_ref[0, 1:2, :]) + mod_ref[0, 0:1, :]
        hs.append(_split2(h))
    for r, (hh, hl) in zip(subs, hs):
        _inproj_sub(hh, hl, w_ref, wvt_ref, wsh_ref, wsl_ref, proj_ref.at[r], vt_ref.at[:, r],
                    small_ref.at[r])


def _inproj_sub(hh, hl, w_ref, wvt_ref, wsh_ref, wsl_ref, proj_ref, vt_ref, small_ref):
    cw = FOX_WIDTH
    for j in range(MAIN_COLS // cw):
        r = _dot(hh, w_ref[:, j * cw:(j + 1) * cw])
        if j * cw == COL_QF:
            r = r * FOX_Q_SCALE
        proj_ref[:, j * cw:(j + 1) * cw] = r.astype(BF16)
    vt = _dot_nt(wvt_ref[...], hh).astype(BF16)
    ones = jnp.ones((FOX_VROWS - FOX_DH, vt.shape[1]), BF16)
    for h in range(FOX_HEADS):
        vt_ref[h * FOX_VROWS:h * FOX_VROWS + FOX_DH, :] = vt[h * FOX_DH:(h + 1) * FOX_DH, :]
        vt_ref[h * FOX_VROWS + FOX_DH:(h + 1) * FOX_VROWS, :] = ones
    a = _dot(hh, wsl_ref[...])
    small_ref[...] = a[:, :LANES] + (a[:, LANES:] + _dot(hl, wsh_ref[...]))


def _inproj_call(x2, mod3, gain, w_main, w_vt, ws_hi, ws_lo, seq):
    ws_hilo = jnp.concatenate([ws_hi, ws_lo], axis=1)
    t, d = x2.shape
    tm = INPROJ_TM * INPROJ_SUB
    per_b = seq // tm
    return pl.pallas_call(
        _inproj_kernel,
        grid=(t // tm,),
        in_specs=[
            pl.BlockSpec((tm, d), lambda i: (i, 0)),
            pl.BlockSpec((1, N_MOD, d), lambda i: (i // per_b, 0, 0)),
            pl.BlockSpec((1, d), lambda i: (0, 0)),
            pl.BlockSpec((d, MAIN_COLS), lambda i: (0, 0)),
            pl.BlockSpec((FOX_WIDTH, d), lambda i: (0, 0)),
            pl.BlockSpec((d, LANES), lambda i: (0, 0)),
            pl.BlockSpec((d, 2 * LANES), lambda i: (0, 0)),
        ],
        out_specs=[
            pl.BlockSpec((tm, MAIN_COLS), lambda i: (i, 0)),
            pl.BlockSpec((FOX_HEADS * FOX_VROWS, tm), lambda i: (0, i)),
            pl.BlockSpec((tm, LANES), lambda i: (i, 0)),
        ],
        out_shape=[
            jax.ShapeDtypeStruct((t, MAIN_COLS), BF16),
            jax.ShapeDtypeStruct((FOX_HEADS * FOX_VROWS, t), BF16),
            jax.ShapeDtypeStruct((t, LANES), F32),
        ],
        compiler_params=_params("parallel"),
        name="inproj",
    )(x2, mod3, gain, w_main, w_vt, ws_hi, ws_hilo)


GLA_TM = 512
GLA_SUB = 2
GLA_BLK = 256


def _gla_kernel(q_ref, k_ref, v_ref, g_ref, small_ref, wgh_ref, wgl_ref, bgk_ref, gain_ref, bf_ref,
                sel_ref, ones_ref, o_ref, kx_ref, frow_ref, st_ref, fcar_ref, oacc_ref):
    @pl.when(pl.program_id(1) == 0)
    def _():
        st_ref[...] = jnp.zeros_like(st_ref)
        fcar_ref[...] = jnp.zeros_like(fcar_ref)

    subs = [pl.ds(s * GLA_TM, GLA_TM) for s in range(GLA_SUB)]
    fronts = [_gla_sub_front(q_ref.at[r], k_ref.at[r], v_ref.at[r], small_ref.at[r],
                             wgh_ref, wgl_ref, bgk_ref, oacc_ref.at[r]) for r in subs]
    for r, front in zip(subs, fronts):
        _gla_sub_chain(front, v_ref.at[r], g_ref.at[r], gain_ref, bf_ref, sel_ref, ones_ref,
                       o_ref.at[r], kx_ref.at[r], frow_ref.at[:, :, :, r], st_ref, fcar_ref,
                       oacc_ref.at[r])


def _gla_sub_front(q_ref, k_ref, v_ref, small_ref, wgh_ref, wgl_ref, bgk_ref, oacc_ref):
    tm = GLA_TM
    nc = tm // GLA_CHUNK
    small = small_ref[...]
    sh, sl = _split2(small)
    gk = _dot(sh, wgh_ref[...]) + (_dot(sh, wgl_ref[...]) + _dot(sl, wgh_ref[...])) + bgk_ref[...]
    log_a = _log_sigmoid(gk) * (1.0 / GLA_GATE_NORM)

    blk = GLA_BLK
    blocks = [slice(j * blk, (j + 1) * blk) for j in range(tm // blk)]
    row = lax.broadcasted_iota(jnp.int32, (blk, blk), 0)
    col = lax.broadcasted_iota(jnp.int32, (blk, blk), 1)
    chunk_shift = GLA_CHUNK.bit_length() - 1
    same = (row >> chunk_shift) == (col >> chunk_shift)
    lower = col <= row
    tri = jnp.logical_and(same, lower)
    tri_b = jnp.where(tri, 1.0, 0.0).astype(BF16)
    lower_b = jnp.where(lower, 1.0, 0.0).astype(BF16)

    lh, ll = _split2(log_a)
    b = jnp.concatenate([_dot(tri_b, lh[r]) + _dot(tri_b, ll[r]) for r in blocks], axis=0)
    btot = jnp.concatenate(
        [jnp.broadcast_to(b[(c + 1) * GLA_CHUNK - 1:(c + 1) * GLA_CHUNK, :], (GLA_CHUNK, GLA_QK))
         for c in range(nc)], axis=0)

    q = q_ref[...].astype(F32)
    k = k_ref[...].astype(F32)
    qt = q * jnp.exp(b) * (GLA_DK ** -0.5)
    kt = (k * jnp.exp(-b)).astype(BF16)
    kd = (k * jnp.exp(btot - b)).astype(BF16)

    lane_qk = lax.broadcasted_iota(jnp.int32, (1, GLA_QK), 1)
    head_masks = [jnp.logical_and(lane_qk >= h * GLA_DK, lane_qk < (h + 1) * GLA_DK)
                  for h in range(GLA_HEADS)]

    for h in range(GLA_HEADS):
        cs = slice(h * GLA_DV, (h + 1) * GLA_DV)
        qm = jnp.where(head_masks[h], qt, 0.0).astype(BF16)
        for r in blocks:
            att = jnp.where(tri, _dot_nt(qm[r], kt[r]), 0.0).astype(BF16)
            oacc_ref[r, cs] = _dot(att, v_ref[r, cs])
    return small, qt, kd, btot, head_masks, lower_b


def _gla_sub_chain(front, v_ref, g_ref, gain_ref, bf_ref, sel_ref, ones_ref, o_ref, kx_ref, frow_ref,
                   st_ref, fcar_ref, oacc_ref):
    tm = GLA_TM
    nc = tm // GLA_CHUNK
    small, qt, kd, btot, head_masks, lower_b = front

    chunks = [slice(c * GLA_CHUNK, (c + 1) * GLA_CHUNK) for c in range(nc)]
    kvts = [_dot_tn(v_ref[rs, :], kd[rs]) for rs in chunks]
    states = []
    st = st_ref[...]
    for c, rs in enumerate(chunks):
        states.append(st.astype(BF16))
        decay = jnp.exp(btot[c * GLA_CHUNK:c * GLA_CHUNK + 1, :])
        new = decay * st
        for h in range(GLA_HEADS):
            new = new + jnp.where(head_masks[h], kvts[c][h * GLA_DV:(h + 1) * GLA_DV, :], 0.0)
        st = new
    st_ref[...] = st
    for c, rs in enumerate(chunks):
        qc = qt[rs]
        qs = jnp.concatenate([jnp.where(head_masks[h], qc, 0.0) for h in range(GLA_HEADS)],
                             axis=0).astype(BF16)
        oi = _dot_nt(qs, states[c])
        for h in range(GLA_HEADS):
            oacc_ref[rs, h * GLA_DV:(h + 1) * GLA_DV] += oi[h * GLA_CHUNK:(h + 1) * GLA_CHUNK, :]

    for h in range(GLA_HEADS):
        cs = slice(h * GLA_DV, (h + 1) * GLA_DV)
        y = _rms(oacc_ref[:, cs]) * gain_ref[:, cs]
        g = g_ref[:, cs].astype(F32)
        o_ref[:, cs] = (y * (g * jax.nn.sigmoid(g))).astype(BF16)

    lane = lax.broadcasted_iota(jnp.int32, (1, LANES), 1)
    lf = jnp.where(lane < FOX_HEADS, _log_sigmoid(small + bf_ref[...]), 0.0)
    l1, l2, l3 = _split3(lf)
    carry = fcar_ref[...]
    parts = []
    for j in range(tm // GLA_BLK):
        r = slice(j * GLA_BLK, (j + 1) * GLA_BLK)
        part = _dot(lower_b, l1[r]) + (_dot(lower_b, l2[r]) + _dot(lower_b, l3[r])) + carry
        carry = part[GLA_BLK - 1:GLA_BLK, :]
        parts.append(part)
    fl = jnp.concatenate(parts, axis=0)
    fcar_ref[...] = carry
    fs = fl * LOG2E
    fst = fs.T
    for h in range(FOX_HEADS):
        frow_ref[0, h] = fst[h:h + 1, :]
    n1, n2, n3 = _split3(-fs)
    kx = _dot(n1, sel_ref[0]) + (_dot(n2, sel_ref[1]) + _dot(n3, sel_ref[2])) + ones_ref[...]
    kx_ref[...] = kx.astype(BF16)


def _fox_bias_selectors():
    import numpy as np
    sel = np.zeros((3, LANES, FOX_WIDTH), np.float32)
    ones = np.zeros((1, FOX_WIDTH), np.float32)
    for h in range(FOX_HEADS):
        base = (h // 2) * LANES + (h % 2) * FOX_BIAS_LANES
        for i in range(3):
            sel[i, h, base + i] = 1.0
            ones[0, base + 3 + i] = 1.0
    return jnp.asarray(sel, BF16), jnp.asarray(ones, F32)


def _gla_call(proj, small, wg_hi, wg_lo, b_gk, gain, bf_pad, batch, seq):
    t = proj.shape[0]
    tm = GLA_TM * GLA_SUB
    nb = seq // tm

    def rows(b, i):
        return b * nb + i

    sel, ones = _fox_bias_selectors()
    return pl.pallas_call(
        _gla_kernel,
        grid=(batch, nb),
        in_specs=[
            pl.BlockSpec((tm, GLA_QK), lambda b, i: (rows(b, i), COL_QG // GLA_QK)),
            pl.BlockSpec((tm, GLA_QK), lambda b, i: (rows(b, i), COL_KG // GLA_QK)),
            pl.BlockSpec((tm, GLA_WIDTH), lambda b, i: (rows(b, i), COL_VG // GLA_WIDTH)),
            pl.BlockSpec((tm, GLA_WIDTH), lambda b, i: (rows(b, i), COL_GG // GLA_WIDTH)),
            pl.BlockSpec((tm, LANES), lambda b, i: (rows(b, i), 0)),
            pl.BlockSpec((LANES, GLA_QK), lambda b, i: (0, 0)),
            pl.BlockSpec((LANES, GLA_QK), lambda b, i: (0, 0)),
            pl.BlockSpec((1, GLA_QK), lambda b, i: (0, 0)),
            pl.BlockSpec((1, GLA_WIDTH), lambda b, i: (0, 0)),
            pl.BlockSpec((1, LANES), lambda b, i: (0, 0)),
            pl.BlockSpec((3, LANES, FOX_WIDTH), lambda b, i: (0, 0, 0)),
            pl.BlockSpec((1, FOX_WIDTH), lambda b, i: (0, 0)),
        ],
        out_specs=[
            pl.BlockSpec((tm, GLA_WIDTH), lambda b, i: (rows(b, i), 0)),
            pl.BlockSpec((tm, FOX_WIDTH), lambda b, i: (rows(b, i), 0)),
            pl.BlockSpec((1, FOX_HEADS, 1, tm), lambda b, i: (b, 0, 0, i)),
        ],
        out_shape=[
            jax.ShapeDtypeStruct((t, GLA_WIDTH), BF16),
            jax.ShapeDtypeStruct((t, FOX_WIDTH), BF16),
            jax.ShapeDtypeStruct((batch, FOX_HEADS, 1, seq), F32),
        ],
        scratch_shapes=[
            pltpu.VMEM((GLA_DV, GLA_QK), F32),
            pltpu.VMEM((1, LANES), F32),
            pltpu.VMEM((tm, GLA_WIDTH), F32),
        ],
        compiler_params=_params("parallel", "arbitrary"),
        name="gla",
    )(proj, proj, proj, proj, small, wg_hi, wg_lo, b_gk, gain, bf_pad, sel, ones)


FOX_TQ = 512
FOX_TK = 512
FOX_PAIRS = 2


def _fox_kernel(q_ref, k_ref, kx_ref, vt_ref, frow_ref, o_ref, m_ref, acc_ref, sa_ref, sb_ref,
                ma_ref, mb_ref):
    tq, tk = FOX_TQ, FOX_TK
    i = pl.program_id(2)
    q0 = pl.multiple_of(i * tq, tq)
    lane = lax.broadcasted_iota(jnp.int32, (1, LANES), 1)
    heads = range(2 * FOX_PAIRS)

    def pair_lanes(hh):
        return slice((hh // 2) * LANES, (hh // 2 + 1) * LANES)

    q_aug = []
    for hh in heads:
        head_lanes = (lane < FOX_DH) if hh % 2 == 0 else (lane >= FOX_DH)
        qm = jnp.where(head_lanes, q_ref[:, pair_lanes(hh)], 0.0).astype(BF16)
        fr = jnp.broadcast_to(frow_ref[0, hh, :, pl.ds(q0, LANES)][:, 0:1], (1, LANES))
        f1, f2, f3 = _split3(fr)
        base = (hh % 2) * FOX_BIAS_LANES
        qx = jnp.where(jnp.logical_and(lane >= base, lane < base + 3), 1.0, 0.0).astype(BF16)
        qx = jnp.where(lane == base + 3, f1, qx)
        qx = jnp.where(lane == base + 4, f2, qx)
        qx = jnp.where(lane == base + 5, f3, qx)
        q_aug.append(jnp.concatenate([qm, jnp.broadcast_to(qx, (tq, LANES))], axis=1))

    m_ref[...] = jnp.full(m_ref.shape, NEG, F32)
    acc_ref[...] = jnp.zeros_like(acc_ref)

    bufs = {"a": (sa_ref, ma_ref), "b": (sb_ref, mb_ref)}

    def produce(which, blk, hh, diagonal):
        buf, mx = bufs[which]
        k0 = pl.multiple_of(blk * tk, tk)
        k_aug = jnp.concatenate([k_ref[pl.ds(k0, tk), pair_lanes(hh)],
                                 kx_ref[pl.ds(k0, tk), pair_lanes(hh)]], axis=1)
        st = _dot_nt(k_aug, q_aug[hh])
        if diagonal:
            key = lax.broadcasted_iota(jnp.int32, (tk, tq), 0)
            qry = lax.broadcasted_iota(jnp.int32, (tk, tq), 1)
            st = jnp.where(key <= qry, st, NEG)
        buf[hh] = st
        mx[hh] = jnp.max(st, axis=0, keepdims=True)

    def consume(which, blk, hh):
        buf, mx = bufs[which]
        k0 = pl.multiple_of(blk * tk, tk)
        m_old = m_ref[hh]
        m_new = jnp.maximum(m_old, mx[hh])
        p = jnp.exp2(buf[hh] - m_new).astype(BF16)
        alpha = jnp.exp2(m_old - m_new)
        vt = vt_ref[hh * FOX_VROWS:(hh + 1) * FOX_VROWS, pl.ds(k0, tk)]
        acc_ref[hh] = alpha * acc_ref[hh] + _dot(vt, p)
        m_ref[hh] = m_new

    def stage(cur, cur_blk, nxt=None, nxt_blk=None, nxt_diagonal=False):
        for hh in heads:
            if nxt is not None:
                produce(nxt, nxt_blk, hh, nxt_diagonal)
            if hh > 0:
                consume(cur, cur_blk, hh - 1)
        consume(cur, cur_blk, heads[-1])

    @pl.when(i == 0)
    def _():
        for hh in heads:
            produce("a", 0, hh, True)
        stage("a", 0)

    @pl.when(i > 0)
    def _():
        for hh in heads:
            produce("a", 0, hh, False)

    def pair(t, carry):
        b0 = 2 * t
        stage("a", b0, "b", b0 + 1)
        stage("b", b0 + 1, "a", b0 + 2)
        return carry

    n_pairs = lax.shift_right_logical(jnp.maximum(i - 1, 0), 1)
    lax.fori_loop(0, n_pairs, pair, 0)

    @pl.when((i & 1) == 1)
    def _():
        stage("a", i - 1, "b", i, True)
        stage("b", i)

    @pl.when(jnp.logical_and((i & 1) == 0, i > 0))
    def _():
        stage("a", i - 2, "b", i - 1)
        stage("b", i - 1, "a", i, True)
        stage("a", i)

    for pr in range(FOX_PAIRS):
        outs = []
        for hh in (2 * pr, 2 * pr + 1):
            acc = acc_ref[hh]
            outs.append(acc[:FOX_DH] * (1.0 / acc[FOX_DH:FOX_DH + 1]))
        o_ref[:, pr * LANES:(pr + 1) * LANES] = jnp.concatenate(outs, axis=0).T.astype(BF16)


def _fox_call(proj, kx, vt, frow, batch, seq):
    t = proj.shape[0]
    tq = FOX_TQ
    nq = seq // tq
    width = FOX_PAIRS * LANES
    groups = FOX_WIDTH // width
    nh = 2 * FOX_PAIRS
    return pl.pallas_call(
        _fox_kernel,
        grid=(batch, groups, nq),
        in_specs=[
            pl.BlockSpec((tq, width), lambda b, hp, i: (b * nq + i, COL_QF // width + hp)),
            pl.BlockSpec((seq, width), lambda b, hp, i: (b, COL_KF // width + hp)),
            pl.BlockSpec((seq, width), lambda b, hp, i: (b, hp)),
            pl.BlockSpec((nh * FOX_VROWS, seq), lambda b, hp, i: (hp, b)),
            pl.BlockSpec((1, nh, 1, seq), lambda b, hp, i: (b, hp, 0, 0)),
        ],
        out_specs=pl.BlockSpec((tq, width), lambda b, hp, i: (b * nq + i, hp)),
        out_shape=jax.ShapeDtypeStruct((t, FOX_WIDTH), BF16),
        scratch_shapes=[
            pltpu.VMEM((nh, 1, tq), F32),
            pltpu.VMEM((nh, FOX_VROWS, tq), F32),
            pltpu.VMEM((nh, FOX_TK, tq), F32),
            pltpu.VMEM((nh, FOX_TK, tq), F32),
            pltpu.VMEM((nh, 1, tq), F32),
            pltpu.VMEM((nh, 1, tq), F32),
        ],
        compiler_params=_params("parallel", "parallel", "arbitrary"),
        name="fox",
    )(proj, proj, kx, vt, frow)


POST_TM = 512
POST_SUB = 2


def _post_kernel(og_ref, of_ref, x_ref, mod_ref, pg_ref, mg_ref, wo_ref, wrh_ref, wrl_ref, br_ref,
                 x1_ref, h2_ref, idx_ref, rank_ref, gate_ref, cnt_ref, car_ref):
    @pl.when(pl.program_id(0) == 0)
    def _():
        car_ref[...] = jnp.zeros_like(car_ref)

    ys = []
    for s in range(POST_SUB):
        rows = slice(s * POST_TM, (s + 1) * POST_TM)
        ys.append(_dot(og_ref[rows, :], wo_ref[:GLA_WIDTH, :])
                  + _dot(of_ref[rows, :], wo_ref[GLA_WIDTH:, :]))
    for s in range(POST_SUB):
        _post_route(s, ys[s], x_ref, mod_ref, pg_ref, mg_ref, wrh_ref, wrl_ref, br_ref,
                    x1_ref, h2_ref, idx_ref, rank_ref, gate_ref, cnt_ref, car_ref)


def _post_route(s, y, x_ref, mod_ref, pg_ref, mg_ref, wrh_ref, wrl_ref, br_ref,
                x1_ref, h2_ref, idx_ref, rank_ref, gate_ref, cnt_ref, car_ref):
    tm = POST_TM
    rows = slice(s * tm, (s + 1) * tm)
    x1 = x_ref[rows, :] + mod_ref[0, 2:3, :] * (_rms(y) * pg_ref[...])
    x1_ref[rows, :] = x1
    h2 = _rms(x1) * mg_ref[...]
    h2 = h2 * (1.0 + mod_ref[0, 4:5, :]) + mod_ref[0, 3:4, :]
    for j in range(ROW_TILES):
        h2_ref[pl.ds(s * tm * ROW_TILES + j, tm, stride=ROW_TILES), :] = h2[:, j * LANES:(j + 1) * LANES]

    hh, hl = _split2(h2)
    logits = _dot(hh, wrh_ref[...]) + (_dot(hh, wrl_ref[...]) + _dot(hl, wrh_ref[...])) + br_ref[...]
    lane = lax.broadcasted_iota(jnp.int32, (tm, LANES), 1).astype(F32)
    work = jnp.where(lane < N_EXPERTS, logits, -jnp.inf)
    vals, idxs = [], []
    for _ in range(TOP_K):
        mx = jnp.max(work, axis=-1, keepdims=True)
        ix = jnp.min(jnp.where(work == mx, lane, float(LANES)), axis=-1, keepdims=True)
        vals.append(mx)
        idxs.append(ix)
        work = jnp.where(lane == ix, -jnp.inf, work)
    es = [jnp.exp(v - vals[0]) for v in vals]
    den = es[0] + es[1] + es[2] + es[3]

    onehot = jnp.zeros((tm, LANES), F32)
    for ix in idxs:
        onehot = onehot + jnp.where(lane == ix, 1.0, 0.0)
    row = lax.broadcasted_iota(jnp.int32, (tm, tm), 0)
    col = lax.broadcasted_iota(jnp.int32, (tm, tm), 1)
    strict_b = jnp.where(col < row, 1.0, 0.0).astype(BF16)
    prefix = _dot(strict_b, onehot.astype(BF16)) + car_ref[...]
    car_new = car_ref[...] + jnp.sum(onehot, axis=0, keepdims=True)
    car_ref[...] = car_new
    cnt_ref[...] = car_new

    idx_o = jnp.zeros((tm, LANES), F32)
    rank_o = jnp.zeros((tm, LANES), F32)
    gate_o = jnp.zeros((tm, LANES), F32)
    for kk in range(TOP_K):
        rk = jnp.sum(jnp.where(lane == idxs[kk], prefix, 0.0), axis=-1, keepdims=True)
        sel = lane == float(kk)
        idx_o = jnp.where(sel, idxs[kk], idx_o)
        rank_o = jnp.where(sel, rk, rank_o)
        gate_o = jnp.where(sel, es[kk] / den, gate_o)
    idx_ref[rows, :] = idx_o.astype(jnp.int32)
    rank_ref[rows, :] = rank_o.astype(jnp.int32)
    gate_ref[rows, :] = gate_o


def _post_call(o_gla, o_fox, x2, mod3, post_gain, mlp_gain, w_o, wr_hi, wr_lo, br_pad, seq):
    t, d = x2.shape
    tm = POST_TM * POST_SUB
    per_b = seq // tm
    return pl.pallas_call(
        _post_kernel,
        grid=(t // tm,),
        in_specs=[
            pl.BlockSpec((tm, GLA_WIDTH), lambda i: (i, 0)),
            pl.BlockSpec((tm, FOX_WIDTH), lambda i: (i, 0)),
            pl.BlockSpec((tm, d), lambda i: (i, 0)),
            pl.BlockSpec((1, N_MOD, d), lambda i: (i // per_b, 0, 0)),
            pl.BlockSpec((1, d), lambda i: (0, 0)),
            pl.BlockSpec((1, d), lambda i: (0, 0)),
            pl.BlockSpec((GLA_WIDTH + FOX_WIDTH, d), lambda i: (0, 0)),
            pl.BlockSpec((d, LANES), lambda i: (0, 0)),
            pl.BlockSpec((d, LANES), lambda i: (0, 0)),
            pl.BlockSpec((1, LANES), lambda i: (0, 0)),
        ],
        out_specs=[
            pl.BlockSpec((tm, d), lambda i: (i, 0)),
            pl.BlockSpec((tm * ROW_TILES, LANES), lambda i: (i, 0)),
            pl.BlockSpec((tm, LANES), lambda i: (i, 0)),
            pl.BlockSpec((tm, LANES), lambda i: (i, 0)),
            pl.BlockSpec((tm, LANES), lambda i: (i, 0)),
            pl.BlockSpec((1, LANES), lambda i: (0, 0)),
        ],
        out_shape=[
            jax.ShapeDtypeStruct((t, d), F32),
            jax.ShapeDtypeStruct((t * ROW_TILES, LANES), F32),
            jax.ShapeDtypeStruct((t, LANES), jnp.int32),
            jax.ShapeDtypeStruct((t, LANES), jnp.int32),
            jax.ShapeDtypeStruct((t, LANES), F32),
            jax.ShapeDtypeStruct((1, LANES), F32),
        ],
        scratch_shapes=[pltpu.VMEM((1, LANES), F32)],
        compiler_params=_params("arbitrary"),
        name="post",
    )(o_gla, o_fox, x2, mod3, post_gain, mlp_gain, w_o, wr_hi, wr_lo, br_pad)


DISP_TM = 512
DISP_SLOTS = 3
ISSUE_UNROLL = 8


def _row_copy(src_ref, src_row, dst_ref, dst_row, sem):
    return pltpu.make_async_copy(
        src_ref.at[pl.ds(pl.multiple_of(src_row * ROW_TILES, ROW_TILES), ROW_TILES)],
        dst_ref.at[pl.ds(pl.multiple_of(dst_row * ROW_TILES, ROW_TILES), ROW_TILES)],
        sem)


def _dispatch_kernel(pend_ref, pos_ref, h_hbm, xs_ref, zero_ref, hbuf_ref, sem, in_sem, out_sem):
    tm = DISP_TM
    rows = tm * ROW_TILES
    zrows = MOE_TM * ROW_TILES
    i = pl.program_id(0)
    last = pl.num_programs(0) - 1
    slot = lax.rem(i, DISP_SLOTS)
    nxt = lax.rem(i + 1, DISP_SLOTS)

    def load(step, s):
        return pltpu.make_async_copy(
            h_hbm.at[pl.ds(pl.multiple_of(step * rows, rows), rows)], hbuf_ref.at[s], in_sem.at[s])

    def drain(s):
        for _ in range(TOP_K):
            pltpu.make_async_copy(hbuf_ref.at[s], xs_ref.at[pl.ds(0, rows)], out_sem.at[s]).wait()

    @pl.when(i == 0)
    def _():
        load(0, 0).start()
        zero_ref[...] = jnp.zeros_like(zero_ref)

        def last_tile(e):
            start = pl.multiple_of((pend_ref[e] - MOE_TM) * ROW_TILES, zrows)
            return pltpu.make_async_copy(zero_ref, xs_ref.at[pl.ds(start, zrows)], sem)

        def nonempty(e):
            return pend_ref[e] > jnp.where(e == 0, 0, pend_ref[jnp.maximum(e - 1, 0)])

        def clear(e, carry):
            @pl.when(nonempty(e))
            def _():
                last_tile(e).start()
            return carry

        def clear_wait(e, carry):
            @pl.when(nonempty(e))
            def _():
                last_tile(e).wait()
            return carry

        lax.fori_loop(0, N_EXPERTS, clear, 0)
        lax.fori_loop(0, N_EXPERTS, clear_wait, 0)

        def tail_tile(j):
            return pltpu.make_async_copy(
                zero_ref, xs_ref.at[pl.ds(pl.multiple_of(j * zrows, zrows), zrows)], sem)

        def tail(j, carry):
            tail_tile(j).start()
            return carry

        def tail_wait(j, carry):
            tail_tile(j).wait()
            return carry

        first_unused = lax.div(pend_ref[N_EXPERTS - 1], MOE_TM)
        n_tiles = xs_ref.shape[0] // zrows
        lax.fori_loop(first_unused, n_tiles, tail, 0)
        lax.fori_loop(first_unused, n_tiles, tail_wait, 0)

    @pl.when(i >= DISP_SLOTS - 1)
    def _():
        drain(nxt)

    @pl.when(i < last)
    def _():
        load(i + 1, nxt).start()

    load(i, slot).wait()

    def issue(g, carry):
        for u in range(ISSUE_UNROLL):
            t = g * ISSUE_UNROLL + u
            for kk in range(TOP_K):
                _row_copy(hbuf_ref.at[slot], t, xs_ref, pos_ref[0, 0, t * TOP_K + kk],
                          out_sem.at[slot]).start(priority=kk % 2)
        return carry

    lax.fori_loop(0, tm // ISSUE_UNROLL, issue, 0)

    @pl.when(i == last)
    def _():
        for back in range(DISP_SLOTS - 1):
            @pl.when(i >= back)
            def _():
                drain(lax.rem(i - back + DISP_SLOTS, DISP_SLOTS))


def _dispatch_call(pend, pos3, h2, n_tiles):
    tm = DISP_TM
    n_steps = pos3.shape[0]
    grid_spec = pltpu.PrefetchScalarGridSpec(
        num_scalar_prefetch=1,
        grid=(n_steps,),
        in_specs=[
            pl.BlockSpec((1, 1, tm * TOP_K), lambda i, pend: (i, 0, 0), memory_space=pltpu.SMEM),
            pl.BlockSpec(memory_space=pl.ANY),
        ],
        out_specs=pl.BlockSpec(memory_space=pl.ANY),
        scratch_shapes=[
            pltpu.VMEM((MOE_TM * ROW_TILES, LANES), F32),
            pltpu.VMEM((DISP_SLOTS, tm * ROW_TILES, LANES), F32),
            pltpu.SemaphoreType.DMA,
            pltpu.SemaphoreType.DMA((DISP_SLOTS,)),
            pltpu.SemaphoreType.DMA((DISP_SLOTS,)),
        ],
    )
    return pl.pallas_call(
        _dispatch_kernel,
        grid_spec=grid_spec,
        out_shape=jax.ShapeDtypeStruct((n_tiles * MOE_TM * ROW_TILES, LANES), F32),
        compiler_params=_params("arbitrary"),
        name="dispatch",
    )(pend, pos3, h2)


MOE_TM = 512


def _experts_kernel(te_ref, nu_ref, x_ref, win_ref, bin_ref, wout_ref, bout_ref, y_ref,
                    wib_ref, wob_ref):
    tm = MOE_TM
    i = pl.program_id(0)
    used = i < nu_ref[0]

    @pl.when(jnp.logical_not(used))
    def _():
        y_ref[...] = jnp.zeros_like(y_ref)

    fresh = jnp.logical_or(i == 0, te_ref[i] != te_ref[jnp.maximum(i - 1, 0)])

    @pl.when(jnp.logical_and(used, fresh))
    def _():
        wib_ref[...] = win_ref[0].astype(BF16)
        wob_ref[...] = wout_ref[0].astype(BF16)

    @pl.when(used)
    def _():
        x = jnp.concatenate([x_ref[pl.ds(j, tm, stride=ROW_TILES), :] for j in range(ROW_TILES)],
                            axis=1).astype(BF16)
        u = _dot(x, wib_ref[...]) + bin_ref[0]
        glu = jnp.minimum(u[:, :D_FF], SWIGLU_LIMIT)
        lin = jnp.clip(u[:, D_FF:], -SWIGLU_LIMIT, SWIGLU_LIMIT)
        a = glu * jax.nn.sigmoid(SWIGLU_ALPHA * glu) * (lin + 1.0)
        y = _dot(a.astype(BF16), wob_ref[...]) + bout_ref[0]
        for j in range(ROW_TILES):
            y_ref[pl.ds(j, tm, stride=ROW_TILES), :] = y[:, j * LANES:(j + 1) * LANES]


def _experts_call(tile_expert, n_used, xs, w_in, b_in, w_out, b_out):
    tm = MOE_TM
    n_tiles = tile_expert.shape[0]
    d = D_MODEL

    def tile(i, te, nu):
        return (jnp.minimum(i, nu[0] - 1), 0)

    grid_spec = pltpu.PrefetchScalarGridSpec(
        num_scalar_prefetch=2,
        grid=(n_tiles,),
        in_specs=[
            pl.BlockSpec((tm * ROW_TILES, LANES), tile),
            pl.BlockSpec((1, d, 2 * D_FF), lambda i, te, nu: (te[i], 0, 0)),
            pl.BlockSpec((1, 1, 2 * D_FF), lambda i, te, nu: (te[i], 0, 0)),
            pl.BlockSpec((1, D_FF, d), lambda i, te, nu: (te[i], 0, 0)),
            pl.BlockSpec((1, 1, d), lambda i, te, nu: (te[i], 0, 0)),
        ],
        out_specs=pl.BlockSpec((tm * ROW_TILES, LANES), lambda i, te, nu: (i, 0)),
        scratch_shapes=[
            pltpu.VMEM((d, 2 * D_FF), BF16),
            pltpu.VMEM((D_FF, d), BF16),
        ],
    )
    return pl.pallas_call(
        _experts_kernel,
        grid_spec=grid_spec,
        out_shape=jax.ShapeDtypeStruct(xs.shape, F32),
        compiler_params=_params("arbitrary"),
        name="experts",
    )(tile_expert, n_used, xs, w_in, b_in, w_out, b_out)


COMB_TM = 256


def _combine_kernel(pos_ref, posn_ref, y_ref, gate_ref, x1_ref, mod_ref, pg_ref, o_ref, buf_ref, sem):
    tm = COMB_TM
    i = pl.program_id(0)
    slot = i & 1

    def gather(p_ref, s):
        def issue(g, carry):
            for u in range(ISSUE_UNROLL):
                t = g * ISSUE_UNROLL + u
                for kk in range(TOP_K):
                    _row_copy(y_ref, p_ref[0, 0, t * TOP_K + kk], buf_ref.at[s, kk], t,
                              sem.at[s]).start(priority=kk % 2)
            return carry

        lax.fori_loop(0, tm // ISSUE_UNROLL, issue, 0)

    @pl.when(i == 0)
    def _():
        gather(pos_ref, 0)

    @pl.when(i + 1 < pl.num_programs(0))
    def _():
        gather(posn_ref, 1 - slot)

    for kk in range(TOP_K):
        pltpu.make_async_copy(y_ref.at[pl.ds(0, tm * ROW_TILES)], buf_ref.at[slot, kk],
                              sem.at[slot]).wait()

    gates = gate_ref[...]
    acc = None
    for kk in range(TOP_K):
        rows = jnp.concatenate(
            [buf_ref[slot, kk, pl.ds(j, tm, stride=ROW_TILES), :] for j in range(ROW_TILES)], axis=1)
        term = rows * gates[:, kk:kk + 1]
        acc = term if acc is None else acc + term
    o_ref[...] = x1_ref[...] + mod_ref[0, 5:6, :] * (_rms(acc) * pg_ref[...])


def _combine_call(pos3, y, gates, x1, mod3, post_gain, seq):
    t, d = x1.shape
    tm = COMB_TM
    per_b = seq // tm
    n_steps = t // tm
    return pl.pallas_call(
        _combine_kernel,
        grid=(n_steps,),
        in_specs=[
            pl.BlockSpec((1, 1, tm * TOP_K), lambda i: (i, 0, 0), memory_space=pltpu.SMEM),
            pl.BlockSpec((1, 1, tm * TOP_K), lambda i: (jnp.minimum(i + 1, n_steps - 1), 0, 0),
                         memory_space=pltpu.SMEM),
            pl.BlockSpec(memory_space=pl.ANY),
            pl.BlockSpec((tm, LANES), lambda i: (i, 0)),
            pl.BlockSpec((tm, d), lambda i: (i, 0)),
            pl.BlockSpec((1, N_MOD, d), lambda i: (i // per_b, 0, 0)),
            pl.BlockSpec((1, d), lambda i: (0, 0)),
        ],
        out_specs=pl.BlockSpec((tm, d), lambda i: (i, 0)),
        out_shape=jax.ShapeDtypeStruct((t, d), F32),
        scratch_shapes=[
            pltpu.VMEM((2, TOP_K, tm * ROW_TILES, LANES), F32),
            pltpu.SemaphoreType.DMA((2,)),
        ],
        compiler_params=_params("arbitrary"),
        name="combine",
    )(pos3, pos3, y, gates, x1, mod3, post_gain)


def _pad_cols(w, n):
    return jnp.pad(w, ((0, 0), (0, n - w.shape[1])))


def kernel(x, c, w_ada, b_ada, attn_pre_gain, attn_post_gain, w_in, w_gk2, b_gk, gla_norm_gain, b_f,
           w_o, mlp_pre_gain, mlp_post_gain, w_router, b_router, w_e_in, b_e_in, w_e_out, b_e_out):
    batch, seq, d = x.shape
    t = batch * seq
    x2 = x.reshape(t, d)

    o = 0
    cols = {}
    for name, width in (("qg", GLA_QK), ("kg", GLA_QK), ("vg", GLA_WIDTH), ("gg", GLA_WIDTH),
                        ("zg", GLA_LOWRANK), ("qf", FOX_WIDTH), ("kf", FOX_WIDTH), ("vf", FOX_WIDTH),
                        ("ff", FOX_HEADS)):
        cols[name] = w_in[:, o:o + width]
        o += width
    gla_cols = 2 * GLA_QK + 2 * GLA_WIDTH
    fox_start = gla_cols + GLA_LOWRANK
    w_main = jnp.concatenate([w_in[:, :gla_cols], w_in[:, fox_start:fox_start + 2 * FOX_WIDTH]],
                             axis=1).astype(BF16)
    w_vt = cols["vf"].T.astype(BF16)
    w_small = _pad_cols(jnp.concatenate([cols["ff"], cols["zg"]], axis=1), LANES)
    ws_hi, ws_lo = _split2(w_small)
    wg = jnp.zeros((LANES, GLA_QK), F32).at[SMALL_ZG:SMALL_ZG + GLA_LOWRANK].set(w_gk2)
    wg_hi, wg_lo = _split2(wg)
    bf_pad = jnp.zeros((1, LANES), F32).at[0, SMALL_FF:SMALL_FF + FOX_HEADS].set(b_f)
    wr_hi, wr_lo = _split2(_pad_cols(w_router, LANES))
    br_pad = _pad_cols(b_router.reshape(1, N_EXPERTS), LANES)

    c_pad = jnp.pad(c, ((0, SUBLANES - batch % SUBLANES if batch % SUBLANES else 0), (0, 0)))
    mod = _mod_call(c_pad, w_ada, b_ada.reshape(1, -1))
    mod3 = mod[:batch].reshape(batch, N_MOD, d)

    proj, vt, small = _inproj_call(x2, mod3, attn_pre_gain.reshape(1, d), w_main, w_vt, ws_hi, ws_lo,
                                   seq)
    o_gla, kx, frow = _gla_call(proj, small, wg_hi, wg_lo, b_gk.reshape(1, -1),
                                gla_norm_gain.reshape(1, -1), bf_pad, batch, seq)
    o_fox = _fox_call(proj, kx, vt, frow, batch, seq)

    x1, h2, idx_w, rank_w, gate_w, cnt = _post_call(
        o_gla, o_fox, x2, mod3, attn_post_gain.reshape(1, d), mlp_pre_gain.reshape(1, d),
        w_o.astype(BF16), wr_hi, wr_lo, br_pad, seq)

    tm = MOE_TM
    n_tiles = (t * TOP_K) // tm + N_EXPERTS
    counts = cnt[0, :N_EXPERTS].astype(jnp.int32)
    padded = ((counts + tm - 1) // tm) * tm
    pend = jnp.cumsum(padded)
    pstart = pend - padded
    experts = jnp.arange(N_EXPERTS, dtype=jnp.int32)
    first_slot = jnp.sum(jnp.where(idx_w[:, :TOP_K, None] == experts, pstart.astype(jnp.int32), 0),
                         axis=-1)
    pos = first_slot + rank_w[:, :TOP_K]
    n_used = (pend[-1] // tm).astype(jnp.int32).reshape(1)
    tile_start = jnp.arange(n_tiles, dtype=jnp.int32) * tm
    tile_expert = jnp.minimum(
        jnp.sum((pend[None, :] <= tile_start[:, None]).astype(jnp.int32), axis=1), N_EXPERTS - 1)

    pos_d = pos.reshape(t // DISP_TM, 1, DISP_TM * TOP_K)
    xs = _dispatch_call(pend.astype(jnp.int32), pos_d, h2, n_tiles)
    y = _experts_call(tile_expert, n_used, xs, w_e_in, b_e_in.reshape(N_EXPERTS, 1, -1), w_e_out,
                      b_e_out.reshape(N_EXPERTS, 1, -1))
    pos_c = pos.reshape(t // COMB_TM, 1, COMB_TM * TOP_K)
    out = _combine_call(pos_c, y, gate_w, x1, mod3, mlp_post_gain.reshape(1, d), seq)
    return out.reshape(batch, seq, d)
```

```python
import jax
import jax.numpy as jnp
from jax import lax
from jax.experimental import pallas as pl
from jax.experimental.pallas import tpu as pltpu

F32 = jnp.float32
BF16 = jnp.bfloat16

D_MODEL = 1024
N_MOD = 6
GLA_HEADS = 4
GLA_DK = 64
GLA_DV = 128
GLA_QK = GLA_HEADS * GLA_DK
GLA_WIDTH = GLA_HEADS * GLA_DV
GLA_LOWRANK = 16
GLA_CHUNK = 64
GLA_GATE_NORM = 16.0
FOX_HEADS = 8
FOX_DH = 64
FOX_WIDTH = FOX_HEADS * FOX_DH
N_EXPERTS = 32
TOP_K = 4
D_FF = D_MODEL
SWIGLU_ALPHA = 1.702
SWIGLU_LIMIT = 7.0
EPS = 1e-6

LANES = 128
SUBLANES = 8
ROW_TILES = D_MODEL // LANES

COL_QG, COL_KG, COL_VG, COL_GG = 0, 256, 512, 1024
COL_QF, COL_KF = 1536, 2048
MAIN_COLS = 2560
LOG2E = 1.4426950408889634
FOX_Q_SCALE = (FOX_DH ** -0.5) * LOG2E
FOX_BIAS_LANES = 6
BF16_SUBLANES = 16
FOX_VROWS = FOX_DH + BF16_SUBLANES
SMALL_FF, SMALL_ZG = 0, 8

NEG = -1e30
VMEM_LIMIT = 56 * 1024 * 1024


def _dot(a, b):
    return jnp.dot(a, b, preferred_element_type=F32)


def _dot_nt(a, b):
    return lax.dot_general(a, b, (((1,), (1,)), ((), ())), preferred_element_type=F32)


def _dot_tn(a, b):
    return lax.dot_general(a, b, (((0,), (0,)), ((), ())), preferred_element_type=F32)


def _split2(a):
    hi = a.astype(BF16)
    lo = (a - hi.astype(F32)).astype(BF16)
    return hi, lo


def _split3(a):
    hi = a.astype(BF16)
    r = a - hi.astype(F32)
    mid = r.astype(BF16)
    lo = (r - mid.astype(F32)).astype(BF16)
    return hi, mid, lo


def _log_sigmoid(x):
    return jnp.minimum(x, 0.0) - jnp.log1p(jnp.exp(-jnp.abs(x)))


def _rms(x):
    return x * lax.rsqrt(jnp.mean(x * x, axis=-1, keepdims=True) + EPS)


def _params(*sem, flags=None):
    return pltpu.CompilerParams(dimension_semantics=sem, vmem_limit_bytes=VMEM_LIMIT, flags=flags)


def _mod_kernel(c_ref, w_ref, b_ref, o_ref):
    c = c_ref[...]
    s = c * jax.nn.sigmoid(c)
    sh, sl = _split2(s)
    wh, wl = _split2(w_ref[...])
    o_ref[...] = _dot(sh, wh) + (_dot(sh, wl) + _dot(sl, wh)) + b_ref[...]


def _mod_call(c_pad, w_ada, b_ada):
    rows, d = c_pad.shape
    n = w_ada.shape[1]
    tn = D_MODEL
    return pl.pallas_call(
        _mod_kernel,
        grid=(n // tn,),
        in_specs=[
            pl.BlockSpec((rows, d), lambda j: (0, 0)),
            pl.BlockSpec((d, tn), lambda j: (0, j)),
            pl.BlockSpec((1, tn), lambda j: (0, j)),
        ],
        out_specs=pl.BlockSpec((rows, tn), lambda j: (0, j)),
        out_shape=jax.ShapeDtypeStruct((rows, n), F32),
        compiler_params=_params("parallel"),
        name="mod",
    )(c_pad, w_ada, b_ada)


INPROJ_TM = 512
INPROJ_SUB = 2


def _inproj_kernel(x_ref, mod_ref, g_ref, w_ref, wvt_ref, wsh_ref, wsl_ref, proj_ref, vt_ref, small_ref):
    subs = [pl.ds(s * INPROJ_TM, INPROJ_TM) for s in range(INPROJ_SUB)]
    hs = []
    for r in subs:
        h = _rms(x_ref[r, :]) * g_ref[...]
        h = h * (1.0 + mod_ref[0, 1:2, :]) + mod_ref[0, 0:1, :]
        hs.append(_split2(h))
    for r, (hh, hl) in zip(subs, hs):
        _inproj_sub(hh, hl, w_ref, wvt_ref, wsh_ref, wsl_ref, proj_ref.at[r], vt_ref.at[:, r],
                    small_ref.at[r])


def _inproj_sub(hh, hl, w_ref, wvt_ref, wsh_ref, wsl_ref, proj_ref, vt_ref, small_ref):
    cw = FOX_WIDTH
    for j in range(MAIN_COLS // cw):
        r = _dot(hh, w_ref[:, j * cw:(j + 1) * cw])
        if j * cw == COL_QF:
            r = r * FOX_Q_SCALE
        proj_ref[:, j * cw:(j + 1) * cw] = r.astype(BF16)
    vt = _dot_nt(wvt_ref[...], hh).astype(BF16)
    ones = jnp.ones((FOX_VROWS - FOX_DH, vt.shape[1]), BF16)
    for h in range(FOX_HEADS):
        vt_ref[h * FOX_VROWS:h * FOX_VROWS + FOX_DH, :] = vt[h * FOX_DH:(h + 1) * FOX_DH, :]
        vt_ref[h * FOX_VROWS + FOX_DH:(h + 1) * FOX_VROWS, :] = ones
    a = _dot(hh, wsl_ref[...])
    small_ref[...] = a[:, :LANES] + (a[:, LANES:] + _dot(hl, wsh_ref[...]))


def _inproj_call(x2, mod3, gain, w_main, w_vt, ws_hi, ws_lo, seq):
    ws_hilo = jnp.concatenate([ws_hi, ws_lo], axis=1)
    t, d = x2.shape
    tm = INPROJ_TM * INPROJ_SUB
    per_b = seq // tm
    return pl.pallas_call(
        _inproj_kernel,
        grid=(t // tm,),
        in_specs=[
            pl.BlockSpec((tm, d), lambda i: (i, 0)),
            pl.BlockSpec((1, N_MOD, d), lambda i: (i // per_b, 0, 0)),
            pl.BlockSpec((1, d), lambda i: (0, 0)),
            pl.BlockSpec((d, MAIN_COLS), lambda i: (0, 0)),
            pl.BlockSpec((FOX_WIDTH, d), lambda i: (0, 0)),
            pl.BlockSpec((d, LANES), lambda i: (0, 0)),
            pl.BlockSpec((d, 2 * LANES), lambda i: (0, 0)),
        ],
        out_specs=[
            pl.BlockSpec((tm, MAIN_COLS), lambda i: (i, 0)),
            pl.BlockSpec((FOX_HEADS * FOX_VROWS, tm), lambda i: (0, i)),
            pl.BlockSpec((tm, LANES), lambda i: (i, 0)),
        ],
        out_shape=[
            jax.ShapeDtypeStruct((t, MAIN_COLS), BF16),
            jax.ShapeDtypeStruct((FOX_HEADS * FOX_VROWS, t), BF16),
            jax.ShapeDtypeStruct((t, LANES), F32),
        ],
        compiler_params=_params("parallel"),
        name="inproj",
    )(x2, mod3, gain, w_main, w_vt, ws_hi, ws_hilo)


GLA_TM = 512
GLA_SUB = 2
GLA_BLK = 256


def _gla_kernel(q_ref, k_ref, v_ref, g_ref, small_ref, wgh_ref, wgl_ref, bgk_ref, gain_ref, bf_ref,
                sel_ref, ones_ref, o_ref, kx_ref, frow_ref, st_ref, fcar_ref, oacc_ref):
    @pl.when(pl.program_id(1) == 0)
    def _():
        st_ref[...] = jnp.zeros_like(st_ref)
        fcar_ref[...] = jnp.zeros_like(fcar_ref)

    subs = [pl.ds(s * GLA_TM, GLA_TM) for s in range(GLA_SUB)]
    fronts = [_gla_sub_front(q_ref.at[r], k_ref.at[r], v_ref.at[r], small_ref.at[r],
                             wgh_ref, wgl_ref, bgk_ref, oacc_ref.at[r]) for r in subs]
    for r, front in zip(subs, fronts):
        _gla_sub_chain(front, v_ref.at[r], g_ref.at[r], gain_ref, bf_ref, sel_ref, ones_ref,
                       o_ref.at[r], kx_ref.at[r], frow_ref.at[:, :, :, r], st_ref, fcar_ref,
                       oacc_ref.at[r])


def _gla_sub_front(q_ref, k_ref, v_ref, small_ref, wgh_ref, wgl_ref, bgk_ref, oacc_ref):
    tm = GLA_TM
    nc = tm // GLA_CHUNK
    small = small_ref[...]
    sh, sl = _split2(small)
    gk = _dot(sh, wgh_ref[...]) + (_dot(sh, wgl_ref[...]) + _dot(sl, wgh_ref[...])) + bgk_ref[...]
    log_a = _log_sigmoid(gk) * (1.0 / GLA_GATE_NORM)

    blk = GLA_BLK
    blocks = [slice(j * blk, (j + 1) * blk) for j in range(tm // blk)]
    row = lax.broadcasted_iota(jnp.int32, (blk, blk), 0)
    col = lax.broadcasted_iota(jnp.int32, (blk, blk), 1)
    chunk_shift = GLA_CHUNK.bit_length() - 1
    same = (row >> chunk_shift) == (col >> chunk_shift)
    lower = col <= row
    tri = jnp.logical_and(same, lower)
    tri_b = jnp.where(tri, 1.0, 0.0).astype(BF16)
    lower_b = jnp.where(lower, 1.0, 0.0).astype(BF16)

    lh, ll = _split2(log_a)
    b = jnp.concatenate([_dot(tri_b, lh[r]) + _dot(tri_b, ll[r]) for r in blocks], axis=0)
    btot = jnp.concatenate(
        [jnp.broadcast_to(b[(c + 1) * GLA_CHUNK - 1:(c + 1) * GLA_CHUNK, :], (GLA_CHUNK, GLA_QK))
         for c in range(nc)], axis=0)

    q = q_ref[...].astype(F32)
    k = k_ref[...].astype(F32)
    qt = q * jnp.exp(b) * (GLA_DK ** -0.5)
    kt = (k * jnp.exp(-b)).astype(BF16)
    kd = (k * jnp.exp(btot - b)).astype(BF16)

    lane_qk = lax.broadcasted_iota(jnp.int32, (1, GLA_QK), 1)
    head_masks = [jnp.logical_and(lane_qk >= h * GLA_DK, lane_qk < (h + 1) * GLA_DK)
                  for h in range(GLA_HEADS)]

    for h in range(GLA_HEADS):
        cs = slice(h * GLA_DV, (h + 1) * GLA_DV)
        qm = jnp.where(head_masks[h], qt, 0.0).astype(BF16)
        for r in blocks:
            att = jnp.where(tri, _dot_nt(qm[r], kt[r]), 0.0).astype(BF16)
            oacc_ref[r, cs] = _dot(att, v_ref[r, cs])
    return small, qt, kd, btot, head_masks, lower_b


def _gla_sub_chain(front, v_ref, g_ref, gain_ref, bf_ref, sel_ref, ones_ref, o_ref, kx_ref, frow_ref,
                   st_ref, fcar_ref, oacc_ref):
    tm = GLA_TM
    nc = tm // GLA_CHUNK
    small, qt, kd, btot, head_masks, lower_b = front

    chunks = [slice(c * GLA_CHUNK, (c + 1) * GLA_CHUNK) for c in range(nc)]
    kvts = [_dot_tn(v_ref[rs, :], kd[rs]) for rs in chunks]
    states = []
    st = st_ref[...]
    for c, rs in enumerate(chunks):
        states.append(st.astype(BF16))
        decay = jnp.exp(btot[c * GLA_CHUNK:c * GLA_CHUNK + 1, :])
        new = decay * st
        for h in range(GLA_HEADS):
            new = new + jnp.where(head_masks[h], kvts[c][h * GLA_DV:(h + 1) * GLA_DV, :], 0.0)
        st = new
    st_ref[...] = st
    for c, rs in enumerate(chunks):
        qc = qt[rs]
        qs = jnp.concatenate([jnp.where(head_masks[h], qc, 0.0) for h in range(GLA_HEADS)],
                             axis=0).astype(BF16)
        oi = _dot_nt(qs, states[c])
        for h in range(GLA_HEADS):
            oacc_ref[rs, h * GLA_DV:(h + 1) * GLA_DV] += oi[h * GLA_CHUNK:(h + 1) * GLA_CHUNK, :]

    for h in range(GLA_HEADS):
        cs = slice(h * GLA_DV, (h + 1) * GLA_DV)
        y = _rms(oacc_ref[:, cs]) * gain_ref[:, cs]
        g = g_ref[:, cs].astype(F32)
        o_ref[:, cs] = (y * (g * jax.nn.sigmoid(g))).astype(BF16)

    lane = lax.broadcasted_iota(jnp.int32, (1, LANES), 1)
    lf = jnp.where(lane < FOX_HEADS, _log_sigmoid(small + bf_ref[...]), 0.0)
    l1, l2, l3 = _split3(lf)
    carry = fcar_ref[...]
    parts = []
    for j in range(tm // GLA_BLK):
        r = slice(j * GLA_BLK, (j + 1) * GLA_BLK)
        part = _dot(lower_b, l1[r]) + (_dot(lower_b, l2[r]) + _dot(lower_b, l3[r])) + carry
        carry = part[GLA_BLK - 1:GLA_BLK, :]
        parts.append(part)
    fl = jnp.concatenate(parts, axis=0)
    fcar_ref[...] = carry
    fs = fl * LOG2E
    fst = fs.T
    for h in range(FOX_HEADS):
        frow_ref[0, h] = fst[h:h + 1, :]
    n1, n2, n3 = _split3(-fs)
    kx = _dot(n1, sel_ref[0]) + (_dot(n2, sel_ref[1]) + _dot(n3, sel_ref[2])) + ones_ref[...]
    kx_ref[...] = kx.astype(BF16)


def _fox_bias_selectors():
    import numpy as np
    sel = np.zeros((3, LANES, FOX_WIDTH), np.float32)
    ones = np.zeros((1, FOX_WIDTH), np.float32)
    for h in range(FOX_HEADS):
        base = (h // 2) * LANES + (h % 2) * FOX_BIAS_LANES
        for i in range(3):
            sel[i, h, base + i] = 1.0
            ones[0, base + 3 + i] = 1.0
    return jnp.asarray(sel, BF16), jnp.asarray(ones, F32)


def _gla_call(proj, small, wg_hi, wg_lo, b_gk, gain, bf_pad, batch, seq):
    t = proj.shape[0]
    tm = GLA_TM * GLA_SUB
    nb = seq // tm

    def rows(b, i):
        return b * nb + i

    sel, ones = _fox_bias_selectors()
    return pl.pallas_call(
        _gla_kernel,
        grid=(batch, nb),
        in_specs=[
            pl.BlockSpec((tm, GLA_QK), lambda b, i: (rows(b, i), COL_QG // GLA_QK)),
            pl.BlockSpec((tm, GLA_QK), lambda b, i: (rows(b, i), COL_KG // GLA_QK)),
            pl.BlockSpec((tm, GLA_WIDTH), lambda b, i: (rows(b, i), COL_VG // GLA_WIDTH)),
            pl.BlockSpec((tm, GLA_WIDTH), lambda b, i: (rows(b, i), COL_GG // GLA_WIDTH)),
            pl.BlockSpec((tm, LANES), lambda b, i: (rows(b, i), 0)),
            pl.BlockSpec((LANES, GLA_QK), lambda b, i: (0, 0)),
            pl.BlockSpec((LANES, GLA_QK), lambda b, i: (0, 0)),
            pl.BlockSpec((1, GLA_QK), lambda b, i: (0, 0)),
            pl.BlockSpec((1, GLA_WIDTH), lambda b, i: (0, 0)),
            pl.BlockSpec((1, LANES), lambda b, i: (0, 0)),
            pl.BlockSpec((3, LANES, FOX_WIDTH), lambda b, i: (0, 0, 0)),
            pl.BlockSpec((1, FOX_WIDTH), lambda b, i: (0, 0)),
        ],
        out_specs=[
            pl.BlockSpec((tm, GLA_WIDTH), lambda b, i: (rows(b, i), 0)),
            pl.BlockSpec((tm, FOX_WIDTH), lambda b, i: (rows(b, i), 0)),
            pl.BlockSpec((1, FOX_HEADS, 1, tm), lambda b, i: (b, 0, 0, i)),
        ],
        out_shape=[
            jax.ShapeDtypeStruct((t, GLA_WIDTH), BF16),
            jax.ShapeDtypeStruct((t, FOX_WIDTH), BF16),
            jax.ShapeDtypeStruct((batch, FOX_HEADS, 1, seq), F32),
        ],
        scratch_shapes=[
            pltpu.VMEM((GLA_DV, GLA_QK), F32),
            pltpu.VMEM((1, LANES), F32),
            pltpu.VMEM((tm, GLA_WIDTH), F32),
        ],
        compiler_params=_params("parallel", "arbitrary"),
        name="gla",
    )(proj, proj, proj, proj, small, wg_hi, wg_lo, b_gk, gain, bf_pad, sel, ones)


FOX_TQ = 512
FOX_TK = 512
FOX_PAIRS = 2


def _fox_kernel(q_ref, k_ref, kx_ref, vt_ref, frow_ref, o_ref, m_ref, acc_ref, sa_ref, sb_ref,
                ma_ref, mb_ref):
    tq, tk = FOX_TQ, FOX_TK
    i = pl.program_id(2)
    q0 = pl.multiple_of(i * tq, tq)
    lane = lax.broadcasted_iota(jnp.int32, (1, LANES), 1)
    heads = range(2 * FOX_PAIRS)

    def pair_lanes(hh):
        return slice((hh // 2) * LANES, (hh // 2 + 1) * LANES)

    q_aug = []
    for hh in heads:
        head_lanes = (lane < FOX_DH) if hh % 2 == 0 else (lane >= FOX_DH)
        qm = jnp.where(head_lanes, q_ref[:, pair_lanes(hh)], 0.0).astype(BF16)
        fr = jnp.broadcast_to(frow_ref[0, hh, :, pl.ds(q0, LANES)][:, 0:1], (1, LANES))
        f1, f2, f3 = _split3(fr)
        base = (hh % 2) * FOX_BIAS_LANES
        qx = jnp.where(jnp.logical_and(lane >= base, lane < base + 3), 1.0, 0.0).astype(BF16)
        qx = jnp.where(lane == base + 3, f1, qx)
        qx = jnp.where(lane == base + 4, f2, qx)
        qx = jnp.where(lane == base + 5, f3, qx)
        q_aug.append(jnp.concatenate([qm, jnp.broadcast_to(qx, (tq, LANES))], axis=1))

    m_ref[...] = jnp.full(m_ref.shape, NEG, F32)
    acc_ref[...] = jnp.zeros_like(acc_ref)

    bufs = {"a": (sa_ref, ma_ref), "b": (sb_ref, mb_ref)}

    half = tk // 2
    lo, hi = slice(0, half), slice(half, tk)

    def k_aug(k0, n, hh):
        return jnp.concatenate([k_ref[pl.ds(k0, n), pair_lanes(hh)],
                                kx_ref[pl.ds(k0, n), pair_lanes(hh)]], axis=1)

    def causal(st):
        key = lax.broadcasted_iota(jnp.int32, st.shape, 0)
        qry = lax.broadcasted_iota(jnp.int32, st.shape, 1)
        return jnp.where(key <= qry, st, NEG)

    def produce(which, blk, hh, diagonal):
        buf, mx = bufs[which]
        k0 = pl.multiple_of(blk * tk, tk)
        if not diagonal:
            st = _dot_nt(k_aug(k0, tk, hh), q_aug[hh])
            buf[hh] = st
            mx[hh] = jnp.max(st, axis=0, keepdims=True)
            return
        st_lo = _dot_nt(k_aug(k0, half, hh), q_aug[hh])
        st_lo = jnp.concatenate([causal(st_lo[:, lo]), st_lo[:, hi]], axis=1)
        st_hi = causal(_dot_nt(k_aug(k0 + half, half, hh), q_aug[hh][hi]))
        buf[hh, lo, :] = st_lo
        buf[hh, hi, hi] = st_hi
        max_lo = jnp.max(st_lo, axis=0, keepdims=True)
        max_hi = jnp.max(st_hi, axis=0, keepdims=True)
        mx[hh] = jnp.concatenate([max_lo[:, lo], jnp.maximum(max_lo[:, hi], max_hi)], axis=1)

    def consume(which, blk, hh, diagonal):
        buf, mx = bufs[which]
        k0 = pl.multiple_of(blk * tk, tk)
        m_old = m_ref[hh]
        m_new = jnp.maximum(m_old, mx[hh])
        alpha = jnp.exp2(m_old - m_new)
        vrows = slice(hh * FOX_VROWS, (hh + 1) * FOX_VROWS)
        if not diagonal:
            p = jnp.exp2(buf[hh] - m_new).astype(BF16)
            acc_ref[hh] = alpha * acc_ref[hh] + _dot(vt_ref[vrows, pl.ds(k0, tk)], p)
        else:
            p_lo = jnp.exp2(buf[hh, lo, :] - m_new).astype(BF16)
            p_hi = jnp.exp2(buf[hh, hi, hi] - m_new[:, hi]).astype(BF16)
            pv_lo = _dot(vt_ref[vrows, pl.ds(k0, half)], p_lo)
            pv_hi = _dot(vt_ref[vrows, pl.ds(k0 + half, half)], p_hi)
            acc_ref[hh] = alpha * acc_ref[hh] + jnp.concatenate(
                [pv_lo[:, lo], pv_lo[:, hi] + pv_hi], axis=1)
        m_ref[hh] = m_new

    def stage(cur, cur_blk, nxt=None, nxt_blk=None, nxt_diagonal=False, cur_diagonal=False):
        for hh in heads:
            if nxt is not None:
                produce(nxt, nxt_blk, hh, nxt_diagonal)
            if hh > 0:
                consume(cur, cur_blk, hh - 1, cur_diagonal)
        consume(cur, cur_blk, heads[-1], cur_diagonal)

    @pl.when(i == 0)
    def _():
        for hh in heads:
            produce("a", 0, hh, True)
        stage("a", 0, cur_diagonal=True)

    @pl.when(i > 0)
    def _():
        for hh in heads:
            produce("a", 0, hh, False)

    def pair(t, carry):
        b0 = 2 * t
        stage("a", b0, "b", b0 + 1)
        stage("b", b0 + 1, "a", b0 + 2)
        return carry

    n_pairs = lax.shift_right_logical(jnp.maximum(i - 1, 0), 1)
    lax.fori_loop(0, n_pairs, pair, 0)

    @pl.when((i & 1) == 1)
    def _():
        stage("a", i - 1, "b", i, True)
        stage("b", i, cur_diagonal=True)

    @pl.when(jnp.logical_and((i & 1) == 0, i > 0))
    def _():
        stage("a", i - 2, "b", i - 1)
        stage("b", i - 1, "a", i, True)
        stage("a", i, cur_diagonal=True)

    for pr in range(FOX_PAIRS):
        outs = []
        for hh in (2 * pr, 2 * pr + 1):
            acc = acc_ref[hh]
            outs.append(acc[:FOX_DH] * (1.0 / acc[FOX_DH:FOX_DH + 1]))
        o_ref[:, pr * LANES:(pr + 1) * LANES] = jnp.concatenate(outs, axis=0).T.astype(BF16)


def _fox_call(proj, kx, vt, frow, batch, seq):
    t = proj.shape[0]
    tq = FOX_TQ
    nq = seq // tq
    width = FOX_PAIRS * LANES
    groups = FOX_WIDTH // width
    nh = 2 * FOX_PAIRS
    return pl.pallas_call(
        _fox_kernel,
        grid=(batch, groups, nq),
        in_specs=[
            pl.BlockSpec((tq, width), lambda b, hp, i: (b * nq + i, COL_QF // width + hp)),
            pl.BlockSpec((seq, width), lambda b, hp, i: (b, COL_KF // width + hp)),
            pl.BlockSpec((seq, width), lambda b, hp, i: (b, hp)),
            pl.BlockSpec((nh * FOX_VROWS, seq), lambda b, hp, i: (hp, b)),
            pl.BlockSpec((1, nh, 1, seq), lambda b, hp, i: (b, hp, 0, 0)),
        ],
        out_specs=pl.BlockSpec((tq, width), lambda b, hp, i: (b * nq + i, hp)),
        out_shape=jax.ShapeDtypeStruct((t, FOX_WIDTH), BF16),
        scratch_shapes=[
            pltpu.VMEM((nh, 1, tq), F32),
            pltpu.VMEM((nh, FOX_VROWS, tq), F32),
            pltpu.VMEM((nh, FOX_TK, tq), F32),
            pltpu.VMEM((nh, FOX_TK, tq), F32),
            pltpu.VMEM((nh, 1, tq), F32),
            pltpu.VMEM((nh, 1, tq), F32),
        ],
        compiler_params=_params("parallel", "parallel", "arbitrary"),
        name="fox",
    )(proj, proj, kx, vt, frow)


POST_TM = 512
POST_SUB = 2


def _post_kernel(og_ref, of_ref, x_ref, mod_ref, pg_ref, mg_ref, wo_ref, wrh_ref, wrl_ref, br_ref,
                 x1_ref, h2_ref, idx_ref, rank_ref, gate_ref, cnt_ref, car_ref):
    @pl.when(pl.program_id(0) == 0)
    def _():
        car_ref[...] = jnp.zeros_like(car_ref)

    ys = []
    for s in range(POST_SUB):
        rows = slice(s * POST_TM, (s + 1) * POST_TM)
        ys.append(_dot(og_ref[rows, :], wo_ref[:GLA_WIDTH, :])
                  + _dot(of_ref[rows, :], wo_ref[GLA_WIDTH:, :]))
    for s in range(POST_SUB):
        _post_route(s, ys[s], x_ref, mod_ref, pg_ref, mg_ref, wrh_ref, wrl_ref, br_ref,
                    x1_ref, h2_ref, idx_ref, rank_ref, gate_ref, cnt_ref, car_ref)


def _post_route(s, y, x_ref, mod_ref, pg_ref, mg_ref, wrh_ref, wrl_ref, br_ref,
                x1_ref, h2_ref, idx_ref, rank_ref, gate_ref, cnt_ref, car_ref):
    tm = POST_TM
    rows = slice(s * tm, (s + 1) * tm)
    x1 = x_ref[rows, :] + mod_ref[0, 2:3, :] * (_rms(y) * pg_ref[...])
    x1_ref[rows, :] = x1
    h2 = _rms(x1) * mg_ref[...]
    h2 = h2 * (1.0 + mod_ref[0, 4:5, :]) + mod_ref[0, 3:4, :]
    for j in range(ROW_TILES):
        h2_ref[pl.ds(s * tm * ROW_TILES + j, tm, stride=ROW_TILES), :] = h2[:, j * LANES:(j + 1) * LANES]

    hh, hl = _split2(h2)
    logits = _dot(hh, wrh_ref[...]) + (_dot(hh, wrl_ref[...]) + _dot(hl, wrh_ref[...])) + br_ref[...]
    lane = lax.broadcasted_iota(jnp.int32, (tm, LANES), 1).astype(F32)
    work = jnp.where(lane < N_EXPERTS, logits, -jnp.inf)
    vals, idxs = [], []
    for _ in range(TOP_K):
        mx = jnp.max(work, axis=-1, keepdims=True)
        ix = jnp.min(jnp.where(work == mx, lane, float(LANES)), axis=-1, keepdims=True)
        vals.append(mx)
        idxs.append(ix)
        work = jnp.where(lane == ix, -jnp.inf, work)
    es = [jnp.exp(v - vals[0]) for v in vals]
    den = es[0] + es[1] + es[2] + es[3]

    onehot = jnp.zeros((tm, LANES), F32)
    for ix in idxs:
        onehot = onehot + jnp.where(lane == ix, 1.0, 0.0)
    row = lax.broadcasted_iota(jnp.int32, (tm, tm), 0)
    col = lax.broadcasted_iota(jnp.int32, (tm, tm), 1)
    strict_b = jnp.where(col < row, 1.0, 0.0).astype(BF16)
    prefix = _dot(strict_b, onehot.astype(BF16)) + car_ref[...]
    car_new = car_ref[...] + jnp.sum(onehot, axis=0, keepdims=True)
    car_ref[...] = car_new
    cnt_ref[...] = car_new

    idx_o = jnp.zeros((tm, LANES), F32)
    rank_o = jnp.zeros((tm, LANES), F32)
    gate_o = jnp.zeros((tm, LANES), F32)
    for kk in range(TOP_K):
        rk = jnp.sum(jnp.where(lane == idxs[kk], prefix, 0.0), axis=-1, keepdims=True)
        sel = lane == float(kk)
        idx_o = jnp.where(sel, idxs[kk], idx_o)
        rank_o = jnp.where(sel, rk, rank_o)
        gate_o = jnp.where(sel, es[kk] / den, gate_o)
    idx_ref[rows, :] = idx_o.astype(jnp.int32)
    rank_ref[rows, :] = rank_o.astype(jnp.int32)
    gate_ref[rows, :] = gate_o


def _post_call(o_gla, o_fox, x2, mod3, post_gain, mlp_gain, w_o, wr_hi, wr_lo, br_pad, seq):
    t, d = x2.shape
    tm = POST_TM * POST_SUB
    per_b = seq // tm
    return pl.pallas_call(
        _post_kernel,
        grid=(t // tm,),
        in_specs=[
            pl.BlockSpec((tm, GLA_WIDTH), lambda i: (i, 0)),
            pl.BlockSpec((tm, FOX_WIDTH), lambda i: (i, 0)),
            pl.BlockSpec((tm, d), lambda i: (i, 0)),
            pl.BlockSpec((1, N_MOD, d), lambda i: (i // per_b, 0, 0)),
            pl.BlockSpec((1, d), lambda i: (0, 0)),
            pl.BlockSpec((1, d), lambda i: (0, 0)),
            pl.BlockSpec((GLA_WIDTH + FOX_WIDTH, d), lambda i: (0, 0)),
            pl.BlockSpec((d, LANES), lambda i: (0, 0)),
            pl.BlockSpec((d, LANES), lambda i: (0, 0)),
            pl.BlockSpec((1, LANES), lambda i: (0, 0)),
        ],
        out_specs=[
            pl.BlockSpec((tm, d), lambda i: (i, 0)),
            pl.BlockSpec((tm * ROW_TILES, LANES), lambda i: (i, 0)),
            pl.BlockSpec((tm, LANES), lambda i: (i, 0)),
            pl.BlockSpec((tm, LANES), lambda i: (i, 0)),
            pl.BlockSpec((tm, LANES), lambda i: (i, 0)),
            pl.BlockSpec((1, LANES), lambda i: (0, 0)),
        ],
        out_shape=[
            jax.ShapeDtypeStruct((t, d), F32),
            jax.ShapeDtypeStruct((t * ROW_TILES, LANES), F32),
            jax.ShapeDtypeStruct((t, LANES), jnp.int32),
            jax.ShapeDtypeStruct((t, LANES), jnp.int32),
            jax.ShapeDtypeStruct((t, LANES), F32),
            jax.ShapeDtypeStruct((1, LANES), F32),
        ],
        scratch_shapes=[pltpu.VMEM((1, LANES), F32)],
        compiler_params=_params("arbitrary"),
        name="post",
    )(o_gla, o_fox, x2, mod3, post_gain, mlp_gain, w_o, wr_hi, wr_lo, br_pad)


DISP_TM = 512
DISP_SLOTS = 3
ISSUE_UNROLL = 8


def _row_copy(src_ref, src_row, dst_ref, dst_row, sem):
    return pltpu.make_async_copy(
        src_ref.at[pl.ds(pl.multiple_of(src_row * ROW_TILES, ROW_TILES), ROW_TILES)],
        dst_ref.at[pl.ds(pl.multiple_of(dst_row * ROW_TILES, ROW_TILES), ROW_TILES)],
        sem)


def _dispatch_kernel(pend_ref, pos_ref, h_hbm, xs_ref, zero_ref, hbuf_ref, sem, in_sem, out_sem):
    tm = DISP_TM
    rows = tm * ROW_TILES
    zrows = MOE_TM * ROW_TILES
    i = pl.program_id(0)
    last = pl.num_programs(0) - 1
    slot = lax.rem(i, DISP_SLOTS)
    nxt = lax.rem(i + 1, DISP_SLOTS)

    def load(step, s):
        return pltpu.make_async_copy(
            h_hbm.at[pl.ds(pl.multiple_of(step * rows, rows), rows)], hbuf_ref.at[s], in_sem.at[s])

    def drain(s):
        for _ in range(TOP_K):
            pltpu.make_async_copy(hbuf_ref.at[s], xs_ref.at[pl.ds(0, rows)], out_sem.at[s]).wait()

    @pl.when(i == 0)
    def _():
        load(0, 0).start()
        zero_ref[...] = jnp.zeros_like(zero_ref)

        def last_tile(e):
            start = pl.multiple_of((pend_ref[e] - MOE_TM) * ROW_TILES, zrows)
            return pltpu.make_async_copy(zero_ref, xs_ref.at[pl.ds(start, zrows)], sem)

        def nonempty(e):
            return pend_ref[e] > jnp.where(e == 0, 0, pend_ref[jnp.maximum(e - 1, 0)])

        def clear(e, carry):
            @pl.when(nonempty(e))
            def _():
                last_tile(e).start()
            return carry

        def clear_wait(e, carry):
            @pl.when(nonempty(e))
            def _():
                last_tile(e).wait()
            return carry

        lax.fori_loop(0, N_EXPERTS, clear, 0)
        lax.fori_loop(0, N_EXPERTS, clear_wait, 0)

        def tail_tile(j):
            return pltpu.make_async_copy(
                zero_ref, xs_ref.at[pl.ds(pl.multiple_of(j * zrows, zrows), zrows)], sem)

        def tail(j, carry):
            tail_tile(j).start()
            return carry

        def tail_wait(j, carry):
            tail_tile(j).wait()
            return carry

        first_unused = lax.div(pend_ref[N_EXPERTS - 1], MOE_TM)
        n_tiles = xs_ref.shape[0] // zrows
        lax.fori_loop(first_unused, n_tiles, tail, 0)
        lax.fori_loop(first_unused, n_tiles, tail_wait, 0)

    @pl.when(i >= DISP_SLOTS - 1)
    def _():
        drain(nxt)

    @pl.when(i < last)
    def _():
        load(i + 1, nxt).start()

    load(i, slot).wait()

    def issue(g, carry):
        for u in range(ISSUE_UNROLL):
            t = g * ISSUE_UNROLL + u
            for kk in range(TOP_K):
                _row_copy(hbuf_ref.at[slot], t, xs_ref, pos_ref[0, 0, t * TOP_K + kk],
                          out_sem.at[slot]).start(priority=kk % 2)
        return carry

    lax.fori_loop(0, tm // ISSUE_UNROLL, issue, 0)

    @pl.when(i == last)
    def _():
        for back in range(DISP_SLOTS - 1):
            @pl.when(i >= back)
            def _():
                drain(lax.rem(i - back + DISP_SLOTS, DISP_SLOTS))


def _dispatch_call(pend, pos3, h2, n_tiles):
    tm = DISP_TM
    n_steps = pos3.shape[0]
    grid_spec = pltpu.PrefetchScalarGridSpec(
        num_scalar_prefetch=1,
        grid=(n_steps,),
        in_specs=[
            pl.BlockSpec((1, 1, tm * TOP_K), lambda i, pend: (i, 0, 0), memory_space=pltpu.SMEM),
            pl.BlockSpec(memory_space=pl.ANY),
        ],
        out_specs=pl.BlockSpec(memory_space=pl.ANY),
        scratch_shapes=[
            pltpu.VMEM((MOE_TM * ROW_TILES, LANES), F32),
            pltpu.VMEM((DISP_SLOTS, tm * ROW_TILES, LANES), F32),
            pltpu.SemaphoreType.DMA,
            pltpu.SemaphoreType.DMA((DISP_SLOTS,)),
            pltpu.SemaphoreType.DMA((DISP_SLOTS,)),
        ],
    )
    return pl.pallas_call(
        _dispatch_kernel,
        grid_spec=grid_spec,
        out_shape=jax.ShapeDtypeStruct((n_tiles * MOE_TM * ROW_TILES, LANES), F32),
        compiler_params=_params("arbitrary"),
        name="dispatch",
    )(pend, pos3, h2)


MOE_TM = 512


def _experts_kernel(te_ref, nu_ref, x_ref, win_ref, bin_ref, wout_ref, bout_ref, y_ref,
                    wib_ref, wob_ref):
    tm = MOE_TM
    i = pl.program_id(0)
    used = i < nu_ref[0]

    @pl.when(jnp.logical_not(used))
    def _():
        y_ref[...] = jnp.zeros_like(y_ref)

    fresh = jnp.logical_or(i == 0, te_ref[i] != te_ref[jnp.maximum(i - 1, 0)])

    @pl.when(jnp.logical_and(used, fresh))
    def _():
        wib_ref[...] = win_ref[0].astype(BF16)
        wob_ref[...] = wout_ref[0].astype(BF16)

    @pl.when(used)
    def _():
        x = jnp.concatenate([x_ref[pl.ds(j, tm, stride=ROW_TILES), :] for j in range(ROW_TILES)],
                            axis=1).astype(BF16)
        u = _dot(x, wib_ref[...]) + bin_ref[0]
        glu = jnp.minimum(u[:, :D_FF], SWIGLU_LIMIT)
        lin = jnp.clip(u[:, D_FF:], -SWIGLU_LIMIT, SWIGLU_LIMIT)
        a = glu * jax.nn.sigmoid(SWIGLU_ALPHA * glu) * (lin + 1.0)
        y = _dot(a.astype(BF16), wob_ref[...]) + bout_ref[0]
        for j in range(ROW_TILES):
            y_ref[pl.ds(j, tm, stride=ROW_TILES), :] = y[:, j * LANES:(j + 1) * LANES]


def _experts_call(tile_expert, n_used, xs, w_in, b_in, w_out, b_out):
    tm = MOE_TM
    n_tiles = tile_expert.shape[0]
    d = D_MODEL

    def tile(i, te, nu):
        return (jnp.minimum(i, nu[0] - 1), 0)

    grid_spec = pltpu.PrefetchScalarGridSpec(
        num_scalar_prefetch=2,
        grid=(n_tiles,),
        in_specs=[
            pl.BlockSpec((tm * ROW_TILES, LANES), tile),
            pl.BlockSpec((1, d, 2 * D_FF), lambda i, te, nu: (te[i], 0, 0)),
            pl.BlockSpec((1, 1, 2 * D_FF), lambda i, te, nu: (te[i], 0, 0)),
            pl.BlockSpec((1, D_FF, d), lambda i, te, nu: (te[i], 0, 0)),
            pl.BlockSpec((1, 1, d), lambda i, te, nu: (te[i], 0, 0)),
        ],
        out_specs=pl.BlockSpec((tm * ROW_TILES, LANES), lambda i, te, nu: (i, 0)),
        scratch_shapes=[
            pltpu.VMEM((d, 2 * D_FF), BF16),
            pltpu.VMEM((D_FF, d), BF16),
        ],
    )
    return pl.pallas_call(
        _experts_kernel,
        grid_spec=grid_spec,
        out_shape=jax.ShapeDtypeStruct(xs.shape, F32),
        compiler_params=_params("arbitrary"),
        name="experts",
    )(tile_expert, n_used, xs, w_in, b_in, w_out, b_out)


COMB_TM = 256


def _combine_kernel(pos_ref, posn_ref, y_ref, gate_ref, x1_ref, mod_ref, pg_ref, o_ref, buf_ref, sem):
    tm = COMB_TM
    i = pl.program_id(0)
    slot = i & 1

    def gather(p_ref, s):
        def issue(g, carry):
            for u in range(ISSUE_UNROLL):
                t = g * ISSUE_UNROLL + u
                for kk in range(TOP_K):
                    _row_copy(y_ref, p_ref[0, 0, t * TOP_K + kk], buf_ref.at[s, kk], t,
                              sem.at[s]).start(priority=kk % 2)
            return carry

        lax.fori_loop(0, tm // ISSUE_UNROLL, issue, 0)

    @pl.when(i == 0)
    def _():
        gather(pos_ref, 0)

    @pl.when(i + 1 < pl.num_programs(0))
    def _():
        gather(posn_ref, 1 - slot)

    for kk in range(TOP_K):
        pltpu.make_async_copy(y_ref.at[pl.ds(0, tm * ROW_TILES)], buf_ref.at[slot, kk],
                              sem.at[slot]).wait()

    gates = gate_ref[...]
    acc = None
    for kk in range(TOP_K):
        rows = jnp.concatenate(
            [buf_ref[slot, kk, pl.ds(j, tm, stride=ROW_TILES), :] for j in range(ROW_TILES)], axis=1)
        term = rows * gates[:, kk:kk + 1]
        acc = term if acc is None else acc + term
    o_ref[...] = x1_ref[...] + mod_ref[0, 5:6, :] * (_rms(acc) * pg_ref[...])


def _combine_call(pos3, y, gates, x1, mod3, post_gain, seq):
    t, d = x1.shape
    tm = COMB_TM
    per_b = seq // tm
    n_steps = t // tm
    return pl.pallas_call(
        _combine_kernel,
        grid=(n_steps,),
        in_specs=[
            pl.BlockSpec((1, 1, tm * TOP_K), lambda i: (i, 0, 0), memory_space=pltpu.SMEM),
            pl.BlockSpec((1, 1, tm * TOP_K), lambda i: (jnp.minimum(i + 1, n_steps - 1), 0, 0),
                         memory_space=pltpu.SMEM),
            pl.BlockSpec(memory_space=pl.ANY),
            pl.BlockSpec((tm, LANES), lambda i: (i, 0)),
            pl.BlockSpec((tm, d), lambda i: (i, 0)),
            pl.BlockSpec((1, N_MOD, d), lambda i: (i // per_b, 0, 0)),
            pl.BlockSpec((1, d), lambda i: (0, 0)),
        ],
        out_specs=pl.BlockSpec((tm, d), lambda i: (i, 0)),
        out_shape=jax.ShapeDtypeStruct((t, d), F32),
        scratch_shapes=[
            pltpu.VMEM((2, TOP_K, tm * ROW_TILES, LANES), F32),
            pltpu.SemaphoreType.DMA((2,)),
        ],
        compiler_params=_params("arbitrary"),
        name="combine",
    )(pos3, pos3, y, gates, x1, mod3, post_gain)


def _pad_cols(w, n):
    return jnp.pad(w, ((0, 0), (0, n - w.shape[1])))


def kernel(x, c, w_ada, b_ada, attn_pre_gain, attn_post_gain, w_in, w_gk2, b_gk, gla_norm_gain, b_f,
           w_o, mlp_pre_gain, mlp_post_gain, w_router, b_router, w_e_in, b_e_in, w_e_out, b_e_out):
    batch, seq, d = x.shape
    t = batch * seq
    x2 = x.reshape(t, d)

    o = 0
    cols = {}
    for name, width in (("qg", GLA_QK), ("kg", GLA_QK), ("vg", GLA_WIDTH), ("gg", GLA_WIDTH),
                        ("zg", GLA_LOWRANK), ("qf", FOX_WIDTH), ("kf", FOX_WIDTH), ("vf", FOX_WIDTH),
                        ("ff", FOX_HEADS)):
        cols[name] = w_in[:, o:o + width]
        o += width
    gla_cols = 2 * GLA_QK + 2 * GLA_WIDTH
    fox_start = gla_cols + GLA_LOWRANK
    w_main = jnp.concatenate([w_in[:, :gla_cols], w_in[:, fox_start:fox_start + 2 * FOX_WIDTH]],
                             axis=1).astype(BF16)
    w_vt = cols["vf"].T.astype(BF16)
    w_small = _pad_cols(jnp.concatenate([cols["ff"], cols["zg"]], axis=1), LANES)
    ws_hi, ws_lo = _split2(w_small)
    wg = jnp.zeros((LANES, GLA_QK), F32).at[SMALL_ZG:SMALL_ZG + GLA_LOWRANK].set(w_gk2)
    wg_hi, wg_lo = _split2(wg)
    bf_pad = jnp.zeros((1, LANES), F32).at[0, SMALL_FF:SMALL_FF + FOX_HEADS].set(b_f)
    wr_hi, wr_lo = _split2(_pad_cols(w_router, LANES))
    br_pad = _pad_cols(b_router.reshape(1, N_EXPERTS), LANES)

    c_pad = jnp.pad(c, ((0, SUBLANES - batch % SUBLANES if batch % SUBLANES else 0), (0, 0)))
    mod = _mod_call(c_pad, w_ada, b_ada.reshape(1, -1))
    mod3 = mod[:batch].reshape(batch, N_MOD, d)

    proj, vt, small = _inproj_call(x2, mod3, attn_pre_gain.reshape(1, d), w_main, w_vt, ws_hi, ws_lo,
                                   seq)
    o_gla, kx, frow = _gla_call(proj, small, wg_hi, wg_lo, b_gk.reshape(1, -1),
                                gla_norm_gain.reshape(1, -1), bf_pad, batch, seq)
    o_fox = _fox_call(proj, kx, vt, frow, batch, seq)

    x1, h2, idx_w, rank_w, gate_w, cnt = _post_call(
        o_gla, o_fox, x2, mod3, attn_post_gain.reshape(1, d), mlp_pre_gain.reshape(1, d),
        w_o.astype(BF16), wr_hi, wr_lo, br_pad, seq)

    tm = MOE_TM
    n_tiles = (t * TOP_K) // tm + N_EXPERTS
    counts = cnt[0, :N_EXPERTS].astype(jnp.int32)
    padded = ((counts + tm - 1) // tm) * tm
    pend = jnp.cumsum(padded)
    pstart = pend - padded
    experts = jnp.arange(N_EXPERTS, dtype=jnp.int32)
    first_slot = jnp.sum(jnp.where(idx_w[:, :TOP_K, None] == experts, pstart.astype(jnp.int32), 0),
                         axis=-1)
    pos = first_slot + rank_w[:, :TOP_K]
    n_used = (pend[-1] // tm).astype(jnp.int32).reshape(1)
    tile_start = jnp.arange(n_tiles, dtype=jnp.int32) * tm
    tile_expert = jnp.minimum(
        jnp.sum((pend[None, :] <= tile_start[:, None]).astype(jnp.int32), axis=1), N_EXPERTS - 1)

    pos_d = pos.reshape(t // DISP_TM, 1, DISP_TM * TOP_K)
    xs = _dispatch_call(pend.astype(jnp.int32), pos_d, h2, n_tiles)
    y = _experts_call(tile_expert, n_used, xs, w_e_in, b_e_in.reshape(N_EXPERTS, 1, -1), w_e_out,
                      b_e_out.reshape(N_EXPERTS, 1, -1))
    pos_c = pos.reshape(t // COMB_TM, 1, COMB_TM * TOP_K)
    out = _combine_call(pos_c, y, gate_w, x1, mod3, mlp_post_gain.reshape(1, d), seq)
    return out.reshape(batch, seq, d)
```

```python
import jax
import jax.numpy as jnp
from jax import lax
from jax.experimental import pallas as pl
from jax.experimental.pallas import tpu as pltpu

F32 = jnp.float32
BF16 = jnp.bfloat16

D_MODEL = 1024
N_MOD = 6
GLA_HEADS = 4
GLA_DK = 64
GLA_DV = 128
GLA_QK = GLA_HEADS * GLA_DK
GLA_WIDTH = GLA_HEADS * GLA_DV
GLA_LOWRANK = 16
GLA_CHUNK = 64
GLA_GATE_NORM = 16.0
FOX_HEADS = 8
FOX_DH = 64
FOX_WIDTH = FOX_HEADS * FOX_DH
N_EXPERTS = 32
TOP_K = 4
D_FF = D_MODEL
SWIGLU_ALPHA = 1.702
SWIGLU_LIMIT = 7.0
EPS = 1e-6

LANES = 128
SUBLANES = 8
ROW_TILES = D_MODEL // LANES

COL_QG, COL_KG, COL_VG, COL_GG = 0, 256, 512, 1024
COL_QF, COL_KF = 1536, 2048
MAIN_COLS = 2560
LOG2E = 1.4426950408889634
FOX_Q_SCALE = (FOX_DH ** -0.5) * LOG2E
FOX_BIAS_LANES = 6
BF16_SUBLANES = 16
FOX_VROWS = FOX_DH + BF16_SUBLANES
SMALL_FF, SMALL_ZG = 0, 8

NEG = -1e30
VMEM_LIMIT = 56 * 1024 * 1024


def _dot(a, b):
    return jnp.dot(a, b, preferred_element_type=F32)


def _dot_nt(a, b):
    return lax.dot_general(a, b, (((1,), (1,)), ((), ())), preferred_element_type=F32)


def _dot_tn(a, b):
    return lax.dot_general(a, b, (((0,), (0,)), ((), ())), preferred_element_type=F32)


def _split2(a):
    hi = a.astype(BF16)
    lo = (a - hi.astype(F32)).astype(BF16)
    return hi, lo


def _split3(a):
    hi = a.astype(BF16)
    r = a - hi.astype(F32)
    mid = r.astype(BF16)
    lo = (r - mid.astype(F32)).astype(BF16)
    return hi, mid, lo


def _log_sigmoid(x):
    return jnp.minimum(x, 0.0) - jnp.log1p(jnp.exp(-jnp.abs(x)))


def _rms(x):
    return x * lax.rsqrt(jnp.mean(x * x, axis=-1, keepdims=True) + EPS)


def _params(*sem):
    return pltpu.CompilerParams(dimension_semantics=sem, vmem_limit_bytes=VMEM_LIMIT)


def _mod_kernel(c_ref, w_ref, b_ref, o_ref):
    c = c_ref[...]
    s = c * jax.nn.sigmoid(c)
    sh, sl = _split2(s)
    wh, wl = _split2(w_ref[...])
    o_ref[...] = _dot(sh, wh) + (_dot(sh, wl) + _dot(sl, wh)) + b_ref[...]


def _mod_call(c_pad, w_ada, b_ada):
    rows, d = c_pad.shape
    n = w_ada.shape[1]
    tn = D_MODEL
    return pl.pallas_call(
        _mod_kernel,
        grid=(n // tn,),
        in_specs=[
            pl.BlockSpec((rows, d), lambda j: (0, 0)),
            pl.BlockSpec((d, tn), lambda j: (0, j)),
            pl.BlockSpec((1, tn), lambda j: (0, j)),
        ],
        out_specs=pl.BlockSpec((rows, tn), lambda j: (0, j)),
        out_shape=jax.ShapeDtypeStruct((rows, n), F32),
        compiler_params=_params("parallel"),
        name="mod",
    )(c_pad, w_ada, b_ada)


INPROJ_TM = 512
INPROJ_SUB = 2


def _inproj_kernel(x_ref, mod_ref, g_ref, w_ref, wvt_ref, wsh_ref, wsl_ref, proj_ref, vt_ref, small_ref):
    subs = [pl.ds(s * INPROJ_TM, INPROJ_TM) for s in range(INPROJ_SUB)]
    hs = []
    for r in subs:
        h = _rms(x_ref[r, :]) * g_ref[...]
        h = h * (1.0 + mod_ref[0, 1:2, :]) + mod_ref[0, 0:1, :]
        hs.append(_split2(h))
    for r, (hh, hl) in zip(subs, hs):
        _inproj_sub(hh, hl, w_ref, wvt_ref, wsh_ref, wsl_ref, proj_ref.at[r], vt_ref.at[:, r],
                    small_ref.at[r])


def _inproj_sub(hh, hl, w_ref, wvt_ref, wsh_ref, wsl_ref, proj_ref, vt_ref, small_ref):
    cw = FOX_WIDTH
    for j in range(MAIN_COLS // cw):
        r = _dot(hh, w_ref[:, j * cw:(j + 1) * cw])
        if j * cw == COL_QF:
            r = r * FOX_Q_SCALE
        proj_ref[:, j * cw:(j + 1) * cw] = r.astype(BF16)
    vt = _dot_nt(wvt_ref[...], hh).astype(BF16)
    ones = jnp.ones((FOX_VROWS - FOX_DH, vt.shape[1]), BF16)
    for h in range(FOX_HEADS):
        vt_ref[h * FOX_VROWS:h * FOX_VROWS + FOX_DH, :] = vt[h * FOX_DH:(h + 1) * FOX_DH, :]
        vt_ref[h * FOX_VROWS + FOX_DH:(h + 1) * FOX_VROWS, :] = ones
    a = _dot(hh, wsl_ref[...])
    small_ref[...] = a[:, :LANES] + (a[:, LANES:] + _dot(hl, wsh_ref[...]))


def _inproj_call(x2, mod3, gain, w_main, w_vt, ws_hi, ws_lo, seq):
    ws_hilo = jnp.concatenate([ws_hi, ws_lo], axis=1)
    t, d = x2.shape
    tm = INPROJ_TM * INPROJ_SUB
    per_b = seq // tm
    return pl.pallas_call(
        _inproj_kernel,
        grid=(t // tm,),
        in_specs=[
            pl.BlockSpec((tm, d), lambda i: (i, 0)),
            pl.BlockSpec((1, N_MOD, d), lambda i: (i // per_b, 0, 0)),
            pl.BlockSpec((1, d), lambda i: (0, 0)),
            pl.BlockSpec((d, MAIN_COLS), lambda i: (0, 0)),
            pl.BlockSpec((FOX_WIDTH, d), lambda i: (0, 0)),
            pl.BlockSpec((d, LANES), lambda i: (0, 0)),
            pl.BlockSpec((d, 2 * LANES), lambda i: (0, 0)),
        ],
        out_specs=[
            pl.BlockSpec((tm, MAIN_COLS), lambda i: (i, 0)),
            pl.BlockSpec((FOX_HEADS * FOX_VROWS, tm), lambda i: (0, i)),
            pl.BlockSpec((tm, LANES), lambda i: (i, 0)),
        ],
        out_shape=[
            jax.ShapeDtypeStruct((t, MAIN_COLS), BF16),
            jax.ShapeDtypeStruct((FOX_HEADS * FOX_VROWS, t), BF16),
            jax.ShapeDtypeStruct((t, LANES), F32),
        ],
        compiler_params=_params("parallel"),
        name="inproj",
    )(x2, mod3, gain, w_main, w_vt, ws_hi, ws_hilo)


GLA_TM = 512
GLA_SUB = 2
GLA_BLK = 256


def _gla_kernel(q_ref, k_ref, v_ref, g_ref, small_ref, wgh_ref, wgl_ref, bgk_ref, gain_ref, bf_ref,
                sel_ref, ones_ref, o_ref, kx_ref, frow_ref, st_ref, fcar_ref, oacc_ref):
    @pl.when(pl.program_id(1) == 0)
    def _():
        st_ref[...] = jnp.zeros_like(st_ref)
        fcar_ref[...] = jnp.zeros_like(fcar_ref)

    subs = [pl.ds(s * GLA_TM, GLA_TM) for s in range(GLA_SUB)]
    fronts = [_gla_sub_front(q_ref.at[r], k_ref.at[r], v_ref.at[r], small_ref.at[r],
                             wgh_ref, wgl_ref, bgk_ref, oacc_ref.at[r]) for r in subs]
    for r, front in zip(subs, fronts):
        _gla_sub_chain(front, v_ref.at[r], g_ref.at[r], gain_ref, bf_ref, sel_ref, ones_ref,
                       o_ref.at[r], kx_ref.at[r], frow_ref.at[:, :, :, r], st_ref, fcar_ref,
                       oacc_ref.at[r])


def _gla_sub_front(q_ref, k_ref, v_ref, small_ref, wgh_ref, wgl_ref, bgk_ref, oacc_ref):
    tm = GLA_TM
    nc = tm // GLA_CHUNK
    small = small_ref[...]
    sh, sl = _split2(small)
    gk = _dot(sh, wgh_ref[...]) + (_dot(sh, wgl_ref[...]) + _dot(sl, wgh_ref[...])) + bgk_ref[...]
    log_a = _log_sigmoid(gk) * (1.0 / GLA_GATE_NORM)

    blk = GLA_BLK
    blocks = [slice(j * blk, (j + 1) * blk) for j in range(tm // blk)]
    row = lax.broadcasted_iota(jnp.int32, (blk, blk), 0)
    col = lax.broadcasted_iota(jnp.int32, (blk, blk), 1)
    chunk_shift = GLA_CHUNK.bit_length() - 1
    same = (row >> chunk_shift) == (col >> chunk_shift)
    lower = col <= row
    tri = jnp.logical_and(same, lower)
    tri_b = jnp.where(tri, 1.0, 0.0).astype(BF16)
    lower_b = jnp.where(lower, 1.0, 0.0).astype(BF16)

    lh, ll = _split2(log_a)
    b = jnp.concatenate([_dot(tri_b, lh[r]) + _dot(tri_b, ll[r]) for r in blocks], axis=0)
    btot = jnp.concatenate(
        [jnp.broadcast_to(b[(c + 1) * GLA_CHUNK - 1:(c + 1) * GLA_CHUNK, :], (GLA_CHUNK, GLA_QK))
         for c in range(nc)], axis=0)

    q = q_ref[...].astype(F32)
    k = k_ref[...].astype(F32)
    qt = q * jnp.exp(b) * (GLA_DK ** -0.5)
    kt = (k * jnp.exp(-b)).astype(BF16)
    kd = (k * jnp.exp(btot - b)).astype(BF16)

    lane_qk = lax.broadcasted_iota(jnp.int32, (1, GLA_QK), 1)
    head_masks = [jnp.logical_and(lane_qk >= h * GLA_DK, lane_qk < (h + 1) * GLA_DK)
                  for h in range(GLA_HEADS)]

    for h in range(GLA_HEADS):
        cs = slice(h * GLA_DV, (h + 1) * GLA_DV)
        qm = jnp.where(head_masks[h], qt, 0.0).astype(BF16)
        for r in blocks:
            att = jnp.where(tri, _dot_nt(qm[r], kt[r]), 0.0).astype(BF16)
            oacc_ref[r, cs] = _dot(att, v_ref[r, cs])
    return small, qt, kd, btot, head_masks, lower_b


def _gla_sub_chain(front, v_ref, g_ref, gain_ref, bf_ref, sel_ref, ones_ref, o_ref, kx_ref, frow_ref,
                   st_ref, fcar_ref, oacc_ref):
    tm = GLA_TM
    nc = tm // GLA_CHUNK
    small, qt, kd, btot, head_masks, lower_b = front
    _fox_gates(small, lower_b, bf_ref, sel_ref, ones_ref, kx_ref, frow_ref, fcar_ref)

    chunks = [slice(c * GLA_CHUNK, (c + 1) * GLA_CHUNK) for c in range(nc)]
    kvts = [_dot_tn(v_ref[rs, :], kd[rs]) for rs in chunks]
    states = []
    st = st_ref[...]
    for c, rs in enumerate(chunks):
        states.append(st.astype(BF16))
        decay = jnp.exp(btot[c * GLA_CHUNK:c * GLA_CHUNK + 1, :])
        new = decay * st
        for h in range(GLA_HEADS):
            new = new + jnp.where(head_masks[h], kvts[c][h * GLA_DV:(h + 1) * GLA_DV, :], 0.0)
        st = new
    st_ref[...] = st
    for c, rs in enumerate(chunks):
        qc = qt[rs]
        qs = jnp.concatenate([jnp.where(head_masks[h], qc, 0.0) for h in range(GLA_HEADS)],
                             axis=0).astype(BF16)
        oi = _dot_nt(qs, states[c])
        for h in range(GLA_HEADS):
            oacc_ref[rs, h * GLA_DV:(h + 1) * GLA_DV] += oi[h * GLA_CHUNK:(h + 1) * GLA_CHUNK, :]

    for h in range(GLA_HEADS):
        cs = slice(h * GLA_DV, (h + 1) * GLA_DV)
        y = _rms(oacc_ref[:, cs]) * gain_ref[:, cs]
        g = g_ref[:, cs].astype(F32)
        o_ref[:, cs] = (y * (g * jax.nn.sigmoid(g))).astype(BF16)


def _fox_gates(small, lower_b, bf_ref, sel_ref, ones_ref, kx_ref, frow_ref, fcar_ref):
    tm = GLA_TM
    lane = lax.broadcasted_iota(jnp.int32, (1, LANES), 1)
    lf = jnp.where(lane < FOX_HEADS, _log_sigmoid(small + bf_ref[...]), 0.0)
    l1, l2, l3 = _split3(lf)
    carry = fcar_ref[...]
    parts = []
    for j in range(tm // GLA_BLK):
        r = slice(j * GLA_BLK, (j + 1) * GLA_BLK)
        part = _dot(lower_b, l1[r]) + (_dot(lower_b, l2[r]) + _dot(lower_b, l3[r])) + carry
        carry = part[GLA_BLK - 1:GLA_BLK, :]
        parts.append(part)
    fl = jnp.concatenate(parts, axis=0)
    fcar_ref[...] = carry
    fs = fl * LOG2E
    fst = fs.T
    for h in range(FOX_HEADS):
        frow_ref[0, h] = fst[h:h + 1, :]
    n1, n2, n3 = _split3(-fs)
    kx = _dot(n1, sel_ref[0]) + (_dot(n2, sel_ref[1]) + _dot(n3, sel_ref[2])) + ones_ref[...]
    kx_ref[...] = kx.astype(BF16)


def _fox_bias_selectors():
    import numpy as np
    sel = np.zeros((3, LANES, FOX_WIDTH), np.float32)
    ones = np.zeros((1, FOX_WIDTH), np.float32)
    for h in range(FOX_HEADS):
        base = (h // 2) * LANES + (h % 2) * FOX_BIAS_LANES
        for i in range(3):
            sel[i, h, base + i] = 1.0
            ones[0, base + 3 + i] = 1.0
    return jnp.asarray(sel, BF16), jnp.asarray(ones, F32)


def _gla_call(proj, small, wg_hi, wg_lo, b_gk, gain, bf_pad, batch, seq):
    t = proj.shape[0]
    tm = GLA_TM * GLA_SUB
    nb = seq // tm

    def rows(b, i):
        return b * nb + i

    sel, ones = _fox_bias_selectors()
    return pl.pallas_call(
        _gla_kernel,
        grid=(batch, nb),
        in_specs=[
            pl.BlockSpec((tm, GLA_QK), lambda b, i: (rows(b, i), COL_QG // GLA_QK)),
            pl.BlockSpec((tm, GLA_QK), lambda b, i: (rows(b, i), COL_KG // GLA_QK)),
            pl.BlockSpec((tm, GLA_WIDTH), lambda b, i: (rows(b, i), COL_VG // GLA_WIDTH)),
            pl.BlockSpec((tm, GLA_WIDTH), lambda b, i: (rows(b, i), COL_GG // GLA_WIDTH)),
            pl.BlockSpec((tm, LANES), lambda b, i: (rows(b, i), 0)),
            pl.BlockSpec((LANES, GLA_QK), lambda b, i: (0, 0)),
            pl.BlockSpec((LANES, GLA_QK), lambda b, i: (0, 0)),
            pl.BlockSpec((1, GLA_QK), lambda b, i: (0, 0)),
            pl.BlockSpec((1, GLA_WIDTH), lambda b, i: (0, 0)),
            pl.BlockSpec((1, LANES), lambda b, i: (0, 0)),
            pl.BlockSpec((3, LANES, FOX_WIDTH), lambda b, i: (0, 0, 0)),
            pl.BlockSpec((1, FOX_WIDTH), lambda b, i: (0, 0)),
        ],
        out_specs=[
            pl.BlockSpec((tm, GLA_WIDTH), lambda b, i: (rows(b, i), 0)),
            pl.BlockSpec((tm, FOX_WIDTH), lambda b, i: (rows(b, i), 0)),
            pl.BlockSpec((1, FOX_HEADS, 1, tm), lambda b, i: (b, 0, 0, i)),
        ],
        out_shape=[
            jax.ShapeDtypeStruct((t, GLA_WIDTH), BF16),
            jax.ShapeDtypeStruct((t, FOX_WIDTH), BF16),
            jax.ShapeDtypeStruct((batch, FOX_HEADS, 1, seq), F32),
        ],
        scratch_shapes=[
            pltpu.VMEM((GLA_DV, GLA_QK), F32),
            pltpu.VMEM((1, LANES), F32),
            pltpu.VMEM((tm, GLA_WIDTH), F32),
        ],
        compiler_params=_params("parallel", "arbitrary"),
        name="gla",
    )(proj, proj, proj, proj, small, wg_hi, wg_lo, b_gk, gain, bf_pad, sel, ones)


FOX_TQ = 512
FOX_TK = 512
FOX_PAIRS = 2


def _fox_kernel(q_ref, k_ref, kx_ref, vt_ref, frow_ref, o_ref, m_ref, acc_ref, sa_ref, sb_ref,
                ma_ref, mb_ref):
    tq, tk = FOX_TQ, FOX_TK
    i = pl.program_id(2)
    q0 = pl.multiple_of(i * tq, tq)
    lane = lax.broadcasted_iota(jnp.int32, (1, LANES), 1)
    heads = range(2 * FOX_PAIRS)

    def pair_lanes(hh):
        return slice((hh // 2) * LANES, (hh // 2 + 1) * LANES)

    q_aug = []
    for hh in heads:
        head_lanes = (lane < FOX_DH) if hh % 2 == 0 else (lane >= FOX_DH)
        qm = jnp.where(head_lanes, q_ref[:, pair_lanes(hh)], 0.0).astype(BF16)
        fr = jnp.broadcast_to(frow_ref[0, hh, :, pl.ds(q0, LANES)][:, 0:1], (1, LANES))
        f1, f2, f3 = _split3(fr)
        base = (hh % 2) * FOX_BIAS_LANES
        qx = jnp.where(jnp.logical_and(lane >= base, lane < base + 3), 1.0, 0.0).astype(BF16)
        qx = jnp.where(lane == base + 3, f1, qx)
        qx = jnp.where(lane == base + 4, f2, qx)
        qx = jnp.where(lane == base + 5, f3, qx)
        q_aug.append(jnp.concatenate([qm, jnp.broadcast_to(qx, (tq, LANES))], axis=1))

    m_ref[...] = jnp.full(m_ref.shape, NEG, F32)
    acc_ref[...] = jnp.zeros_like(acc_ref)

    bufs = {"a": (sa_ref, ma_ref), "b": (sb_ref, mb_ref)}

    half = tk // 2
    lo, hi = slice(0, half), slice(half, tk)

    def k_aug(k0, n, hh):
        return jnp.concatenate([k_ref[pl.ds(k0, n), pair_lanes(hh)],
                                kx_ref[pl.ds(k0, n), pair_lanes(hh)]], axis=1)

    def causal(st):
        key = lax.broadcasted_iota(jnp.int32, st.shape, 0)
        qry = lax.broadcasted_iota(jnp.int32, st.shape, 1)
        return jnp.where(key <= qry, st, NEG)

    def produce(which, blk, hh, diagonal):
        buf, mx = bufs[which]
        k0 = pl.multiple_of(blk * tk, tk)
        if not diagonal:
            st = _dot_nt(k_aug(k0, tk, hh), q_aug[hh])
            buf[hh] = st
            mx[hh] = jnp.max(st, axis=0, keepdims=True)
            return
        st_lo = _dot_nt(k_aug(k0, half, hh), q_aug[hh])
        st_lo = jnp.concatenate([causal(st_lo[:, lo]), st_lo[:, hi]], axis=1)
        st_hi = causal(_dot_nt(k_aug(k0 + half, half, hh), q_aug[hh][hi]))
        buf[hh, lo, :] = st_lo
        buf[hh, hi, hi] = st_hi
        max_lo = jnp.max(st_lo, axis=0, keepdims=True)
        max_hi = jnp.max(st_hi, axis=0, keepdims=True)
        mx[hh] = jnp.concatenate([max_lo[:, lo], jnp.maximum(max_lo[:, hi], max_hi)], axis=1)

    def consume(which, blk, hh, diagonal):
        buf, mx = bufs[which]
        k0 = pl.multiple_of(blk * tk, tk)
        m_old = m_ref[hh]
        m_new = jnp.maximum(m_old, mx[hh])
        alpha = jnp.exp2(m_old - m_new)
        vrows = slice(hh * FOX_VROWS, (hh + 1) * FOX_VROWS)
        if not diagonal:
            p = jnp.exp2(buf[hh] - m_new).astype(BF16)
            acc_ref[hh] = alpha * acc_ref[hh] + _dot(vt_ref[vrows, pl.ds(k0, tk)], p)
        else:
            p_lo = jnp.exp2(buf[hh, lo, :] - m_new).astype(BF16)
            p_hi = jnp.exp2(buf[hh, hi, hi] - m_new[:, hi]).astype(BF16)
            pv_lo = _dot(vt_ref[vrows, pl.ds(k0, half)], p_lo)
            pv_hi = _dot(vt_ref[vrows, pl.ds(k0 + half, half)], p_hi)
            acc_ref[hh] = alpha * acc_ref[hh] + jnp.concatenate(
                [pv_lo[:, lo], pv_lo[:, hi] + pv_hi], axis=1)
        m_ref[hh] = m_new

    def stage(cur, cur_blk, nxt=None, nxt_blk=None, nxt_diagonal=False, cur_diagonal=False):
        for hh in heads:
            if nxt is not None:
                produce(nxt, nxt_blk, hh, nxt_diagonal)
            if hh > 0:
                consume(cur, cur_blk, hh - 1, cur_diagonal)
        consume(cur, cur_blk, heads[-1], cur_diagonal)

    @pl.when(i == 0)
    def _():
        for hh in heads:
            produce("a", 0, hh, True)
        stage("a", 0, cur_diagonal=True)

    @pl.when(i > 0)
    def _():
        for hh in heads:
            produce("a", 0, hh, False)

    def pair(t, carry):
        b0 = 2 * t
        stage("a", b0, "b", b0 + 1)
        stage("b", b0 + 1, "a", b0 + 2)
        return carry

    n_pairs = lax.shift_right_logical(jnp.maximum(i - 1, 0), 1)
    lax.fori_loop(0, n_pairs, pair, 0)

    @pl.when((i & 1) == 1)
    def _():
        stage("a", i - 1, "b", i, True)
        stage("b", i, cur_diagonal=True)

    @pl.when(jnp.logical_and((i & 1) == 0, i > 0))
    def _():
        stage("a", i - 2, "b", i - 1)
        stage("b", i - 1, "a", i, True)
        stage("a", i, cur_diagonal=True)

    for pr in range(FOX_PAIRS):
        outs = []
        for hh in (2 * pr, 2 * pr + 1):
            acc = acc_ref[hh]
            outs.append(acc[:FOX_DH] * (1.0 / acc[FOX_DH:FOX_DH + 1]))
        o_ref[:, pr * LANES:(pr + 1) * LANES] = jnp.concatenate(outs, axis=0).T.astype(BF16)


def _fox_call(proj, kx, vt, frow, batch, seq):
    t = proj.shape[0]
    tq = FOX_TQ
    nq = seq // tq
    width = FOX_PAIRS * LANES
    groups = FOX_WIDTH // width
    nh = 2 * FOX_PAIRS
    return pl.pallas_call(
        _fox_kernel,
        grid=(batch, groups, nq),
        in_specs=[
            pl.BlockSpec((tq, width), lambda b, hp, i: (b * nq + i, COL_QF // width + hp)),
            pl.BlockSpec((seq, width), lambda b, hp, i: (b, COL_KF // width + hp)),
            pl.BlockSpec((seq, width), lambda b, hp, i: (b, hp)),
            pl.BlockSpec((nh * FOX_VROWS, seq), lambda b, hp, i: (hp, b)),
            pl.BlockSpec((1, nh, 1, seq), lambda b, hp, i: (b, hp, 0, 0)),
        ],
        out_specs=pl.BlockSpec((tq, width), lambda b, hp, i: (b * nq + i, hp)),
        out_shape=jax.ShapeDtypeStruct((t, FOX_WIDTH), BF16),
        scratch_shapes=[
            pltpu.VMEM((nh, 1, tq), F32),
            pltpu.VMEM((nh, FOX_VROWS, tq), F32),
            pltpu.VMEM((nh, FOX_TK, tq), F32),
            pltpu.VMEM((nh, FOX_TK, tq), F32),
            pltpu.VMEM((nh, 1, tq), F32),
            pltpu.VMEM((nh, 1, tq), F32),
        ],
        compiler_params=_params("parallel", "parallel", "arbitrary"),
        name="fox",
    )(proj, proj, kx, vt, frow)


POST_TM = 512
POST_SUB = 2


def _post_kernel(og_ref, of_ref, x_ref, mod_ref, pg_ref, mg_ref, wo_ref, wrh_ref, wrl_ref, br_ref,
                 x1_ref, h2_ref, idx_ref, rank_ref, gate_ref, cnt_ref, car_ref):
    @pl.when(pl.program_id(0) == 0)
    def _():
        car_ref[...] = jnp.zeros_like(car_ref)

    ys = []
    for s in range(POST_SUB):
        rows = slice(s * POST_TM, (s + 1) * POST_TM)
        ys.append(_dot(og_ref[rows, :], wo_ref[:GLA_WIDTH, :])
                  + _dot(of_ref[rows, :], wo_ref[GLA_WIDTH:, :]))
    h2s = [_post_norm(s, ys[s], x_ref, mod_ref, pg_ref, mg_ref, x1_ref, h2_ref)
           for s in range(POST_SUB)]
    logits = jnp.concatenate([_post_logits(h2, wrh_ref, wrl_ref, br_ref) for h2 in h2s], axis=0)
    _post_topk(logits, idx_ref, rank_ref, gate_ref, cnt_ref, car_ref)


def _post_norm(s, y, x_ref, mod_ref, pg_ref, mg_ref, x1_ref, h2_ref):
    tm = POST_TM
    rows = slice(s * tm, (s + 1) * tm)
    x1 = x_ref[rows, :] + mod_ref[0, 2:3, :] * (_rms(y) * pg_ref[...])
    x1_ref[rows, :] = x1
    h2 = _rms(x1) * mg_ref[...]
    h2 = h2 * (1.0 + mod_ref[0, 4:5, :]) + mod_ref[0, 3:4, :]
    for j in range(ROW_TILES):
        h2_ref[pl.ds(s * tm * ROW_TILES + j, tm, stride=ROW_TILES), :] = h2[:, j * LANES:(j + 1) * LANES]

    return _split2(h2)


def _post_logits(h2_split, wrh_ref, wrl_ref, br_ref):
    hh, hl = h2_split
    return _dot(hh, wrh_ref[...]) + (_dot(hh, wrl_ref[...]) + _dot(hl, wrh_ref[...])) + br_ref[...]


def _post_topk(logits, idx_ref, rank_ref, gate_ref, cnt_ref, car_ref):
    tm = POST_TM
    n = logits.shape[0]
    lane = lax.broadcasted_iota(jnp.int32, (n, LANES), 1).astype(F32)
    work = jnp.where(lane < N_EXPERTS, logits, -jnp.inf)
    vals, idxs = [], []
    for _ in range(TOP_K):
        mx = jnp.max(work, axis=-1, keepdims=True)
        ix = jnp.min(jnp.where(work == mx, lane, float(LANES)), axis=-1, keepdims=True)
        vals.append(mx)
        idxs.append(ix)
        work = jnp.where(lane == ix, -jnp.inf, work)
    es = [jnp.exp(v - vals[0]) for v in vals]
    den = es[0] + es[1] + es[2] + es[3]

    onehot = jnp.zeros((n, LANES), F32)
    for ix in idxs:
        onehot = onehot + jnp.where(lane == ix, 1.0, 0.0)
    row = lax.broadcasted_iota(jnp.int32, (tm, tm), 0)
    col = lax.broadcasted_iota(jnp.int32, (tm, tm), 1)
    strict_b = jnp.where(col < row, 1.0, 0.0).astype(BF16)
    carry = car_ref[...]
    prefixes = []
    for s in range(n // tm):
        part = onehot[s * tm:(s + 1) * tm]
        prefixes.append(_dot(strict_b, part.astype(BF16)) + carry)
        carry = carry + jnp.sum(part, axis=0, keepdims=True)
    prefix = jnp.concatenate(prefixes, axis=0)
    car_ref[...] = carry
    cnt_ref[...] = carry

    idx_o = jnp.zeros((n, LANES), F32)
    rank_o = jnp.zeros((n, LANES), F32)
    gate_o = jnp.zeros((n, LANES), F32)
    for kk in range(TOP_K):
        rk = jnp.sum(jnp.where(lane == idxs[kk], prefix, 0.0), axis=-1, keepdims=True)
        sel = lane == float(kk)
        idx_o = jnp.where(sel, idxs[kk], idx_o)
        rank_o = jnp.where(sel, rk, rank_o)
        gate_o = jnp.where(sel, es[kk] / den, gate_o)
    idx_ref[...] = idx_o.astype(jnp.int32)
    rank_ref[...] = rank_o.astype(jnp.int32)
    gate_ref[...] = gate_o


def _post_call(o_gla, o_fox, x2, mod3, post_gain, mlp_gain, w_o, wr_hi, wr_lo, br_pad, seq):
    t, d = x2.shape
    tm = POST_TM * POST_SUB
    per_b = seq // tm
    return pl.pallas_call(
        _post_kernel,
        grid=(t // tm,),
        in_specs=[
            pl.BlockSpec((tm, GLA_WIDTH), lambda i: (i, 0)),
            pl.BlockSpec((tm, FOX_WIDTH), lambda i: (i, 0)),
            pl.BlockSpec((tm, d), lambda i: (i, 0)),
            pl.BlockSpec((1, N_MOD, d), lambda i: (i // per_b, 0, 0)),
            pl.BlockSpec((1, d), lambda i: (0, 0)),
            pl.BlockSpec((1, d), lambda i: (0, 0)),
            pl.BlockSpec((GLA_WIDTH + FOX_WIDTH, d), lambda i: (0, 0)),
            pl.BlockSpec((d, LANES), lambda i: (0, 0)),
            pl.BlockSpec((d, LANES), lambda i: (0, 0)),
            pl.BlockSpec((1, LANES), lambda i: (0, 0)),
        ],
        out_specs=[
            pl.BlockSpec((tm, d), lambda i: (i, 0)),
            pl.BlockSpec((tm * ROW_TILES, LANES), lambda i: (i, 0)),
            pl.BlockSpec((tm, LANES), lambda i: (i, 0)),
            pl.BlockSpec((tm, LANES), lambda i: (i, 0)),
            pl.BlockSpec((tm, LANES), lambda i: (i, 0)),
            pl.BlockSpec((1, LANES), lambda i: (0, 0)),
        ],
        out_shape=[
            jax.ShapeDtypeStruct((t, d), F32),
            jax.ShapeDtypeStruct((t * ROW_TILES, LANES), F32),
            jax.ShapeDtypeStruct((t, LANES), jnp.int32),
            jax.ShapeDtypeStruct((t, LANES), jnp.int32),
            jax.ShapeDtypeStruct((t, LANES), F32),
            jax.ShapeDtypeStruct((1, LANES), F32),
        ],
        scratch_shapes=[pltpu.VMEM((1, LANES), F32)],
        compiler_params=_params("arbitrary"),
        name="post",
    )(o_gla, o_fox, x2, mod3, post_gain, mlp_gain, w_o, wr_hi, wr_lo, br_pad)


DISP_TM = 512
DISP_SLOTS = 3
ISSUE_UNROLL = 8


def _row_copy(src_ref, src_row, dst_ref, dst_row, sem):
    return pltpu.make_async_copy(
        src_ref.at[pl.ds(pl.multiple_of(src_row * ROW_TILES, ROW_TILES), ROW_TILES)],
        dst_ref.at[pl.ds(pl.multiple_of(dst_row * ROW_TILES, ROW_TILES), ROW_TILES)],
        sem)


def _dispatch_kernel(pend_ref, pos_ref, h_hbm, xs_ref, zero_ref, hbuf_ref, sem, in_sem, out_sem):
    tm = DISP_TM
    rows = tm * ROW_TILES
    zrows = MOE_TM * ROW_TILES
    i = pl.program_id(0)
    last = pl.num_programs(0) - 1
    slot = lax.rem(i, DISP_SLOTS)
    nxt = lax.rem(i + 1, DISP_SLOTS)

    def load(step, s):
        return pltpu.make_async_copy(
            h_hbm.at[pl.ds(pl.multiple_of(step * rows, rows), rows)], hbuf_ref.at[s], in_sem.at[s])

    def drain(s):
        for _ in range(TOP_K):
            pltpu.make_async_copy(hbuf_ref.at[s], xs_ref.at[pl.ds(0, rows)], out_sem.at[s]).wait()

    @pl.when(i == 0)
    def _():
        load(0, 0).start()
        zero_ref[...] = jnp.zeros_like(zero_ref)

        def last_tile(e):
            start = pl.multiple_of((pend_ref[e] - MOE_TM) * ROW_TILES, zrows)
            return pltpu.make_async_copy(zero_ref, xs_ref.at[pl.ds(start, zrows)], sem)

        def nonempty(e):
            return pend_ref[e] > jnp.where(e == 0, 0, pend_ref[jnp.maximum(e - 1, 0)])

        def clear(e, carry):
            @pl.when(nonempty(e))
            def _():
                last_tile(e).start()
            return carry

        def clear_wait(e, carry):
            @pl.when(nonempty(e))
            def _():
                last_tile(e).wait()
            return carry

        lax.fori_loop(0, N_EXPERTS, clear, 0)
        lax.fori_loop(0, N_EXPERTS, clear_wait, 0)

        def tail_tile(j):
            return pltpu.make_async_copy(
                zero_ref, xs_ref.at[pl.ds(pl.multiple_of(j * zrows, zrows), zrows)], sem)

        def tail(j, carry):
            tail_tile(j).start()
            return carry

        def tail_wait(j, carry):
            tail_tile(j).wait()
            return carry

        first_unused = lax.div(pend_ref[N_EXPERTS - 1], MOE_TM)
        n_tiles = xs_ref.shape[0] // zrows
        lax.fori_loop(first_unused, n_tiles, tail, 0)
        lax.fori_loop(first_unused, n_tiles, tail_wait, 0)

    @pl.when(i >= DISP_SLOTS - 1)
    def _():
        drain(nxt)

    @pl.when(i < last)
    def _():
        load(i + 1, nxt).start()

    load(i, slot).wait()

    def issue(g, carry):
        for u in range(ISSUE_UNROLL):
            t = g * ISSUE_UNROLL + u
            for kk in range(TOP_K):
                _row_copy(hbuf_ref.at[slot], t, xs_ref, pos_ref[0, 0, t * TOP_K + kk],
                          out_sem.at[slot]).start(priority=kk % 2)
        return carry

    lax.fori_loop(0, tm // ISSUE_UNROLL, issue, 0)

    @pl.when(i == last)
    def _():
        for back in range(DISP_SLOTS - 1):
            @pl.when(i >= back)
            def _():
                drain(lax.rem(i - back + DISP_SLOTS, DISP_SLOTS))


def _dispatch_call(pend, pos3, h2, n_tiles):
    tm = DISP_TM
    n_steps = pos3.shape[0]
    grid_spec = pltpu.PrefetchScalarGridSpec(
        num_scalar_prefetch=1,
        grid=(n_steps,),
        in_specs=[
            pl.BlockSpec((1, 1, tm * TOP_K), lambda i, pend: (i, 0, 0), memory_space=pltpu.SMEM),
            pl.BlockSpec(memory_space=pl.ANY),
        ],
        out_specs=pl.BlockSpec(memory_space=pl.ANY),
        scratch_shapes=[
            pltpu.VMEM((MOE_TM * ROW_TILES, LANES), F32),
            pltpu.VMEM((DISP_SLOTS, tm * ROW_TILES, LANES), F32),
            pltpu.SemaphoreType.DMA,
            pltpu.SemaphoreType.DMA((DISP_SLOTS,)),
            pltpu.SemaphoreType.DMA((DISP_SLOTS,)),
        ],
    )
    return pl.pallas_call(
        _dispatch_kernel,
        grid_spec=grid_spec,
        out_shape=jax.ShapeDtypeStruct((n_tiles * MOE_TM * ROW_TILES, LANES), F32),
        compiler_params=_params("arbitrary"),
        name="dispatch",
    )(pend, pos3, h2)


MOE_TM = 512


def _experts_kernel(te_ref, nu_ref, x_ref, win_ref, bin_ref, wout_ref, bout_ref, y_ref,
                    wib_ref, wob_ref):
    tm = MOE_TM
    i = pl.program_id(0)
    used = i < nu_ref[0]

    @pl.when(jnp.logical_not(used))
    def _():
        y_ref[...] = jnp.zeros_like(y_ref)

    fresh = jnp.logical_or(i == 0, te_ref[i] != te_ref[jnp.maximum(i - 1, 0)])

    @pl.when(jnp.logical_and(used, fresh))
    def _():
        wib_ref[...] = win_ref[0].astype(BF16)
        wob_ref[...] = wout_ref[0].astype(BF16)

    @pl.when(used)
    def _():
        x = jnp.concatenate([x_ref[pl.ds(j, tm, stride=ROW_TILES), :] for j in range(ROW_TILES)],
                            axis=1).astype(BF16)
        u = _dot(x, wib_ref[...]) + bin_ref[0]
        glu = jnp.minimum(u[:, :D_FF], SWIGLU_LIMIT)
        lin = jnp.clip(u[:, D_FF:], -SWIGLU_LIMIT, SWIGLU_LIMIT)
        a = glu * jax.nn.sigmoid(SWIGLU_ALPHA * glu) * (lin + 1.0)
        y = _dot(a.astype(BF16), wob_ref[...]) + bout_ref[0]
        for j in range(ROW_TILES):
            y_ref[pl.ds(j, tm, stride=ROW_TILES), :] = y[:, j * LANES:(j + 1) * LANES]


def _experts_call(tile_expert, n_used, xs, w_in, b_in, w_out, b_out):
    tm = MOE_TM
    n_tiles = tile_expert.shape[0]
    d = D_MODEL

    def tile(i, te, nu):
        return (jnp.minimum(i, nu[0] - 1), 0)

    grid_spec = pltpu.PrefetchScalarGridSpec(
        num_scalar_prefetch=2,
        grid=(n_tiles,),
        in_specs=[
            pl.BlockSpec((tm * ROW_TILES, LANES), tile),
            pl.BlockSpec((1, d, 2 * D_FF), lambda i, te, nu: (te[i], 0, 0)),
            pl.BlockSpec((1, 1, 2 * D_FF), lambda i, te, nu: (te[i], 0, 0)),
            pl.BlockSpec((1, D_FF, d), lambda i, te, nu: (te[i], 0, 0)),
            pl.BlockSpec((1, 1, d), lambda i, te, nu: (te[i], 0, 0)),
        ],
        out_specs=pl.BlockSpec((tm * ROW_TILES, LANES), lambda i, te, nu: (i, 0)),
        scratch_shapes=[
            pltpu.VMEM((d, 2 * D_FF), BF16),
            pltpu.VMEM((D_FF, d), BF16),
        ],
    )
    return pl.pallas_call(
        _experts_kernel,
        grid_spec=grid_spec,
        out_shape=jax.ShapeDtypeStruct(xs.shape, F32),
        compiler_params=_params("arbitrary"),
        name="experts",
    )(tile_expert, n_used, xs, w_in, b_in, w_out, b_out)


COMB_TM = 256


def _combine_kernel(pos_ref, posn_ref, y_ref, gate_ref, x1_ref, mod_ref, pg_ref, o_ref, buf_ref, sem):
    tm = COMB_TM
    i = pl.program_id(0)
    slot = i & 1

    def gather(p_ref, s):
        def issue(g, carry):
            for u in range(ISSUE_UNROLL):
                t = g * ISSUE_UNROLL + u
                for kk in range(TOP_K):
                    _row_copy(y_ref, p_ref[0, 0, t * TOP_K + kk], buf_ref.at[s, kk], t,
                              sem.at[s]).start(priority=kk % 2)
            return carry

        lax.fori_loop(0, tm // ISSUE_UNROLL, issue, 0)

    @pl.when(i == 0)
    def _():
        gather(pos_ref, 0)

    @pl.when(i + 1 < pl.num_programs(0))
    def _():
        gather(posn_ref, 1 - slot)

    for kk in range(TOP_K):
        pltpu.make_async_copy(y_ref.at[pl.ds(0, tm * ROW_TILES)], buf_ref.at[slot, kk],
                              sem.at[slot]).wait()

    gates = gate_ref[...]
    acc = None
    for kk in range(TOP_K):
        rows = jnp.concatenate(
            [buf_ref[slot, kk, pl.ds(j, tm, stride=ROW_TILES), :] for j in range(ROW_TILES)], axis=1)
        term = rows * gates[:, kk:kk + 1]
        acc = term if acc is None else acc + term
    o_ref[...] = x1_ref[...] + mod_ref[0, 5:6, :] * (_rms(acc) * pg_ref[...])


def _combine_call(pos3, y, gates, x1, mod3, post_gain, seq):
    t, d = x1.shape
    tm = COMB_TM
    per_b = seq // tm
    n_steps = t // tm
    return pl.pallas_call(
        _combine_kernel,
        grid=(n_steps,),
        in_specs=[
            pl.BlockSpec((1, 1, tm * TOP_K), lambda i: (i, 0, 0), memory_space=pltpu.SMEM),
            pl.BlockSpec((1, 1, tm * TOP_K), lambda i: (jnp.minimum(i + 1, n_steps - 1), 0, 0),
                         memory_space=pltpu.SMEM),
            pl.BlockSpec(memory_space=pl.ANY),
            pl.BlockSpec((tm, LANES), lambda i: (i, 0)),
            pl.BlockSpec((tm, d), lambda i: (i, 0)),
            pl.BlockSpec((1, N_MOD, d), lambda i: (i // per_b, 0, 0)),
            pl.BlockSpec((1, d), lambda i: (0, 0)),
        ],
        out_specs=pl.BlockSpec((tm, d), lambda i: (i, 0)),
        out_shape=jax.ShapeDtypeStruct((t, d), F32),
        scratch_shapes=[
            pltpu.VMEM((2, TOP_K, tm * ROW_TILES, LANES), F32),
            pltpu.SemaphoreType.DMA((2,)),
        ],
        compiler_params=_params("arbitrary"),
        name="combine",
    )(pos3, pos3, y, gates, x1, mod3, post_gain)


def _pad_cols(w, n):
    return jnp.pad(w, ((0, 0), (0, n - w.shape[1])))


def kernel(x, c, w_ada, b_ada, attn_pre_gain, attn_post_gain, w_in, w_gk2, b_gk, gla_norm_gain, b_f,
           w_o, mlp_pre_gain, mlp_post_gain, w_router, b_router, w_e_in, b_e_in, w_e_out, b_e_out):
    batch, seq, d = x.shape
    t = batch * seq
    x2 = x.reshape(t, d)

    o = 0
    cols = {}
    for name, width in (("qg", GLA_QK), ("kg", GLA_QK), ("vg", GLA_WIDTH), ("gg", GLA_WIDTH),
                        ("zg", GLA_LOWRANK), ("qf", FOX_WIDTH), ("kf", FOX_WIDTH), ("vf", FOX_WIDTH),
                        ("ff", FOX_HEADS)):
        cols[name] = w_in[:, o:o + width]
        o += width
    gla_cols = 2 * GLA_QK + 2 * GLA_WIDTH
    fox_start = gla_cols + GLA_LOWRANK
    w_main = jnp.concatenate([w_in[:, :gla_cols], w_in[:, fox_start:fox_start + 2 * FOX_WIDTH]],
                             axis=1).astype(BF16)
    w_vt = cols["vf"].T.astype(BF16)
    w_small = _pad_cols(jnp.concatenate([cols["ff"], cols["zg"]], axis=1), LANES)
    ws_hi, ws_lo = _split2(w_small)
    wg = jnp.zeros((LANES, GLA_QK), F32).at[SMALL_ZG:SMALL_ZG + GLA_LOWRANK].set(w_gk2)
    wg_hi, wg_lo = _split2(wg)
    bf_pad = jnp.zeros((1, LANES), F32).at[0, SMALL_FF:SMALL_FF + FOX_HEADS].set(b_f)
    wr_hi, wr_lo = _split2(_pad_cols(w_router, LANES))
    br_pad = _pad_cols(b_router.reshape(1, N_EXPERTS), LANES)

    c_pad = jnp.pad(c, ((0, SUBLANES - batch % SUBLANES if batch % SUBLANES else 0), (0, 0)))
    mod = _mod_call(c_pad, w_ada, b_ada.reshape(1, -1))
    mod3 = mod[:batch].reshape(batch, N_MOD, d)

    proj, vt, small = _inproj_call(x2, mod3, attn_pre_gain.reshape(1, d), w_main, w_vt, ws_hi, ws_lo,
                                   seq)
    o_gla, kx, frow = _gla_call(proj, small, wg_hi, wg_lo, b_gk.reshape(1, -1),
                                gla_norm_gain.reshape(1, -1), bf_pad, batch, seq)
    o_fox = _fox_call(proj, kx, vt, frow, batch, seq)

    x1, h2, idx_w, rank_w, gate_w, cnt = _post_call(
        o_gla, o_fox, x2, mod3, attn_post_gain.reshape(1, d), mlp_pre_gain.reshape(1, d),
        w_o.astype(BF16), wr_hi, wr_lo, br_pad, seq)

    tm = MOE_TM
    n_tiles = (t * TOP_K) // tm + N_EXPERTS
    counts = cnt[0, :N_EXPERTS].astype(jnp.int32)
    padded = ((counts + tm - 1) // tm) * tm
    pend = jnp.cumsum(padded)
    pstart = pend - padded
    experts = jnp.arange(N_EXPERTS, dtype=jnp.int32)
    first_slot = jnp.sum(jnp.where(idx_w[:, :TOP_K, None] == experts, pstart.astype(jnp.int32), 0),
                         axis=-1)
    pos = first_slot + rank_w[:, :TOP_K]
    n_used = (pend[-1] // tm).astype(jnp.int32).reshape(1)
    tile_start = jnp.arange(n_tiles, dtype=jnp.int32) * tm
    tile_expert = jnp.minimum(
        jnp.sum((pend[None, :] <= tile_start[:, None]).astype(jnp.int32), axis=1), N_EXPERTS - 1)

    pos_d = pos.reshape(t // DISP_TM, 1, DISP_TM * TOP_K)
    xs = _dispatch_call(pend.astype(jnp.int32), pos_d, h2, n_tiles)
    y = _experts_call(tile_expert, n_used, xs, w_e_in, b_e_in.reshape(N_EXPERTS, 1, -1), w_e_out,
                      b_e_out.reshape(N_EXPERTS, 1, -1))
    pos_c = pos.reshape(t // COMB_TM, 1, COMB_TM * TOP_K)
    out = _combine_call(pos_c, y, gate_w, x1, mod3, mlp_post_gain.reshape(1, d), seq)
    return out.reshape(batch, seq, d)
```

```python
import jax
import jax.numpy as jnp
from jax import lax
from jax.experimental import pallas as pl
from jax.experimental.pallas import tpu as pltpu

F32 = jnp.float32
BF16 = jnp.bfloat16

D_MODEL = 1024
N_MOD = 6
GLA_HEADS = 4
GLA_DK = 64
GLA_DV = 128
GLA_QK = GLA_HEADS * GLA_DK
GLA_WIDTH = GLA_HEADS * GLA_DV
GLA_LOWRANK = 16
GLA_CHUNK = 64
GLA_GATE_NORM = 16.0
FOX_HEADS = 8
FOX_DH = 64
FOX_WIDTH = FOX_HEADS * FOX_DH
N_EXPERTS = 32
TOP_K = 4
D_FF = D_MODEL
SWIGLU_ALPHA = 1.702
SWIGLU_LIMIT = 7.0
EPS = 1e-6

LANES = 128
SUBLANES = 8
ROW_TILES = D_MODEL // LANES

COL_QG, COL_KG, COL_VG, COL_GG = 0, 256, 512, 1024
COL_QF, COL_KF = 1536, 2048
MAIN_COLS = 2560
LOG2E = 1.4426950408889634
FOX_Q_SCALE = (FOX_DH ** -0.5) * LOG2E
FOX_BIAS_LANES = 6
BF16_SUBLANES = 16
FOX_VROWS = FOX_DH + BF16_SUBLANES
SMALL_FF, SMALL_ZG = 0, 8

NEG = -1e30
VMEM_LIMIT = 56 * 1024 * 1024


def _dot(a, b):
    return jnp.dot(a, b, preferred_element_type=F32)


def _dot_nt(a, b):
    return lax.dot_general(a, b, (((1,), (1,)), ((), ())), preferred_element_type=F32)


def _dot_tn(a, b):
    return lax.dot_general(a, b, (((0,), (0,)), ((), ())), preferred_element_type=F32)


def _split2(a):
    hi = a.astype(BF16)
    lo = (a - hi.astype(F32)).astype(BF16)
    return hi, lo


def _split3(a):
    hi = a.astype(BF16)
    r = a - hi.astype(F32)
    mid = r.astype(BF16)
    lo = (r - mid.astype(F32)).astype(BF16)
    return hi, mid, lo


def _log_sigmoid(x):
    return jnp.minimum(x, 0.0) - jnp.log1p(jnp.exp(-jnp.abs(x)))


def _rms(x):
    return x * lax.rsqrt(jnp.mean(x * x, axis=-1, keepdims=True) + EPS)


def _params(*sem):
    return pltpu.CompilerParams(dimension_semantics=sem, vmem_limit_bytes=VMEM_LIMIT)


def _mod_kernel(c_ref, w_ref, b_ref, o_ref):
    c = c_ref[...]
    s = c * jax.nn.sigmoid(c)
    sh, sl = _split2(s)
    wh, wl = _split2(w_ref[...])
    o_ref[...] = _dot(sh, wh) + (_dot(sh, wl) + _dot(sl, wh)) + b_ref[...]


def _mod_call(c_pad, w_ada, b_ada):
    rows, d = c_pad.shape
    n = w_ada.shape[1]
    tn = D_MODEL
    return pl.pallas_call(
        _mod_kernel,
        grid=(n // tn,),
        in_specs=[
            pl.BlockSpec((rows, d), lambda j: (0, 0)),
            pl.BlockSpec((d, tn), lambda j: (0, j)),
            pl.BlockSpec((1, tn), lambda j: (0, j)),
        ],
        out_specs=pl.BlockSpec((rows, tn), lambda j: (0, j)),
        out_shape=jax.ShapeDtypeStruct((rows, n), F32),
        compiler_params=_params("parallel"),
        name="mod",
    )(c_pad, w_ada, b_ada)


INPROJ_TM = 512
INPROJ_SUB = 2


def _inproj_kernel(x_ref, mod_ref, g_ref, w_ref, wvt_ref, wsh_ref, wsl_ref, proj_ref, vt_ref, small_ref):
    subs = [pl.ds(s * INPROJ_TM, INPROJ_TM) for s in range(INPROJ_SUB)]
    hs = []
    for r in subs:
        h = _rms(x_ref[r, :]) * g_ref[...]
        h = h * (1.0 + mod_ref[0, 1:2, :]) + mod_ref[0, 0:1, :]
        hs.append(_split2(h))
    for r, (hh, hl) in zip(subs, hs):
        _inproj_sub(hh, hl, w_ref, wvt_ref, wsh_ref, wsl_ref, proj_ref.at[r], vt_ref.at[:, r],
                    small_ref.at[r])


def _inproj_sub(hh, hl, w_ref, wvt_ref, wsh_ref, wsl_ref, proj_ref, vt_ref, small_ref):
    cw = FOX_WIDTH
    for j in range(MAIN_COLS // cw):
        r = _dot(hh, w_ref[:, j * cw:(j + 1) * cw])
        if j * cw == COL_QF:
            r = r * FOX_Q_SCALE
        proj_ref[:, j * cw:(j + 1) * cw] = r.astype(BF16)
    vt = _dot_nt(wvt_ref[...], hh).astype(BF16)
    ones = jnp.ones((FOX_VROWS - FOX_DH, vt.shape[1]), BF16)
    for h in range(FOX_HEADS):
        vt_ref[h * FOX_VROWS:h * FOX_VROWS + FOX_DH, :] = vt[h * FOX_DH:(h + 1) * FOX_DH, :]
        vt_ref[h * FOX_VROWS + FOX_DH:(h + 1) * FOX_VROWS, :] = ones
    a = _dot(hh, wsl_ref[...])
    small_ref[...] = a[:, :LANES] + (a[:, LANES:] + _dot(hl, wsh_ref[...]))


def _inproj_call(x2, mod3, gain, w_main, w_vt, ws_hi, ws_lo, seq):
    ws_hilo = jnp.concatenate([ws_hi, ws_lo], axis=1)
    t, d = x2.shape
    tm = INPROJ_TM * INPROJ_SUB
    per_b = seq // tm
    return pl.pallas_call(
        _inproj_kernel,
        grid=(t // tm,),
        in_specs=[
            pl.BlockSpec((tm, d), lambda i: (i, 0)),
            pl.BlockSpec((1, N_MOD, d), lambda i: (i // per_b, 0, 0)),
            pl.BlockSpec((1, d), lambda i: (0, 0)),
            pl.BlockSpec((d, MAIN_COLS), lambda i: (0, 0)),
            pl.BlockSpec((FOX_WIDTH, d), lambda i: (0, 0)),
            pl.BlockSpec((d, LANES), lambda i: (0, 0)),
            pl.BlockSpec((d, 2 * LANES), lambda i: (0, 0)),
        ],
        out_specs=[
            pl.BlockSpec((tm, MAIN_COLS), lambda i: (i, 0)),
            pl.BlockSpec((FOX_HEADS * FOX_VROWS, tm), lambda i: (0, i)),
            pl.BlockSpec((tm, LANES), lambda i: (i, 0)),
        ],
        out_shape=[
            jax.ShapeDtypeStruct((t, MAIN_COLS), BF16),
            jax.ShapeDtypeStruct((FOX_HEADS * FOX_VROWS, t), BF16),
            jax.ShapeDtypeStruct((t, LANES), F32),
        ],
        compiler_params=_params("parallel"),
        name="inproj",
    )(x2, mod3, gain, w_main, w_vt, ws_hi, ws_hilo)


GLA_TM = 512
GLA_SUB = 2
GLA_BLK = 256


def _gla_kernel(q_ref, k_ref, v_ref, g_ref, small_ref, wgh_ref, wgl_ref, bgk_ref, gain_ref, bf_ref,
                sel_ref, ones_ref, o_ref, kx_ref, frow_ref, st_ref, fcar_ref, oacc_ref):
    @pl.when(pl.program_id(1) == 0)
    def _():
        st_ref[...] = jnp.zeros_like(st_ref)
        fcar_ref[...] = jnp.zeros_like(fcar_ref)

    subs = [pl.ds(s * GLA_TM, GLA_TM) for s in range(GLA_SUB)]
    fronts = [_gla_sub_front(q_ref.at[r], k_ref.at[r], v_ref.at[r], small_ref.at[r],
                             wgh_ref, wgl_ref, bgk_ref, oacc_ref.at[r]) for r in subs]
    for r, front in zip(subs, fronts):
        _gla_sub_chain(front, v_ref.at[r], g_ref.at[r], gain_ref, bf_ref, sel_ref, ones_ref,
                       o_ref.at[r], kx_ref.at[r], frow_ref.at[:, :, :, r], st_ref, fcar_ref,
                       oacc_ref.at[r])


def _gla_sub_front(q_ref, k_ref, v_ref, small_ref, wgh_ref, wgl_ref, bgk_ref, oacc_ref):
    tm = GLA_TM
    nc = tm // GLA_CHUNK
    small = small_ref[...]
    sh, sl = _split2(small)
    gk = _dot(sh, wgh_ref[...]) + (_dot(sh, wgl_ref[...]) + _dot(sl, wgh_ref[...])) + bgk_ref[...]
    log_a = _log_sigmoid(gk) * (1.0 / GLA_GATE_NORM)

    blk = GLA_BLK
    blocks = [slice(j * blk, (j + 1) * blk) for j in range(tm // blk)]
    row = lax.broadcasted_iota(jnp.int32, (blk, blk), 0)
    col = lax.broadcasted_iota(jnp.int32, (blk, blk), 1)
    chunk_shift = GLA_CHUNK.bit_length() - 1
    same = (row >> chunk_shift) == (col >> chunk_shift)
    lower = col <= row
    tri = jnp.logical_and(same, lower)
    tri_b = jnp.where(tri, 1.0, 0.0).astype(BF16)
    lower_b = jnp.where(lower, 1.0, 0.0).astype(BF16)

    lh, ll = _split2(log_a)
    b = jnp.concatenate([_dot(tri_b, lh[r]) + _dot(tri_b, ll[r]) for r in blocks], axis=0)
    btot = jnp.concatenate(
        [jnp.broadcast_to(b[(c + 1) * GLA_CHUNK - 1:(c + 1) * GLA_CHUNK, :], (GLA_CHUNK, GLA_QK))
         for c in range(nc)], axis=0)

    q = q_ref[...].astype(F32)
    k = k_ref[...].astype(F32)
    qt = q * jnp.exp(b) * (GLA_DK ** -0.5)
    kt = (k * jnp.exp(-b)).astype(BF16)
    kd = (k * jnp.exp(btot - b)).astype(BF16)

    lane_qk = lax.broadcasted_iota(jnp.int32, (1, GLA_QK), 1)
    head_masks = [jnp.logical_and(lane_qk >= h * GLA_DK, lane_qk < (h + 1) * GLA_DK)
                  for h in range(GLA_HEADS)]

    for h in range(GLA_HEADS):
        cs = slice(h * GLA_DV, (h + 1) * GLA_DV)
        qm = jnp.where(head_masks[h], qt, 0.0).astype(BF16)
        for r in blocks:
            att = jnp.where(tri, _dot_nt(qm[r], kt[r]), 0.0).astype(BF16)
            oacc_ref[r, cs] = _dot(att, v_ref[r, cs])
    return small, qt, kd, btot, head_masks, lower_b


def _gla_sub_chain(front, v_ref, g_ref, gain_ref, bf_ref, sel_ref, ones_ref, o_ref, kx_ref, frow_ref,
                   st_ref, fcar_ref, oacc_ref):
    tm = GLA_TM
    nc = tm // GLA_CHUNK
    small, qt, kd, btot, head_masks, lower_b = front
    _fox_gates(small, lower_b, bf_ref, sel_ref, ones_ref, kx_ref, frow_ref, fcar_ref)

    chunks = [slice(c * GLA_CHUNK, (c + 1) * GLA_CHUNK) for c in range(nc)]
    kvts = [_dot_tn(v_ref[rs, :], kd[rs]) for rs in chunks]
    states = []
    st = st_ref[...]
    for c, rs in enumerate(chunks):
        states.append(st.astype(BF16))
        decay = jnp.exp(btot[c * GLA_CHUNK:c * GLA_CHUNK + 1, :])
        new = decay * st
        for h in range(GLA_HEADS):
            new = new + jnp.where(head_masks[h], kvts[c][h * GLA_DV:(h + 1) * GLA_DV, :], 0.0)
        st = new
    st_ref[...] = st
    for c, rs in enumerate(chunks):
        qc = qt[rs]
        qs = jnp.concatenate([jnp.where(head_masks[h], qc, 0.0) for h in range(GLA_HEADS)],
                             axis=0).astype(BF16)
        oi = _dot_nt(qs, states[c])
        for h in range(GLA_HEADS):
            oacc_ref[rs, h * GLA_DV:(h + 1) * GLA_DV] += oi[h * GLA_CHUNK:(h + 1) * GLA_CHUNK, :]

    for h in range(GLA_HEADS):
        cs = slice(h * GLA_DV, (h + 1) * GLA_DV)
        y = _rms(oacc_ref[:, cs]) * gain_ref[:, cs]
        g = g_ref[:, cs].astype(F32)
        o_ref[:, cs] = (y * (g * jax.nn.sigmoid(g))).astype(BF16)


def _fox_gates(small, lower_b, bf_ref, sel_ref, ones_ref, kx_ref, frow_ref, fcar_ref):
    tm = GLA_TM
    lane = lax.broadcasted_iota(jnp.int32, (1, LANES), 1)
    lf = jnp.where(lane < FOX_HEADS, _log_sigmoid(small + bf_ref[...]), 0.0)
    l1, l2, l3 = _split3(lf)
    carry = fcar_ref[...]
    parts = []
    for j in range(tm // GLA_BLK):
        r = slice(j * GLA_BLK, (j + 1) * GLA_BLK)
        part = _dot(lower_b, l1[r]) + (_dot(lower_b, l2[r]) + _dot(lower_b, l3[r])) + carry
        carry = part[GLA_BLK - 1:GLA_BLK, :]
        parts.append(part)
    fl = jnp.concatenate(parts, axis=0)
    fcar_ref[...] = carry
    fs = fl * LOG2E
    fst = fs.T
    for h in range(FOX_HEADS):
        frow_ref[0, h] = fst[h:h + 1, :]
    n1, n2, n3 = _split3(-fs)
    kx = _dot(n1, sel_ref[0]) + (_dot(n2, sel_ref[1]) + _dot(n3, sel_ref[2])) + ones_ref[...]
    kx_ref[...] = kx.astype(BF16)


def _fox_bias_selectors():
    import numpy as np
    sel = np.zeros((3, LANES, FOX_WIDTH), np.float32)
    ones = np.zeros((1, FOX_WIDTH), np.float32)
    for h in range(FOX_HEADS):
        base = (h // 2) * LANES + (h % 2) * FOX_BIAS_LANES
        for i in range(3):
            sel[i, h, base + i] = 1.0
            ones[0, base + 3 + i] = 1.0
    return jnp.asarray(sel, BF16), jnp.asarray(ones, F32)


def _gla_call(proj, small, wg_hi, wg_lo, b_gk, gain, bf_pad, batch, seq):
    t = proj.shape[0]
    tm = GLA_TM * GLA_SUB
    nb = seq // tm

    def rows(b, i):
        return b * nb + i

    sel, ones = _fox_bias_selectors()
    return pl.pallas_call(
        _gla_kernel,
        grid=(batch, nb),
        in_specs=[
            pl.BlockSpec((tm, GLA_QK), lambda b, i: (rows(b, i), COL_QG // GLA_QK)),
            pl.BlockSpec((tm, GLA_QK), lambda b, i: (rows(b, i), COL_KG // GLA_QK)),
            pl.BlockSpec((tm, GLA_WIDTH), lambda b, i: (rows(b, i), COL_VG // GLA_WIDTH)),
            pl.BlockSpec((tm, GLA_WIDTH), lambda b, i: (rows(b, i), COL_GG // GLA_WIDTH)),
            pl.BlockSpec((tm, LANES), lambda b, i: (rows(b, i), 0)),
            pl.BlockSpec((LANES, GLA_QK), lambda b, i: (0, 0)),
            pl.BlockSpec((LANES, GLA_QK), lambda b, i: (0, 0)),
            pl.BlockSpec((1, GLA_QK), lambda b, i: (0, 0)),
            pl.BlockSpec((1, GLA_WIDTH), lambda b, i: (0, 0)),
            pl.BlockSpec((1, LANES), lambda b, i: (0, 0)),
            pl.BlockSpec((3, LANES, FOX_WIDTH), lambda b, i: (0, 0, 0)),
            pl.BlockSpec((1, FOX_WIDTH), lambda b, i: (0, 0)),
        ],
        out_specs=[
            pl.BlockSpec((tm, GLA_WIDTH), lambda b, i: (rows(b, i), 0)),
            pl.BlockSpec((tm, FOX_WIDTH), lambda b, i: (rows(b, i), 0)),
            pl.BlockSpec((1, FOX_HEADS, 1, tm), lambda b, i: (b, 0, 0, i)),
        ],
        out_shape=[
            jax.ShapeDtypeStruct((t, GLA_WIDTH), BF16),
            jax.ShapeDtypeStruct((t, FOX_WIDTH), BF16),
            jax.ShapeDtypeStruct((batch, FOX_HEADS, 1, seq), F32),
        ],
        scratch_shapes=[
            pltpu.VMEM((GLA_DV, GLA_QK), F32),
            pltpu.VMEM((1, LANES), F32),
            pltpu.VMEM((tm, GLA_WIDTH), F32),
        ],
        compiler_params=_params("parallel", "arbitrary"),
        name="gla",
    )(proj, proj, proj, proj, small, wg_hi, wg_lo, b_gk, gain, bf_pad, sel, ones)


FOX_TQ = 512
FOX_TK = 512
FOX_PAIRS = 4


def _fox_kernel(q_ref, k_ref, kx_ref, vt_ref, frow_ref, o_ref, m_ref, acc_ref, sa_ref, sb_ref,
                ma_ref, mb_ref):
    tq, tk = FOX_TQ, FOX_TK
    i = pl.program_id(2)
    q0 = pl.multiple_of(i * tq, tq)
    lane = lax.broadcasted_iota(jnp.int32, (1, LANES), 1)
    heads = range(2 * FOX_PAIRS)

    def pair_lanes(hh):
        return slice((hh // 2) * LANES, (hh // 2 + 1) * LANES)

    q_aug = []
    for hh in heads:
        head_lanes = (lane < FOX_DH) if hh % 2 == 0 else (lane >= FOX_DH)
        qm = jnp.where(head_lanes, q_ref[:, pair_lanes(hh)], 0.0).astype(BF16)
        fr = jnp.broadcast_to(frow_ref[0, hh, :, pl.ds(q0, LANES)][:, 0:1], (1, LANES))
        f1, f2, f3 = _split3(fr)
        base = (hh % 2) * FOX_BIAS_LANES
        qx = jnp.where(jnp.logical_and(lane >= base, lane < base + 3), 1.0, 0.0).astype(BF16)
        qx = jnp.where(lane == base + 3, f1, qx)
        qx = jnp.where(lane == base + 4, f2, qx)
        qx = jnp.where(lane == base + 5, f3, qx)
        q_aug.append(jnp.concatenate([qm, jnp.broadcast_to(qx, (tq, LANES))], axis=1))

    m_ref[...] = jnp.full(m_ref.shape, NEG, F32)
    acc_ref[...] = jnp.zeros_like(acc_ref)

    bufs = {"a": (sa_ref, ma_ref), "b": (sb_ref, mb_ref)}

    half = tk // 2
    lo, hi = slice(0, half), slice(half, tk)

    def k_aug(k0, n, hh):
        return jnp.concatenate([k_ref[pl.ds(k0, n), pair_lanes(hh)],
                                kx_ref[pl.ds(k0, n), pair_lanes(hh)]], axis=1)

    def causal(st):
        key = lax.broadcasted_iota(jnp.int32, st.shape, 0)
        qry = lax.broadcasted_iota(jnp.int32, st.shape, 1)
        return jnp.where(key <= qry, st, NEG)

    def produce(which, blk, hh, diagonal):
        buf, mx = bufs[which]
        k0 = pl.multiple_of(blk * tk, tk)
        if not diagonal:
            st = _dot_nt(k_aug(k0, tk, hh), q_aug[hh])
            buf[hh] = st
            mx[hh] = jnp.max(st, axis=0, keepdims=True)
            return
        st_lo = _dot_nt(k_aug(k0, half, hh), q_aug[hh])
        st_lo = jnp.concatenate([causal(st_lo[:, lo]), st_lo[:, hi]], axis=1)
        st_hi = causal(_dot_nt(k_aug(k0 + half, half, hh), q_aug[hh][hi]))
        buf[hh, lo, :] = st_lo
        buf[hh, hi, hi] = st_hi
        max_lo = jnp.max(st_lo, axis=0, keepdims=True)
        max_hi = jnp.max(st_hi, axis=0, keepdims=True)
        mx[hh] = jnp.concatenate([max_lo[:, lo], jnp.maximum(max_lo[:, hi], max_hi)], axis=1)

    def consume(which, blk, hh, diagonal):
        buf, mx = bufs[which]
        k0 = pl.multiple_of(blk * tk, tk)
        m_old = m_ref[hh]
        m_new = jnp.maximum(m_old, mx[hh])
        alpha = jnp.exp2(m_old - m_new)
        vrows = slice(hh * FOX_VROWS, (hh + 1) * FOX_VROWS)
        if not diagonal:
            p = jnp.exp2(buf[hh] - m_new).astype(BF16)
            acc_ref[hh] = alpha * acc_ref[hh] + _dot(vt_ref[vrows, pl.ds(k0, tk)], p)
        else:
            p_lo = jnp.exp2(buf[hh, lo, :] - m_new).astype(BF16)
            p_hi = jnp.exp2(buf[hh, hi, hi] - m_new[:, hi]).astype(BF16)
            pv_lo = _dot(vt_ref[vrows, pl.ds(k0, half)], p_lo)
            pv_hi = _dot(vt_ref[vrows, pl.ds(k0 + half, half)], p_hi)
            acc_ref[hh] = alpha * acc_ref[hh] + jnp.concatenate(
                [pv_lo[:, lo], pv_lo[:, hi] + pv_hi], axis=1)
        m_ref[hh] = m_new

    def stage(cur, cur_blk, nxt=None, nxt_blk=None, nxt_diagonal=False, cur_diagonal=False):
        for hh in heads:
            if nxt is not None:
                produce(nxt, nxt_blk, hh, nxt_diagonal)
            if hh > 0:
                consume(cur, cur_blk, hh - 1, cur_diagonal)
        consume(cur, cur_blk, heads[-1], cur_diagonal)

    @pl.when(i == 0)
    def _():
        for hh in heads:
            produce("a", 0, hh, True)
        stage("a", 0, cur_diagonal=True)

    @pl.when(i > 0)
    def _():
        for hh in heads:
            produce("a", 0, hh, False)

    def pair(t, carry):
        b0 = 2 * t
        stage("a", b0, "b", b0 + 1)
        stage("b", b0 + 1, "a", b0 + 2)
        return carry

    n_pairs = lax.shift_right_logical(jnp.maximum(i - 1, 0), 1)
    lax.fori_loop(0, n_pairs, pair, 0)

    @pl.when((i & 1) == 1)
    def _():
        stage("a", i - 1, "b", i, True)
        stage("b", i, cur_diagonal=True)

    @pl.when(jnp.logical_and((i & 1) == 0, i > 0))
    def _():
        stage("a", i - 2, "b", i - 1)
        stage("b", i - 1, "a", i, True)
        stage("a", i, cur_diagonal=True)

    for pr in range(FOX_PAIRS):
        outs = []
        for hh in (2 * pr, 2 * pr + 1):
            acc = acc_ref[hh]
            outs.append(acc[:FOX_DH] * (1.0 / acc[FOX_DH:FOX_DH + 1]))
        o_ref[:, pr * LANES:(pr + 1) * LANES] = jnp.concatenate(outs, axis=0).T.astype(BF16)


def _fox_call(proj, kx, vt, frow, batch, seq):
    t = proj.shape[0]
    tq = FOX_TQ
    nq = seq // tq
    width = FOX_PAIRS * LANES
    groups = FOX_WIDTH // width
    nh = 2 * FOX_PAIRS
    return pl.pallas_call(
        _fox_kernel,
        grid=(batch, groups, nq),
        in_specs=[
            pl.BlockSpec((tq, width), lambda b, hp, i: (b * nq + i, COL_QF // width + hp)),
            pl.BlockSpec((seq, width), lambda b, hp, i: (b, COL_KF // width + hp),
                         pipeline_mode=pl.Buffered(1)),
            pl.BlockSpec((seq, width), lambda b, hp, i: (b, hp), pipeline_mode=pl.Buffered(1)),
            pl.BlockSpec((nh * FOX_VROWS, seq), lambda b, hp, i: (hp, b),
                         pipeline_mode=pl.Buffered(1)),
            pl.BlockSpec((1, nh, 1, seq), lambda b, hp, i: (b, hp, 0, 0)),
        ],
        out_specs=pl.BlockSpec((tq, width), lambda b, hp, i: (b * nq + i, hp)),
        out_shape=jax.ShapeDtypeStruct((t, FOX_WIDTH), BF16),
        scratch_shapes=[
            pltpu.VMEM((nh, 1, tq), F32),
            pltpu.VMEM((nh, FOX_VROWS, tq), F32),
            pltpu.VMEM((nh, FOX_TK, tq), F32),
            pltpu.VMEM((nh, FOX_TK, tq), F32),
            pltpu.VMEM((nh, 1, tq), F32),
            pltpu.VMEM((nh, 1, tq), F32),
        ],
        compiler_params=_params("parallel", "parallel", "arbitrary"),
        name="fox",
    )(proj, proj, kx, vt, frow)


POST_TM = 512
POST_SUB = 2


def _post_kernel(og_ref, of_ref, x_ref, mod_ref, pg_ref, mg_ref, wo_ref, wrh_ref, wrl_ref, br_ref,
                 x1_ref, h2_ref, idx_ref, rank_ref, gate_ref, cnt_ref, car_ref):
    @pl.when(pl.program_id(0) == 0)
    def _():
        car_ref[...] = jnp.zeros_like(car_ref)

    ys = []
    for s in range(POST_SUB):
        rows = slice(s * POST_TM, (s + 1) * POST_TM)
        ys.append(_dot(og_ref[rows, :], wo_ref[:GLA_WIDTH, :])
                  + _dot(of_ref[rows, :], wo_ref[GLA_WIDTH:, :]))
    h2s = [_post_norm(s, ys[s], x_ref, mod_ref, pg_ref, mg_ref, x1_ref, h2_ref)
           for s in range(POST_SUB)]
    logits = jnp.concatenate([_post_logits(h2, wrh_ref, wrl_ref, br_ref) for h2 in h2s], axis=0)
    _post_topk(logits, idx_ref, rank_ref, gate_ref, cnt_ref, car_ref)


def _post_norm(s, y, x_ref, mod_ref, pg_ref, mg_ref, x1_ref, h2_ref):
    tm = POST_TM
    rows = slice(s * tm, (s + 1) * tm)
    x1 = x_ref[rows, :] + mod_ref[0, 2:3, :] * (_rms(y) * pg_ref[...])
    x1_ref[rows, :] = x1
    h2 = _rms(x1) * mg_ref[...]
    h2 = h2 * (1.0 + mod_ref[0, 4:5, :]) + mod_ref[0, 3:4, :]
    for j in range(ROW_TILES):
        h2_ref[pl.ds(s * tm * ROW_TILES + j, tm, stride=ROW_TILES), :] = h2[:, j * LANES:(j + 1) * LANES]

    return _split2(h2)


def _post_logits(h2_split, wrh_ref, wrl_ref, br_ref):
    hh, hl = h2_split
    return _dot(hh, wrh_ref[...]) + (_dot(hh, wrl_ref[...]) + _dot(hl, wrh_ref[...])) + br_ref[...]


def _post_topk(logits, idx_ref, rank_ref, gate_ref, cnt_ref, car_ref):
    tm = POST_TM
    n = logits.shape[0]
    lane = lax.broadcasted_iota(jnp.int32, (n, LANES), 1).astype(F32)
    work = jnp.where(lane < N_EXPERTS, logits, -jnp.inf)
    vals, idxs = [], []
    for _ in range(TOP_K):
        mx = jnp.max(work, axis=-1, keepdims=True)
        ix = jnp.min(jnp.where(work == mx, lane, float(LANES)), axis=-1, keepdims=True)
        vals.append(mx)
        idxs.append(ix)
        work = jnp.where(lane == ix, -jnp.inf, work)
    es = [jnp.exp(v - vals[0]) for v in vals]
    den = es[0] + es[1] + es[2] + es[3]

    onehot = jnp.zeros((n, LANES), F32)
    for ix in idxs:
        onehot = onehot + jnp.where(lane == ix, 1.0, 0.0)
    row = lax.broadcasted_iota(jnp.int32, (tm, tm), 0)
    col = lax.broadcasted_iota(jnp.int32, (tm, tm), 1)
    strict_b = jnp.where(col < row, 1.0, 0.0).astype(BF16)
    carry = car_ref[...]
    prefixes = []
    for s in range(n // tm):
        part = onehot[s * tm:(s + 1) * tm]
        prefixes.append(_dot(strict_b, part.astype(BF16)) + carry)
        carry = carry + jnp.sum(part, axis=0, keepdims=True)
    prefix = jnp.concatenate(prefixes, axis=0)
    car_ref[...] = carry
    cnt_ref[...] = carry

    idx_o = jnp.zeros((n, LANES), F32)
    rank_o = jnp.zeros((n, LANES), F32)
    gate_o = jnp.zeros((n, LANES), F32)
    for kk in range(TOP_K):
        rk = jnp.sum(jnp.where(lane == idxs[kk], prefix, 0.0), axis=-1, keepdims=True)
        sel = lane == float(kk)
        idx_o = jnp.where(sel, idxs[kk], idx_o)
        rank_o = jnp.where(sel, rk, rank_o)
        gate_o = jnp.where(sel, es[kk] / den, gate_o)
    idx_ref[...] = idx_o.astype(jnp.int32)
    rank_ref[...] = rank_o.astype(jnp.int32)
    gate_ref[...] = gate_o


def _post_call(o_gla, o_fox, x2, mod3, post_gain, mlp_gain, w_o, wr_hi, wr_lo, br_pad, seq):
    t, d = x2.shape
    tm = POST_TM * POST_SUB
    per_b = seq // tm
    return pl.pallas_call(
        _post_kernel,
        grid=(t // tm,),
        in_specs=[
            pl.BlockSpec((tm, GLA_WIDTH), lambda i: (i, 0)),
            pl.BlockSpec((tm, FOX_WIDTH), lambda i: (i, 0)),
            pl.BlockSpec((tm, d), lambda i: (i, 0)),
            pl.BlockSpec((1, N_MOD, d), lambda i: (i // per_b, 0, 0)),
            pl.BlockSpec((1, d), lambda i: (0, 0)),
            pl.BlockSpec((1, d), lambda i: (0, 0)),
            pl.BlockSpec((GLA_WIDTH + FOX_WIDTH, d), lambda i: (0, 0)),
            pl.BlockSpec((d, LANES), lambda i: (0, 0)),
            pl.BlockSpec((d, LANES), lambda i: (0, 0)),
            pl.BlockSpec((1, LANES), lambda i: (0, 0)),
        ],
        out_specs=[
            pl.BlockSpec((tm, d), lambda i: (i, 0)),
            pl.BlockSpec((tm * ROW_TILES, LANES), lambda i: (i, 0)),
            pl.BlockSpec((tm, LANES), lambda i: (i, 0)),
            pl.BlockSpec((tm, LANES), lambda i: (i, 0)),
            pl.BlockSpec((tm, LANES), lambda i: (i, 0)),
            pl.BlockSpec((1, LANES), lambda i: (0, 0)),
        ],
        out_shape=[
            jax.ShapeDtypeStruct((t, d), F32),
            jax.ShapeDtypeStruct((t * ROW_TILES, LANES), F32),
            jax.ShapeDtypeStruct((t, LANES), jnp.int32),
            jax.ShapeDtypeStruct((t, LANES), jnp.int32),
            jax.ShapeDtypeStruct((t, LANES), F32),
            jax.ShapeDtypeStruct((1, LANES), F32),
        ],
        scratch_shapes=[pltpu.VMEM((1, LANES), F32)],
        compiler_params=_params("arbitrary"),
        name="post",
    )(o_gla, o_fox, x2, mod3, post_gain, mlp_gain, w_o, wr_hi, wr_lo, br_pad)


DISP_TM = 512
DISP_SLOTS = 3
ISSUE_UNROLL = 8


def _row_copy(src_ref, src_row, dst_ref, dst_row, sem):
    return pltpu.make_async_copy(
        src_ref.at[pl.ds(pl.multiple_of(src_row * ROW_TILES, ROW_TILES), ROW_TILES)],
        dst_ref.at[pl.ds(pl.multiple_of(dst_row * ROW_TILES, ROW_TILES), ROW_TILES)],
        sem)


def _dispatch_kernel(pend_ref, pos_ref, h_hbm, xs_ref, zero_ref, hbuf_ref, sem, in_sem, out_sem):
    tm = DISP_TM
    rows = tm * ROW_TILES
    zrows = MOE_TM * ROW_TILES
    i = pl.program_id(0)
    last = pl.num_programs(0) - 1
    slot = lax.rem(i, DISP_SLOTS)
    nxt = lax.rem(i + 1, DISP_SLOTS)

    def load(step, s):
        return pltpu.make_async_copy(
            h_hbm.at[pl.ds(pl.multiple_of(step * rows, rows), rows)], hbuf_ref.at[s], in_sem.at[s])

    def drain(s):
        for _ in range(TOP_K):
            pltpu.make_async_copy(hbuf_ref.at[s], xs_ref.at[pl.ds(0, rows)], out_sem.at[s]).wait()

    @pl.when(i == 0)
    def _():
        load(0, 0).start()
        zero_ref[...] = jnp.zeros_like(zero_ref)

        def last_tile(e):
            start = pl.multiple_of((pend_ref[e] - MOE_TM) * ROW_TILES, zrows)
            return pltpu.make_async_copy(zero_ref, xs_ref.at[pl.ds(start, zrows)], sem)

        def nonempty(e):
            return pend_ref[e] > jnp.where(e == 0, 0, pend_ref[jnp.maximum(e - 1, 0)])

        def clear(e, carry):
            @pl.when(nonempty(e))
            def _():
                last_tile(e).start()
            return carry

        def clear_wait(e, carry):
            @pl.when(nonempty(e))
            def _():
                last_tile(e).wait()
            return carry

        lax.fori_loop(0, N_EXPERTS, clear, 0)
        lax.fori_loop(0, N_EXPERTS, clear_wait, 0)

        def tail_tile(j):
            return pltpu.make_async_copy(
                zero_ref, xs_ref.at[pl.ds(pl.multiple_of(j * zrows, zrows), zrows)], sem)

        def tail(j, carry):
            tail_tile(j).start()
            return carry

        def tail_wait(j, carry):
            tail_tile(j).wait()
            return carry

        first_unused = lax.div(pend_ref[N_EXPERTS - 1], MOE_TM)
        n_tiles = xs_ref.shape[0] // zrows
        lax.fori_loop(first_unused, n_tiles, tail, 0)
        lax.fori_loop(first_unused, n_tiles, tail_wait, 0)

    @pl.when(i >= DISP_SLOTS - 1)
    def _():
        drain(nxt)

    @pl.when(i < last)
    def _():
        load(i + 1, nxt).start()

    load(i, slot).wait()

    def issue(g, carry):
        for u in range(ISSUE_UNROLL):
            t = g * ISSUE_UNROLL + u
            for kk in range(TOP_K):
                _row_copy(hbuf_ref.at[slot], t, xs_ref, pos_ref[0, 0, t * TOP_K + kk],
                          out_sem.at[slot]).start(priority=kk % 2)
        return carry

    lax.fori_loop(0, tm // ISSUE_UNROLL, issue, 0)

    @pl.when(i == last)
    def _():
        for back in range(DISP_SLOTS - 1):
            @pl.when(i >= back)
            def _():
                drain(lax.rem(i - back + DISP_SLOTS, DISP_SLOTS))


def _dispatch_call(pend, pos3, h2, n_tiles):
    tm = DISP_TM
    n_steps = pos3.shape[0]
    grid_spec = pltpu.PrefetchScalarGridSpec(
        num_scalar_prefetch=1,
        grid=(n_steps,),
        in_specs=[
            pl.BlockSpec((1, 1, tm * TOP_K), lambda i, pend: (i, 0, 0), memory_space=pltpu.SMEM),
            pl.BlockSpec(memory_space=pl.ANY),
        ],
        out_specs=pl.BlockSpec(memory_space=pl.ANY),
        scratch_shapes=[
            pltpu.VMEM((MOE_TM * ROW_TILES, LANES), F32),
            pltpu.VMEM((DISP_SLOTS, tm * ROW_TILES, LANES), F32),
            pltpu.SemaphoreType.DMA,
            pltpu.SemaphoreType.DMA((DISP_SLOTS,)),
            pltpu.SemaphoreType.DMA((DISP_SLOTS,)),
        ],
    )
    return pl.pallas_call(
        _dispatch_kernel,
        grid_spec=grid_spec,
        out_shape=jax.ShapeDtypeStruct((n_tiles * MOE_TM * ROW_TILES, LANES), F32),
        compiler_params=_params("arbitrary"),
        name="dispatch",
    )(pend, pos3, h2)


MOE_TM = 512


def _experts_kernel(te_ref, nu_ref, x_ref, win_ref, bin_ref, wout_ref, bout_ref, y_ref,
                    wib_ref, wob_ref):
    tm = MOE_TM
    i = pl.program_id(0)
    used = i < nu_ref[0]

    @pl.when(jnp.logical_not(used))
    def _():
        y_ref[...] = jnp.zeros_like(y_ref)

    fresh = jnp.logical_or(i == 0, te_ref[i] != te_ref[jnp.maximum(i - 1, 0)])

    @pl.when(jnp.logical_and(used, fresh))
    def _():
        wib_ref[...] = win_ref[0].astype(BF16)
        wob_ref[...] = wout_ref[0].astype(BF16)

    @pl.when(used)
    def _():
        x = jnp.concatenate([x_ref[pl.ds(j, tm, stride=ROW_TILES), :] for j in range(ROW_TILES)],
                            axis=1).astype(BF16)
        u = _dot(x, wib_ref[...]) + bin_ref[0]
        glu = jnp.minimum(u[:, :D_FF], SWIGLU_LIMIT)
        lin = jnp.clip(u[:, D_FF:], -SWIGLU_LIMIT, SWIGLU_LIMIT)
        a = glu * jax.nn.sigmoid(SWIGLU_ALPHA * glu) * (lin + 1.0)
        y = _dot(a.astype(BF16), wob_ref[...]) + bout_ref[0]
        for j in range(ROW_TILES):
            y_ref[pl.ds(j, tm, stride=ROW_TILES), :] = y[:, j * LANES:(j + 1) * LANES]


def _experts_call(tile_expert, n_used, xs, w_in, b_in, w_out, b_out):
    tm = MOE_TM
    n_tiles = tile_expert.shape[0]
    d = D_MODEL

    def tile(i, te, nu):
        return (jnp.minimum(i, nu[0] - 1), 0)

    grid_spec = pltpu.PrefetchScalarGridSpec(
        num_scalar_prefetch=2,
        grid=(n_tiles,),
        in_specs=[
            pl.BlockSpec((tm * ROW_TILES, LANES), tile),
            pl.BlockSpec((1, d, 2 * D_FF), lambda i, te, nu: (te[i], 0, 0)),
            pl.BlockSpec((1, 1, 2 * D_FF), lambda i, te, nu: (te[i], 0, 0)),
            pl.BlockSpec((1, D_FF, d), lambda i, te, nu: (te[i], 0, 0)),
            pl.BlockSpec((1, 1, d), lambda i, te, nu: (te[i], 0, 0)),
        ],
        out_specs=pl.BlockSpec((tm * ROW_TILES, LANES), lambda i, te, nu: (i, 0)),
        scratch_shapes=[
            pltpu.VMEM((d, 2 * D_FF), BF16),
            pltpu.VMEM((D_FF, d), BF16),
        ],
    )
    return pl.pallas_call(
        _experts_kernel,
        grid_spec=grid_spec,
        out_shape=jax.ShapeDtypeStruct(xs.shape, F32),
        compiler_params=_params("arbitrary"),
        name="experts",
    )(tile_expert, n_used, xs, w_in, b_in, w_out, b_out)


COMB_TM = 256


def _combine_kernel(pos_ref, posn_ref, y_ref, gate_ref, x1_ref, mod_ref, pg_ref, o_ref, buf_ref, sem):
    tm = COMB_TM
    i = pl.program_id(0)
    slot = i & 1

    def gather(p_ref, s):
        def issue(g, carry):
            for u in range(ISSUE_UNROLL):
                t = g * ISSUE_UNROLL + u
                for kk in range(TOP_K):
                    _row_copy(y_ref, p_ref[0, 0, t * TOP_K + kk], buf_ref.at[s, kk], t,
                              sem.at[s]).start(priority=kk % 2)
            return carry

        lax.fori_loop(0, tm // ISSUE_UNROLL, issue, 0)

    @pl.when(i == 0)
    def _():
        gather(pos_ref, 0)

    @pl.when(i + 1 < pl.num_programs(0))
    def _():
        gather(posn_ref, 1 - slot)

    for kk in range(TOP_K):
        pltpu.make_async_copy(y_ref.at[pl.ds(0, tm * ROW_TILES)], buf_ref.at[slot, kk],
                              sem.at[slot]).wait()

    gates = gate_ref[...]
    acc = None
    for kk in range(TOP_K):
        rows = jnp.concatenate(
            [buf_ref[slot, kk, pl.ds(j, tm, stride=ROW_TILES), :] for j in range(ROW_TILES)], axis=1)
        term = rows * gates[:, kk:kk + 1]
        acc = term if acc is None else acc + term
    o_ref[...] = x1_ref[...] + mod_ref[0, 5:6, :] * (_rms(acc) * pg_ref[...])


def _combine_call(pos3, y, gates, x1, mod3, post_gain, seq):
    t, d = x1.shape
    tm = COMB_TM
    per_b = seq // tm
    n_steps = t // tm
    return pl.pallas_call(
        _combine_kernel,
        grid=(n_steps,),
        in_specs=[
            pl.BlockSpec((1, 1, tm * TOP_K), lambda i: (i, 0, 0), memory_space=pltpu.SMEM),
            pl.BlockSpec((1, 1, tm * TOP_K), lambda i: (jnp.minimum(i + 1, n_steps - 1), 0, 0),
                         memory_space=pltpu.SMEM),
            pl.BlockSpec(memory_space=pl.ANY),
            pl.BlockSpec((tm, LANES), lambda i: (i, 0)),
            pl.BlockSpec((tm, d), lambda i: (i, 0)),
            pl.BlockSpec((1, N_MOD, d), lambda i: (i // per_b, 0, 0)),
            pl.BlockSpec((1, d), lambda i: (0, 0)),
        ],
        out_specs=pl.BlockSpec((tm, d), lambda i: (i, 0)),
        out_shape=jax.ShapeDtypeStruct((t, d), F32),
        scratch_shapes=[
            pltpu.VMEM((2, TOP_K, tm * ROW_TILES, LANES), F32),
            pltpu.SemaphoreType.DMA((2,)),
        ],
        compiler_params=_params("arbitrary"),
        name="combine",
    )(pos3, pos3, y, gates, x1, mod3, post_gain)


def _pad_cols(w, n):
    return jnp.pad(w, ((0, 0), (0, n - w.shape[1])))


def kernel(x, c, w_ada, b_ada, attn_pre_gain, attn_post_gain, w_in, w_gk2, b_gk, gla_norm_gain, b_f,
           w_o, mlp_pre_gain, mlp_post_gain, w_router, b_router, w_e_in, b_e_in, w_e_out, b_e_out):
    batch, seq, d = x.shape
    t = batch * seq
    x2 = x.reshape(t, d)

    o = 0
    cols = {}
    for name, width in (("qg", GLA_QK), ("kg", GLA_QK), ("vg", GLA_WIDTH), ("gg", GLA_WIDTH),
                        ("zg", GLA_LOWRANK), ("qf", FOX_WIDTH), ("kf", FOX_WIDTH), ("vf", FOX_WIDTH),
                        ("ff", FOX_HEADS)):
        cols[name] = w_in[:, o:o + width]
        o += width
    gla_cols = 2 * GLA_QK + 2 * GLA_WIDTH
    fox_start = gla_cols + GLA_LOWRANK
    w_main = jnp.concatenate([w_in[:, :gla_cols], w_in[:, fox_start:fox_start + 2 * FOX_WIDTH]],
                             axis=1).astype(BF16)
    w_vt = cols["vf"].T.astype(BF16)
    w_small = _pad_cols(jnp.concatenate([cols["ff"], cols["zg"]], axis=1), LANES)
    ws_hi, ws_lo = _split2(w_small)
    wg = jnp.zeros((LANES, GLA_QK), F32).at[SMALL_ZG:SMALL_ZG + GLA_LOWRANK].set(w_gk2)
    wg_hi, wg_lo = _split2(wg)
    bf_pad = jnp.zeros((1, LANES), F32).at[0, SMALL_FF:SMALL_FF + FOX_HEADS].set(b_f)
    wr_hi, wr_lo = _split2(_pad_cols(w_router, LANES))
    br_pad = _pad_cols(b_router.reshape(1, N_EXPERTS), LANES)

    c_pad = jnp.pad(c, ((0, SUBLANES - batch % SUBLANES if batch % SUBLANES else 0), (0, 0)))
    mod = _mod_call(c_pad, w_ada, b_ada.reshape(1, -1))
    mod3 = mod[:batch].reshape(batch, N_MOD, d)

    proj, vt, small = _inproj_call(x2, mod3, attn_pre_gain.reshape(1, d), w_main, w_vt, ws_hi, ws_lo,
                                   seq)
    o_gla, kx, frow = _gla_call(proj, small, wg_hi, wg_lo, b_gk.reshape(1, -1),
                                gla_norm_gain.reshape(1, -1), bf_pad, batch, seq)
    o_fox = _fox_call(proj, kx, vt, frow, batch, seq)

    x1, h2, idx_w, rank_w, gate_w, cnt = _post_call(
        o_gla, o_fox, x2, mod3, attn_post_gain.reshape(1, d), mlp_pre_gain.reshape(1, d),
        w_o.astype(BF16), wr_hi, wr_lo, br_pad, seq)

    tm = MOE_TM
    n_tiles = (t * TOP_K) // tm + N_EXPERTS
    counts = cnt[0, :N_EXPERTS].astype(jnp.int32)
    padded = ((counts + tm - 1) // tm) * tm
    pend = jnp.cumsum(padded)
    pstart = pend - padded
    experts = jnp.arange(N_EXPERTS, dtype=jnp.int32)
    first_slot = jnp.sum(jnp.where(idx_w[:, :TOP_K, None] == experts, pstart.astype(jnp.int32), 0),
                         axis=-1)
    pos = first_slot + rank_w[:, :TOP_K]
    n_used = (pend[-1] // tm).astype(jnp.int32).reshape(1)
    tile_start = jnp.arange(n_tiles, dtype=jnp.int32) * tm
    tile_expert = jnp.minimum(
        jnp.sum((pend[None, :] <= tile_start[:, None]).astype(jnp.int32), axis=1), N_EXPERTS - 1)

    pos_d = pos.reshape(t // DISP_TM, 1, DISP_TM * TOP_K)
    xs = _dispatch_call(pend.astype(jnp.int32), pos_d, h2, n_tiles)
    y = _experts_call(tile_expert, n_used, xs, w_e_in, b_e_in.reshape(N_EXPERTS, 1, -1), w_e_out,
                      b_e_out.reshape(N_EXPERTS, 1, -1))
    pos_c = pos.reshape(t // COMB_TM, 1, COMB_TM * TOP_K)
    out = _combine_call(pos_c, y, gate_w, x1, mod3, mlp_post_gain.reshape(1, d), seq)
    return out.reshape(batch, seq, d)
```

```python
import jax
import jax.numpy as jnp
from jax import lax
from jax.experimental import pallas as pl
from jax.experimental.pallas import tpu as pltpu

F32 = jnp.float32
BF16 = jnp.bfloat16

D_MODEL = 1024
N_MOD = 6
GLA_HEADS = 4
GLA_DK = 64
GLA_DV = 128
GLA_QK = GLA_HEADS * GLA_DK
GLA_WIDTH = GLA_HEADS * GLA_DV
GLA_LOWRANK = 16
GLA_CHUNK = 64
GLA_GATE_NORM = 16.0
FOX_HEADS = 8
FOX_DH = 64
FOX_WIDTH = FOX_HEADS * FOX_DH
N_EXPERTS = 32
TOP_K = 4
D_FF = D_MODEL
SWIGLU_ALPHA = 1.702
SWIGLU_LIMIT = 7.0
EPS = 1e-6

LANES = 128
SUBLANES = 8
ROW_TILES = D_MODEL // LANES

COL_QG, COL_KG, COL_VG, COL_GG = 0, 256, 512, 1024
COL_QF, COL_KF = 1536, 2048
MAIN_COLS = 2560
LOG2E = 1.4426950408889634
FOX_Q_SCALE = (FOX_DH ** -0.5) * LOG2E
FOX_BIAS_LANES = 6
BF16_SUBLANES = 16
FOX_VROWS = FOX_DH + BF16_SUBLANES
SMALL_FF, SMALL_ZG = 0, 8

NEG = -1e30
VMEM_LIMIT = 56 * 1024 * 1024


def _dot(a, b):
    return jnp.dot(a, b, preferred_element_type=F32)


def _dot_nt(a, b):
    return lax.dot_general(a, b, (((1,), (1,)), ((), ())), preferred_element_type=F32)


def _dot_tn(a, b):
    return lax.dot_general(a, b, (((0,), (0,)), ((), ())), preferred_element_type=F32)


def _split2(a):
    hi = a.astype(BF16)
    lo = (a - hi.astype(F32)).astype(BF16)
    return hi, lo


def _split3(a):
    hi = a.astype(BF16)
    r = a - hi.astype(F32)
    mid = r.astype(BF16)
    lo = (r - mid.astype(F32)).astype(BF16)
    return hi, mid, lo


def _log_sigmoid(x):
    return jnp.minimum(x, 0.0) - jnp.log1p(jnp.exp(-jnp.abs(x)))


def _rms(x):
    return x * lax.rsqrt(jnp.mean(x * x, axis=-1, keepdims=True) + EPS)


def _params(*sem):
    return pltpu.CompilerParams(dimension_semantics=sem, vmem_limit_bytes=VMEM_LIMIT)


def _mod_kernel(c_ref, w_ref, b_ref, o_ref):
    c = c_ref[...]
    s = c * jax.nn.sigmoid(c)
    sh, sl = _split2(s)
    wh, wl = _split2(w_ref[...])
    o_ref[...] = _dot(sh, wh) + (_dot(sh, wl) + _dot(sl, wh)) + b_ref[...]


def _mod_call(c_pad, w_ada, b_ada):
    rows, d = c_pad.shape
    n = w_ada.shape[1]
    tn = D_MODEL
    return pl.pallas_call(
        _mod_kernel,
        grid=(n // tn,),
        in_specs=[
            pl.BlockSpec((rows, d), lambda j: (0, 0)),
            pl.BlockSpec((d, tn), lambda j: (0, j)),
            pl.BlockSpec((1, tn), lambda j: (0, j)),
        ],
        out_specs=pl.BlockSpec((rows, tn), lambda j: (0, j)),
        out_shape=jax.ShapeDtypeStruct((rows, n), F32),
        compiler_params=_params("parallel"),
        name="mod",
    )(c_pad, w_ada, b_ada)


INPROJ_TM = 512
INPROJ_SUB = 2


def _inproj_kernel(x_ref, mod_ref, g_ref, w_ref, wvt_ref, wsh_ref, wsl_ref, proj_ref, vt_ref, small_ref):
    subs = [pl.ds(s * INPROJ_TM, INPROJ_TM) for s in range(INPROJ_SUB)]
    hs = []
    for r in subs:
        h = _rms(x_ref[r, :]) * g_ref[...]
        h = h * (1.0 + mod_ref[0, 1:2, :]) + mod_ref[0, 0:1, :]
        hs.append(_split2(h))
    for r, (hh, hl) in zip(subs, hs):
        _inproj_sub(hh, hl, w_ref, wvt_ref, wsh_ref, wsl_ref, proj_ref.at[r], vt_ref.at[:, r],
                    small_ref.at[r])


def _inproj_sub(hh, hl, w_ref, wvt_ref, wsh_ref, wsl_ref, proj_ref, vt_ref, small_ref):
    cw = FOX_WIDTH
    for j in range(MAIN_COLS // cw):
        r = _dot(hh, w_ref[:, j * cw:(j + 1) * cw])
        if j * cw == COL_QF:
            r = r * FOX_Q_SCALE
        proj_ref[:, j * cw:(j + 1) * cw] = r.astype(BF16)
    vt = _dot_nt(wvt_ref[...], hh).astype(BF16)
    ones = jnp.ones((FOX_VROWS - FOX_DH, vt.shape[1]), BF16)
    for h in range(FOX_HEADS):
        vt_ref[h * FOX_VROWS:h * FOX_VROWS + FOX_DH, :] = vt[h * FOX_DH:(h + 1) * FOX_DH, :]
        vt_ref[h * FOX_VROWS + FOX_DH:(h + 1) * FOX_VROWS, :] = ones
    a = _dot(hh, wsl_ref[...])
    small_ref[...] = a[:, :LANES] + (a[:, LANES:] + _dot(hl, wsh_ref[...]))


def _inproj_call(x2, mod3, gain, w_main, w_vt, ws_hi, ws_lo, seq):
    ws_hilo = jnp.concatenate([ws_hi, ws_lo], axis=1)
    t, d = x2.shape
    tm = INPROJ_TM * INPROJ_SUB
    per_b = seq // tm
    return pl.pallas_call(
        _inproj_kernel,
        grid=(t // tm,),
        in_specs=[
            pl.BlockSpec((tm, d), lambda i: (i, 0)),
            pl.BlockSpec((1, N_MOD, d), lambda i: (i // per_b, 0, 0)),
            pl.BlockSpec((1, d), lambda i: (0, 0)),
            pl.BlockSpec((d, MAIN_COLS), lambda i: (0, 0)),
            pl.BlockSpec((FOX_WIDTH, d), lambda i: (0, 0)),
            pl.BlockSpec((d, LANES), lambda i: (0, 0)),
            pl.BlockSpec((d, 2 * LANES), lambda i: (0, 0)),
        ],
        out_specs=[
            pl.BlockSpec((tm, MAIN_COLS), lambda i: (i, 0)),
            pl.BlockSpec((FOX_HEADS * FOX_VROWS, tm), lambda i: (0, i)),
            pl.BlockSpec((tm, LANES), lambda i: (i, 0)),
        ],
        out_shape=[
            jax.ShapeDtypeStruct((t, MAIN_COLS), BF16),
            jax.ShapeDtypeStruct((FOX_HEADS * FOX_VROWS, t), BF16),
            jax.ShapeDtypeStruct((t, LANES), F32),
        ],
        compiler_params=_params("parallel"),
        name="inproj",
    )(x2, mod3, gain, w_main, w_vt, ws_hi, ws_hilo)


GLA_TM = 512
GLA_SUB = 4
GLA_BLK = 256


def _gla_kernel(q_ref, k_ref, v_ref, g_ref, small_ref, wgh_ref, wgl_ref, bgk_ref, gain_ref, bf_ref,
                sel_ref, ones_ref, o_ref, kx_ref, frow_ref, st_ref, fcar_ref, oacc_ref):
    @pl.when(pl.program_id(1) == 0)
    def _():
        st_ref[...] = jnp.zeros_like(st_ref)
        fcar_ref[...] = jnp.zeros_like(fcar_ref)

    subs = [pl.ds(s * GLA_TM, GLA_TM) for s in range(GLA_SUB)]
    fronts = [_gla_sub_front(q_ref.at[r], k_ref.at[r], v_ref.at[r], small_ref.at[r],
                             wgh_ref, wgl_ref, bgk_ref, oacc_ref.at[r]) for r in subs]
    for r, front in zip(subs, fronts):
        _fox_gates(front[0], front[5], bf_ref, sel_ref, ones_ref, kx_ref.at[r],
                   frow_ref.at[:, :, :, r], fcar_ref)
    for r, front in zip(subs, fronts):
        _gla_sub_state(front, v_ref.at[r], st_ref, oacc_ref.at[r])
    for r in subs:
        _gla_sub_out(g_ref.at[r], gain_ref, o_ref.at[r], oacc_ref.at[r])


def _gla_sub_front(q_ref, k_ref, v_ref, small_ref, wgh_ref, wgl_ref, bgk_ref, oacc_ref):
    tm = GLA_TM
    nc = tm // GLA_CHUNK
    small = small_ref[...]
    sh, sl = _split2(small)
    gk = _dot(sh, wgh_ref[...]) + (_dot(sh, wgl_ref[...]) + _dot(sl, wgh_ref[...])) + bgk_ref[...]
    log_a = _log_sigmoid(gk) * (1.0 / GLA_GATE_NORM)

    blk = GLA_BLK
    blocks = [slice(j * blk, (j + 1) * blk) for j in range(tm // blk)]
    row = lax.broadcasted_iota(jnp.int32, (blk, blk), 0)
    col = lax.broadcasted_iota(jnp.int32, (blk, blk), 1)
    chunk_shift = GLA_CHUNK.bit_length() - 1
    same = (row >> chunk_shift) == (col >> chunk_shift)
    lower = col <= row
    tri = jnp.logical_and(same, lower)
    tri_b = jnp.where(tri, 1.0, 0.0).astype(BF16)
    lower_b = jnp.where(lower, 1.0, 0.0).astype(BF16)

    lh, ll = _split2(log_a)
    b = jnp.concatenate([_dot(tri_b, lh[r]) + _dot(tri_b, ll[r]) for r in blocks], axis=0)
    btot = jnp.concatenate(
        [jnp.broadcast_to(b[(c + 1) * GLA_CHUNK - 1:(c + 1) * GLA_CHUNK, :], (GLA_CHUNK, GLA_QK))
         for c in range(nc)], axis=0)

    q = q_ref[...].astype(F32)
    k = k_ref[...].astype(F32)
    qt = q * jnp.exp(b) * (GLA_DK ** -0.5)
    kt = (k * jnp.exp(-b)).astype(BF16)
    kd = (k * jnp.exp(btot - b)).astype(BF16)

    lane_qk = lax.broadcasted_iota(jnp.int32, (1, GLA_QK), 1)
    head_masks = [jnp.logical_and(lane_qk >= h * GLA_DK, lane_qk < (h + 1) * GLA_DK)
                  for h in range(GLA_HEADS)]

    for h in range(GLA_HEADS):
        cs = slice(h * GLA_DV, (h + 1) * GLA_DV)
        qm = jnp.where(head_masks[h], qt, 0.0).astype(BF16)
        for r in blocks:
            att = jnp.where(tri, _dot_nt(qm[r], kt[r]), 0.0).astype(BF16)
            oacc_ref[r, cs] = _dot(att, v_ref[r, cs])
    return small, qt, kd, btot, head_masks, lower_b


def _gla_sub_state(front, v_ref, st_ref, oacc_ref):
    tm = GLA_TM
    nc = tm // GLA_CHUNK
    _, qt, kd, btot, head_masks, _ = front

    chunks = [slice(c * GLA_CHUNK, (c + 1) * GLA_CHUNK) for c in range(nc)]
    kvts = [_dot_tn(v_ref[rs, :], kd[rs]) for rs in chunks]
    states = []
    st = st_ref[...]
    for c, rs in enumerate(chunks):
        states.append(st.astype(BF16))
        decay = jnp.exp(btot[c * GLA_CHUNK:c * GLA_CHUNK + 1, :])
        new = decay * st
        for h in range(GLA_HEADS):
            new = new + jnp.where(head_masks[h], kvts[c][h * GLA_DV:(h + 1) * GLA_DV, :], 0.0)
        st = new
    st_ref[...] = st
    for c, rs in enumerate(chunks):
        qc = qt[rs]
        qs = jnp.concatenate([jnp.where(head_masks[h], qc, 0.0) for h in range(GLA_HEADS)],
                             axis=0).astype(BF16)
        oi = _dot_nt(qs, states[c])
        for h in range(GLA_HEADS):
            oacc_ref[rs, h * GLA_DV:(h + 1) * GLA_DV] += oi[h * GLA_CHUNK:(h + 1) * GLA_CHUNK, :]


def _gla_sub_out(g_ref, gain_ref, o_ref, oacc_ref):
    for h in range(GLA_HEADS):
        cs = slice(h * GLA_DV, (h + 1) * GLA_DV)
        y = _rms(oacc_ref[:, cs]) * gain_ref[:, cs]
        g = g_ref[:, cs].astype(F32)
        o_ref[:, cs] = (y * (g * jax.nn.sigmoid(g))).astype(BF16)


def _fox_gates(small, lower_b, bf_ref, sel_ref, ones_ref, kx_ref, frow_ref, fcar_ref):
    tm = GLA_TM
    lane = lax.broadcasted_iota(jnp.int32, (1, LANES), 1)
    lf = jnp.where(lane < FOX_HEADS, _log_sigmoid(small + bf_ref[...]), 0.0)
    l1, l2, l3 = _split3(lf)
    carry = fcar_ref[...]
    parts = []
    for j in range(tm // GLA_BLK):
        r = slice(j * GLA_BLK, (j + 1) * GLA_BLK)
        part = _dot(lower_b, l1[r]) + (_dot(lower_b, l2[r]) + _dot(lower_b, l3[r])) + carry
        carry = part[GLA_BLK - 1:GLA_BLK, :]
        parts.append(part)
    fl = jnp.concatenate(parts, axis=0)
    fcar_ref[...] = carry
    fs = fl * LOG2E
    fst = fs.T
    for h in range(FOX_HEADS):
        frow_ref[0, h] = fst[h:h + 1, :]
    n1, n2, n3 = _split3(-fs)
    kx = _dot(n1, sel_ref[0]) + (_dot(n2, sel_ref[1]) + _dot(n3, sel_ref[2])) + ones_ref[...]
    kx_ref[...] = kx.astype(BF16)


def _fox_bias_selectors():
    import numpy as np
    sel = np.zeros((3, LANES, FOX_WIDTH), np.float32)
    ones = np.zeros((1, FOX_WIDTH), np.float32)
    for h in range(FOX_HEADS):
        base = (h // 2) * LANES + (h % 2) * FOX_BIAS_LANES
        for i in range(3):
            sel[i, h, base + i] = 1.0
            ones[0, base + 3 + i] = 1.0
    return jnp.asarray(sel, BF16), jnp.asarray(ones, F32)


def _gla_call(proj, small, wg_hi, wg_lo, b_gk, gain, bf_pad, batch, seq):
    t = proj.shape[0]
    tm = GLA_TM * GLA_SUB
    nb = seq // tm

    def rows(b, i):
        return b * nb + i

    sel, ones = _fox_bias_selectors()
    return pl.pallas_call(
        _gla_kernel,
        grid=(batch, nb),
        in_specs=[
            pl.BlockSpec((tm, GLA_QK), lambda b, i: (rows(b, i), COL_QG // GLA_QK)),
            pl.BlockSpec((tm, GLA_QK), lambda b, i: (rows(b, i), COL_KG // GLA_QK)),
            pl.BlockSpec((tm, GLA_WIDTH), lambda b, i: (rows(b, i), COL_VG // GLA_WIDTH)),
            pl.BlockSpec((tm, GLA_WIDTH), lambda b, i: (rows(b, i), COL_GG // GLA_WIDTH)),
            pl.BlockSpec((tm, LANES), lambda b, i: (rows(b, i), 0)),
            pl.BlockSpec((LANES, GLA_QK), lambda b, i: (0, 0)),
            pl.BlockSpec((LANES, GLA_QK), lambda b, i: (0, 0)),
            pl.BlockSpec((1, GLA_QK), lambda b, i: (0, 0)),
            pl.BlockSpec((1, GLA_WIDTH), lambda b, i: (0, 0)),
            pl.BlockSpec((1, LANES), lambda b, i: (0, 0)),
            pl.BlockSpec((3, LANES, FOX_WIDTH), lambda b, i: (0, 0, 0)),
            pl.BlockSpec((1, FOX_WIDTH), lambda b, i: (0, 0)),
        ],
        out_specs=[
            pl.BlockSpec((tm, GLA_WIDTH), lambda b, i: (rows(b, i), 0)),
            pl.BlockSpec((tm, FOX_WIDTH), lambda b, i: (rows(b, i), 0)),
            pl.BlockSpec((1, FOX_HEADS, 1, tm), lambda b, i: (b, 0, 0, i)),
        ],
        out_shape=[
            jax.ShapeDtypeStruct((t, GLA_WIDTH), BF16),
            jax.ShapeDtypeStruct((t, FOX_WIDTH), BF16),
            jax.ShapeDtypeStruct((batch, FOX_HEADS, 1, seq), F32),
        ],
        scratch_shapes=[
            pltpu.VMEM((GLA_DV, GLA_QK), F32),
            pltpu.VMEM((1, LANES), F32),
            pltpu.VMEM((tm, GLA_WIDTH), F32),
        ],
        compiler_params=_params("parallel", "arbitrary"),
        name="gla",
    )(proj, proj, proj, proj, small, wg_hi, wg_lo, b_gk, gain, bf_pad, sel, ones)


FOX_TQ = 512
FOX_TK = 512
FOX_PAIRS = 4


def _fox_kernel(q_ref, k_ref, kx_ref, vt_ref, frow_ref, o_ref, m_ref, acc_ref, sa_ref, sb_ref,
                ma_ref, mb_ref):
    tq, tk = FOX_TQ, FOX_TK
    i = pl.program_id(2)
    q0 = pl.multiple_of(i * tq, tq)
    lane = lax.broadcasted_iota(jnp.int32, (1, LANES), 1)
    heads = range(2 * FOX_PAIRS)

    def pair_lanes(hh):
        return slice((hh // 2) * LANES, (hh // 2 + 1) * LANES)

    q_aug = []
    for hh in heads:
        head_lanes = (lane < FOX_DH) if hh % 2 == 0 else (lane >= FOX_DH)
        qm = jnp.where(head_lanes, q_ref[:, pair_lanes(hh)], 0.0).astype(BF16)
        fr = jnp.broadcast_to(frow_ref[0, hh, :, pl.ds(q0, LANES)][:, 0:1], (1, LANES))
        f1, f2, f3 = _split3(fr)
        base = (hh % 2) * FOX_BIAS_LANES
        qx = jnp.where(jnp.logical_and(lane >= base, lane < base + 3), 1.0, 0.0).astype(BF16)
        qx = jnp.where(lane == base + 3, f1, qx)
        qx = jnp.where(lane == base + 4, f2, qx)
        qx = jnp.where(lane == base + 5, f3, qx)
        q_aug.append(jnp.concatenate([qm, jnp.broadcast_to(qx, (tq, LANES))], axis=1))

    m_ref[...] = jnp.full(m_ref.shape, NEG, F32)
    acc_ref[...] = jnp.zeros_like(acc_ref)

    bufs = {"a": (sa_ref, ma_ref), "b": (sb_ref, mb_ref)}

    half = tk // 2
    lo, hi = slice(0, half), slice(half, tk)

    def k_aug(k0, n, hh):
        return jnp.concatenate([k_ref[pl.ds(k0, n), pair_lanes(hh)],
                                kx_ref[pl.ds(k0, n), pair_lanes(hh)]], axis=1)

    def causal(st):
        key = lax.broadcasted_iota(jnp.int32, st.shape, 0)
        qry = lax.broadcasted_iota(jnp.int32, st.shape, 1)
        return jnp.where(key <= qry, st, NEG)

    def produce(which, blk, hh, diagonal):
        buf, mx = bufs[which]
        k0 = pl.multiple_of(blk * tk, tk)
        if not diagonal:
            st = _dot_nt(k_aug(k0, tk, hh), q_aug[hh])
            buf[hh] = st
            mx[hh] = jnp.max(st, axis=0, keepdims=True)
            return
        st_lo = _dot_nt(k_aug(k0, half, hh), q_aug[hh])
        st_lo = jnp.concatenate([causal(st_lo[:, lo]), st_lo[:, hi]], axis=1)
        st_hi = causal(_dot_nt(k_aug(k0 + half, half, hh), q_aug[hh][hi]))
        buf[hh, lo, :] = st_lo
        buf[hh, hi, hi] = st_hi
        max_lo = jnp.max(st_lo, axis=0, keepdims=True)
        max_hi = jnp.max(st_hi, axis=0, keepdims=True)
        mx[hh] = jnp.concatenate([max_lo[:, lo], jnp.maximum(max_lo[:, hi], max_hi)], axis=1)

    def consume(which, blk, hh, diagonal):
        buf, mx = bufs[which]
        k0 = pl.multiple_of(blk * tk, tk)
        m_old = m_ref[hh]
        m_new = jnp.maximum(m_old, mx[hh])
        alpha = jnp.exp2(m_old - m_new)
        vrows = slice(hh * FOX_VROWS, (hh + 1) * FOX_VROWS)
        if not diagonal:
            p = jnp.exp2(buf[hh] - m_new).astype(BF16)
            acc_ref[hh] = alpha * acc_ref[hh] + _dot(vt_ref[vrows, pl.ds(k0, tk)], p)
        else:
            p_lo = jnp.exp2(buf[hh, lo, :] - m_new).astype(BF16)
            p_hi = jnp.exp2(buf[hh, hi, hi] - m_new[:, hi]).astype(BF16)
            pv_lo = _dot(vt_ref[vrows, pl.ds(k0, half)], p_lo)
            pv_hi = _dot(vt_ref[vrows, pl.ds(k0 + half, half)], p_hi)
            acc_ref[hh] = alpha * acc_ref[hh] + jnp.concatenate(
                [pv_lo[:, lo], pv_lo[:, hi] + pv_hi], axis=1)
        m_ref[hh] = m_new

    def stage(cur, cur_blk, nxt=None, nxt_blk=None, nxt_diagonal=False, cur_diagonal=False):
        for hh in heads:
            if nxt is not None:
                produce(nxt, nxt_blk, hh, nxt_diagonal)
            if hh > 0:
                consume(cur, cur_blk, hh - 1, cur_diagonal)
        consume(cur, cur_blk, heads[-1], cur_diagonal)

    @pl.when(i == 0)
    def _():
        for hh in heads:
            produce("a", 0, hh, True)
        stage("a", 0, cur_diagonal=True)

    @pl.when(i > 0)
    def _():
        for hh in heads:
            produce("a", 0, hh, False)

    def pair(t, carry):
        b0 = 2 * t
        stage("a", b0, "b", b0 + 1)
        stage("b", b0 + 1, "a", b0 + 2)
        return carry

    n_pairs = lax.shift_right_logical(jnp.maximum(i - 1, 0), 1)
    lax.fori_loop(0, n_pairs, pair, 0)

    @pl.when((i & 1) == 1)
    def _():
        stage("a", i - 1, "b", i, True)
        stage("b", i, cur_diagonal=True)

    @pl.when(jnp.logical_and((i & 1) == 0, i > 0))
    def _():
        stage("a", i - 2, "b", i - 1)
        stage("b", i - 1, "a", i, True)
        stage("a", i, cur_diagonal=True)

    for pr in range(FOX_PAIRS):
        outs = []
        for hh in (2 * pr, 2 * pr + 1):
            acc = acc_ref[hh]
            outs.append(acc[:FOX_DH] * (1.0 / acc[FOX_DH:FOX_DH + 1]))
        o_ref[:, pr * LANES:(pr + 1) * LANES] = jnp.concatenate(outs, axis=0).T.astype(BF16)


def _fox_call(proj, kx, vt, frow, batch, seq):
    t = proj.shape[0]
    tq = FOX_TQ
    nq = seq // tq
    width = FOX_PAIRS * LANES
    groups = FOX_WIDTH // width
    nh = 2 * FOX_PAIRS
    return pl.pallas_call(
        _fox_kernel,
        grid=(batch, groups, nq),
        in_specs=[
            pl.BlockSpec((tq, width), lambda b, hp, i: (b * nq + i, COL_QF // width + hp)),
            pl.BlockSpec((seq, width), lambda b, hp, i: (b, COL_KF // width + hp),
                         pipeline_mode=pl.Buffered(1)),
            pl.BlockSpec((seq, width), lambda b, hp, i: (b, hp), pipeline_mode=pl.Buffered(1)),
            pl.BlockSpec((nh * FOX_VROWS, seq), lambda b, hp, i: (hp, b),
                         pipeline_mode=pl.Buffered(1)),
            pl.BlockSpec((1, nh, 1, seq), lambda b, hp, i: (b, hp, 0, 0)),
        ],
        out_specs=pl.BlockSpec((tq, width), lambda b, hp, i: (b * nq + i, hp)),
        out_shape=jax.ShapeDtypeStruct((t, FOX_WIDTH), BF16),
        scratch_shapes=[
            pltpu.VMEM((nh, 1, tq), F32),
            pltpu.VMEM((nh, FOX_VROWS, tq), F32),
            pltpu.VMEM((nh, FOX_TK, tq), F32),
            pltpu.VMEM((nh, FOX_TK, tq), F32),
            pltpu.VMEM((nh, 1, tq), F32),
            pltpu.VMEM((nh, 1, tq), F32),
        ],
        compiler_params=_params("parallel", "parallel", "arbitrary"),
        name="fox",
    )(proj, proj, kx, vt, frow)


POST_TM = 512
POST_SUB = 2


def _post_kernel(og_ref, of_ref, x_ref, mod_ref, pg_ref, mg_ref, wo_ref, wrh_ref, wrl_ref, br_ref,
                 x1_ref, h2_ref, idx_ref, rank_ref, gate_ref, cnt_ref, car_ref):
    @pl.when(pl.program_id(0) == 0)
    def _():
        car_ref[...] = jnp.zeros_like(car_ref)

    ys = []
    for s in range(POST_SUB):
        rows = slice(s * POST_TM, (s + 1) * POST_TM)
        ys.append(_dot(og_ref[rows, :], wo_ref[:GLA_WIDTH, :])
                  + _dot(of_ref[rows, :], wo_ref[GLA_WIDTH:, :]))
    h2s = [_post_norm(s, ys[s], x_ref, mod_ref, pg_ref, mg_ref, x1_ref, h2_ref)
           for s in range(POST_SUB)]
    logits = jnp.concatenate([_post_logits(h2, wrh_ref, wrl_ref, br_ref) for h2 in h2s], axis=0)
    _post_topk(logits, idx_ref, rank_ref, gate_ref, cnt_ref, car_ref)


def _post_norm(s, y, x_ref, mod_ref, pg_ref, mg_ref, x1_ref, h2_ref):
    tm = POST_TM
    rows = slice(s * tm, (s + 1) * tm)
    x1 = x_ref[rows, :] + mod_ref[0, 2:3, :] * (_rms(y) * pg_ref[...])
    x1_ref[rows, :] = x1
    h2 = _rms(x1) * mg_ref[...]
    h2 = h2 * (1.0 + mod_ref[0, 4:5, :]) + mod_ref[0, 3:4, :]
    for j in range(ROW_TILES):
        h2_ref[pl.ds(s * tm * ROW_TILES + j, tm, stride=ROW_TILES), :] = h2[:, j * LANES:(j + 1) * LANES]

    return _split2(h2)


def _post_logits(h2_split, wrh_ref, wrl_ref, br_ref):
    hh, hl = h2_split
    return _dot(hh, wrh_ref[...]) + (_dot(hh, wrl_ref[...]) + _dot(hl, wrh_ref[...])) + br_ref[...]


def _post_topk(logits, idx_ref, rank_ref, gate_ref, cnt_ref, car_ref):
    tm = POST_TM
    n = logits.shape[0]
    lane = lax.broadcasted_iota(jnp.int32, (n, LANES), 1).astype(F32)
    work = jnp.where(lane < N_EXPERTS, logits, -jnp.inf)
    vals, idxs = [], []
    for _ in range(TOP_K):
        mx = jnp.max(work, axis=-1, keepdims=True)
        ix = jnp.min(jnp.where(work == mx, lane, float(LANES)), axis=-1, keepdims=True)
        vals.append(mx)
        idxs.append(ix)
        work = jnp.where(lane == ix, -jnp.inf, work)
    es = [jnp.exp(v - vals[0]) for v in vals]
    den = es[0] + es[1] + es[2] + es[3]

    onehot = jnp.zeros((n, LANES), F32)
    for ix in idxs:
        onehot = onehot + jnp.where(lane == ix, 1.0, 0.0)
    row = lax.broadcasted_iota(jnp.int32, (tm, tm), 0)
    col = lax.broadcasted_iota(jnp.int32, (tm, tm), 1)
    strict_b = jnp.where(col < row, 1.0, 0.0).astype(BF16)
    carry = car_ref[...]
    prefixes = []
    for s in range(n // tm):
        part = onehot[s * tm:(s + 1) * tm]
        prefixes.append(_dot(strict_b, part.astype(BF16)) + carry)
        carry = carry + jnp.sum(part, axis=0, keepdims=True)
    prefix = jnp.concatenate(prefixes, axis=0)
    car_ref[...] = carry
    cnt_ref[...] = carry

    idx_o = jnp.zeros((n, LANES), F32)
    rank_o = jnp.zeros((n, LANES), F32)
    gate_o = jnp.zeros((n, LANES), F32)
    for kk in range(TOP_K):
        rk = jnp.sum(jnp.where(lane == idxs[kk], prefix, 0.0), axis=-1, keepdims=True)
        sel = lane == float(kk)
        idx_o = jnp.where(sel, idxs[kk], idx_o)
        rank_o = jnp.where(sel, rk, rank_o)
        gate_o = jnp.where(sel, es[kk] / den, gate_o)
    idx_ref[...] = idx_o.astype(jnp.int32)
    rank_ref[...] = rank_o.astype(jnp.int32)
    gate_ref[...] = gate_o


def _post_call(o_gla, o_fox, x2, mod3, post_gain, mlp_gain, w_o, wr_hi, wr_lo, br_pad, seq):
    t, d = x2.shape
    tm = POST_TM * POST_SUB
    per_b = seq // tm
    return pl.pallas_call(
        _post_kernel,
        grid=(t // tm,),
        in_specs=[
            pl.BlockSpec((tm, GLA_WIDTH), lambda i: (i, 0)),
            pl.BlockSpec((tm, FOX_WIDTH), lambda i: (i, 0)),
            pl.BlockSpec((tm, d), lambda i: (i, 0)),
            pl.BlockSpec((1, N_MOD, d), lambda i: (i // per_b, 0, 0)),
            pl.BlockSpec((1, d), lambda i: (0, 0)),
            pl.BlockSpec((1, d), lambda i: (0, 0)),
            pl.BlockSpec((GLA_WIDTH + FOX_WIDTH, d), lambda i: (0, 0)),
            pl.BlockSpec((d, LANES), lambda i: (0, 0)),
            pl.BlockSpec((d, LANES), lambda i: (0, 0)),
            pl.BlockSpec((1, LANES), lambda i: (0, 0)),
        ],
        out_specs=[
            pl.BlockSpec((tm, d), lambda i: (i, 0)),
            pl.BlockSpec((tm * ROW_TILES, LANES), lambda i: (i, 0)),
            pl.BlockSpec((tm, LANES), lambda i: (i, 0)),
            pl.BlockSpec((tm, LANES), lambda i: (i, 0)),
            pl.BlockSpec((tm, LANES), lambda i: (i, 0)),
            pl.BlockSpec((1, LANES), lambda i: (0, 0)),
        ],
        out_shape=[
            jax.ShapeDtypeStruct((t, d), F32),
            jax.ShapeDtypeStruct((t * ROW_TILES, LANES), F32),
            jax.ShapeDtypeStruct((t, LANES), jnp.int32),
            jax.ShapeDtypeStruct((t, LANES), jnp.int32),
            jax.ShapeDtypeStruct((t, LANES), F32),
            jax.ShapeDtypeStruct((1, LANES), F32),
        ],
        scratch_shapes=[pltpu.VMEM((1, LANES), F32)],
        compiler_params=_params("arbitrary"),
        name="post",
    )(o_gla, o_fox, x2, mod3, post_gain, mlp_gain, w_o, wr_hi, wr_lo, br_pad)


DISP_TM = 512
DISP_SLOTS = 3
ISSUE_UNROLL = 8


def _row_copy(src_ref, src_row, dst_ref, dst_row, sem):
    return pltpu.make_async_copy(
        src_ref.at[pl.ds(pl.multiple_of(src_row * ROW_TILES, ROW_TILES), ROW_TILES)],
        dst_ref.at[pl.ds(pl.multiple_of(dst_row * ROW_TILES, ROW_TILES), ROW_TILES)],
        sem)


def _dispatch_kernel(pend_ref, pos_ref, h_hbm, xs_ref, zero_ref, hbuf_ref, sem, in_sem, out_sem):
    tm = DISP_TM
    rows = tm * ROW_TILES
    zrows = MOE_TM * ROW_TILES
    i = pl.program_id(0)
    last = pl.num_programs(0) - 1
    slot = lax.rem(i, DISP_SLOTS)
    nxt = lax.rem(i + 1, DISP_SLOTS)

    def load(step, s):
        return pltpu.make_async_copy(
            h_hbm.at[pl.ds(pl.multiple_of(step * rows, rows), rows)], hbuf_ref.at[s], in_sem.at[s])

    def drain(s):
        for _ in range(TOP_K):
            pltpu.make_async_copy(hbuf_ref.at[s], xs_ref.at[pl.ds(0, rows)], out_sem.at[s]).wait()

    @pl.when(i == 0)
    def _():
        load(0, 0).start()
        zero_ref[...] = jnp.zeros_like(zero_ref)

        def last_tile(e):
            start = pl.multiple_of((pend_ref[e] - MOE_TM) * ROW_TILES, zrows)
            return pltpu.make_async_copy(zero_ref, xs_ref.at[pl.ds(start, zrows)], sem)

        def nonempty(e):
            return pend_ref[e] > jnp.where(e == 0, 0, pend_ref[jnp.maximum(e - 1, 0)])

        def clear(e, carry):
            @pl.when(nonempty(e))
            def _():
                last_tile(e).start()
            return carry

        def clear_wait(e, carry):
            @pl.when(nonempty(e))
            def _():
                last_tile(e).wait()
            return carry

        lax.fori_loop(0, N_EXPERTS, clear, 0)
        lax.fori_loop(0, N_EXPERTS, clear_wait, 0)

        def tail_tile(j):
            return pltpu.make_async_copy(
                zero_ref, xs_ref.at[pl.ds(pl.multiple_of(j * zrows, zrows), zrows)], sem)

        def tail(j, carry):
            tail_tile(j).start()
            return carry

        def tail_wait(j, carry):
            tail_tile(j).wait()
            return carry

        first_unused = lax.div(pend_ref[N_EXPERTS - 1], MOE_TM)
        n_tiles = xs_ref.shape[0] // zrows
        lax.fori_loop(first_unused, n_tiles, tail, 0)
        lax.fori_loop(first_unused, n_tiles, tail_wait, 0)

    @pl.when(i >= DISP_SLOTS - 1)
    def _():
        drain(nxt)

    @pl.when(i < last)
    def _():
        load(i + 1, nxt).start()

    load(i, slot).wait()

    def issue(g, carry):
        for u in range(ISSUE_UNROLL):
            t = g * ISSUE_UNROLL + u
            for kk in range(TOP_K):
                _row_copy(hbuf_ref.at[slot], t, xs_ref, pos_ref[0, 0, t * TOP_K + kk],
                          out_sem.at[slot]).start(priority=kk % 2)
        return carry

    lax.fori_loop(0, tm // ISSUE_UNROLL, issue, 0)

    @pl.when(i == last)
    def _():
        for back in range(DISP_SLOTS - 1):
            @pl.when(i >= back)
            def _():
                drain(lax.rem(i - back + DISP_SLOTS, DISP_SLOTS))


def _dispatch_call(pend, pos3, h2, n_tiles):
    tm = DISP_TM
    n_steps = pos3.shape[0]
    grid_spec = pltpu.PrefetchScalarGridSpec(
        num_scalar_prefetch=1,
        grid=(n_steps,),
        in_specs=[
            pl.BlockSpec((1, 1, tm * TOP_K), lambda i, pend: (i, 0, 0), memory_space=pltpu.SMEM),
            pl.BlockSpec(memory_space=pl.ANY),
        ],
        out_specs=pl.BlockSpec(memory_space=pl.ANY),
        scratch_shapes=[
            pltpu.VMEM((MOE_TM * ROW_TILES, LANES), F32),
            pltpu.VMEM((DISP_SLOTS, tm * ROW_TILES, LANES), F32),
            pltpu.SemaphoreType.DMA,
            pltpu.SemaphoreType.DMA((DISP_SLOTS,)),
            pltpu.SemaphoreType.DMA((DISP_SLOTS,)),
        ],
    )
    return pl.pallas_call(
        _dispatch_kernel,
        grid_spec=grid_spec,
        out_shape=jax.ShapeDtypeStruct((n_tiles * MOE_TM * ROW_TILES, LANES), F32),
        compiler_params=_params("arbitrary"),
        name="dispatch",
    )(pend, pos3, h2)


MOE_TM = 512


def _experts_kernel(te_ref, nu_ref, x_ref, win_ref, bin_ref, wout_ref, bout_ref, y_ref,
                    wib_ref, wob_ref):
    tm = MOE_TM
    i = pl.program_id(0)
    used = i < nu_ref[0]

    @pl.when(jnp.logical_not(used))
    def _():
        y_ref[...] = jnp.zeros_like(y_ref)

    fresh = jnp.logical_or(i == 0, te_ref[i] != te_ref[jnp.maximum(i - 1, 0)])

    @pl.when(jnp.logical_and(used, fresh))
    def _():
        wib_ref[...] = win_ref[0].astype(BF16)
        wob_ref[...] = wout_ref[0].astype(BF16)

    @pl.when(used)
    def _():
        x = jnp.concatenate([x_ref[pl.ds(j, tm, stride=ROW_TILES), :] for j in range(ROW_TILES)],
                            axis=1).astype(BF16)
        u = _dot(x, wib_ref[...]) + bin_ref[0]
        glu = jnp.minimum(u[:, :D_FF], SWIGLU_LIMIT)
        lin = jnp.clip(u[:, D_FF:], -SWIGLU_LIMIT, SWIGLU_LIMIT)
        a = glu * jax.nn.sigmoid(SWIGLU_ALPHA * glu) * (lin + 1.0)
        y = _dot(a.astype(BF16), wob_ref[...]) + bout_ref[0]
        for j in range(ROW_TILES):
            y_ref[pl.ds(j, tm, stride=ROW_TILES), :] = y[:, j * LANES:(j + 1) * LANES]


def _experts_call(tile_expert, n_used, xs, w_in, b_in, w_out, b_out):
    tm = MOE_TM
    n_tiles = tile_expert.shape[0]
    d = D_MODEL

    def tile(i, te, nu):
        return (jnp.minimum(i, nu[0] - 1), 0)

    grid_spec = pltpu.PrefetchScalarGridSpec(
        num_scalar_prefetch=2,
        grid=(n_tiles,),
        in_specs=[
            pl.BlockSpec((tm * ROW_TILES, LANES), tile),
            pl.BlockSpec((1, d, 2 * D_FF), lambda i, te, nu: (te[i], 0, 0)),
            pl.BlockSpec((1, 1, 2 * D_FF), lambda i, te, nu: (te[i], 0, 0)),
            pl.BlockSpec((1, D_FF, d), lambda i, te, nu: (te[i], 0, 0)),
            pl.BlockSpec((1, 1, d), lambda i, te, nu: (te[i], 0, 0)),
        ],
        out_specs=pl.BlockSpec((tm * ROW_TILES, LANES), lambda i, te, nu: (i, 0)),
        scratch_shapes=[
            pltpu.VMEM((d, 2 * D_FF), BF16),
            pltpu.VMEM((D_FF, d), BF16),
        ],
    )
    return pl.pallas_call(
        _experts_kernel,
        grid_spec=grid_spec,
        out_shape=jax.ShapeDtypeStruct(xs.shape, F32),
        compiler_params=_params("arbitrary"),
        name="experts",
    )(tile_expert, n_used, xs, w_in, b_in, w_out, b_out)


COMB_TM = 256


def _combine_kernel(pos_ref, posn_ref, y_ref, gate_ref, x1_ref, mod_ref, pg_ref, o_ref, buf_ref, sem):
    tm = COMB_TM
    i = pl.program_id(0)
    slot = i & 1

    def gather(p_ref, s):
        def issue(g, carry):
            for u in range(ISSUE_UNROLL):
                t = g * ISSUE_UNROLL + u
                for kk in range(TOP_K):
                    _row_copy(y_ref, p_ref[0, 0, t * TOP_K + kk], buf_ref.at[s, kk], t,
                              sem.at[s]).start(priority=kk % 2)
            return carry

        lax.fori_loop(0, tm // ISSUE_UNROLL, issue, 0)

    @pl.when(i == 0)
    def _():
        gather(pos_ref, 0)

    @pl.when(i + 1 < pl.num_programs(0))
    def _():
        gather(posn_ref, 1 - slot)

    for kk in range(TOP_K):
        pltpu.make_async_copy(y_ref.at[pl.ds(0, tm * ROW_TILES)], buf_ref.at[slot, kk],
                              sem.at[slot]).wait()

    gates = gate_ref[...]
    acc = None
    for kk in range(TOP_K):
        rows = jnp.concatenate(
            [buf_ref[slot, kk, pl.ds(j, tm, stride=ROW_TILES), :] for j in range(ROW_TILES)], axis=1)
        term = rows * gates[:, kk:kk + 1]
        acc = term if acc is None else acc + term
    o_ref[...] = x1_ref[...] + mod_ref[0, 5:6, :] * (_rms(acc) * pg_ref[...])


def _combine_call(pos3, y, gates, x1, mod3, post_gain, seq):
    t, d = x1.shape
    tm = COMB_TM
    per_b = seq // tm
    n_steps = t // tm
    return pl.pallas_call(
        _combine_kernel,
        grid=(n_steps,),
        in_specs=[
            pl.BlockSpec((1, 1, tm * TOP_K), lambda i: (i, 0, 0), memory_space=pltpu.SMEM),
            pl.BlockSpec((1, 1, tm * TOP_K), lambda i: (jnp.minimum(i + 1, n_steps - 1), 0, 0),
                         memory_space=pltpu.SMEM),
            pl.BlockSpec(memory_space=pl.ANY),
            pl.BlockSpec((tm, LANES), lambda i: (i, 0)),
            pl.BlockSpec((tm, d), lambda i: (i, 0)),
            pl.BlockSpec((1, N_MOD, d), lambda i: (i // per_b, 0, 0)),
            pl.BlockSpec((1, d), lambda i: (0, 0)),
        ],
        out_specs=pl.BlockSpec((tm, d), lambda i: (i, 0)),
        out_shape=jax.ShapeDtypeStruct((t, d), F32),
        scratch_shapes=[
            pltpu.VMEM((2, TOP_K, tm * ROW_TILES, LANES), F32),
            pltpu.SemaphoreType.DMA((2,)),
        ],
        compiler_params=_params("arbitrary"),
        name="combine",
    )(pos3, pos3, y, gates, x1, mod3, post_gain)


def _pad_cols(w, n):
    return jnp.pad(w, ((0, 0), (0, n - w.shape[1])))


def kernel(x, c, w_ada, b_ada, attn_pre_gain, attn_post_gain, w_in, w_gk2, b_gk, gla_norm_gain, b_f,
           w_o, mlp_pre_gain, mlp_post_gain, w_router, b_router, w_e_in, b_e_in, w_e_out, b_e_out):
    batch, seq, d = x.shape
    t = batch * seq
    x2 = x.reshape(t, d)

    o = 0
    cols = {}
    for name, width in (("qg", GLA_QK), ("kg", GLA_QK), ("vg", GLA_WIDTH), ("gg", GLA_WIDTH),
                        ("zg", GLA_LOWRANK), ("qf", FOX_WIDTH), ("kf", FOX_WIDTH), ("vf", FOX_WIDTH),
                        ("ff", FOX_HEADS)):
        cols[name] = w_in[:, o:o + width]
        o += width
    gla_cols = 2 * GLA_QK + 2 * GLA_WIDTH
    fox_start = gla_cols + GLA_LOWRANK
    w_main = jnp.concatenate([w_in[:, :gla_cols], w_in[:, fox_start:fox_start + 2 * FOX_WIDTH]],
                             axis=1).astype(BF16)
    w_vt = cols["vf"].T.astype(BF16)
    w_small = _pad_cols(jnp.concatenate([cols["ff"], cols["zg"]], axis=1), LANES)
    ws_hi, ws_lo = _split2(w_small)
    wg = jnp.zeros((LANES, GLA_QK), F32).at[SMALL_ZG:SMALL_ZG + GLA_LOWRANK].set(w_gk2)
    wg_hi, wg_lo = _split2(wg)
    bf_pad = jnp.zeros((1, LANES), F32).at[0, SMALL_FF:SMALL_FF + FOX_HEADS].set(b_f)
    wr_hi, wr_lo = _split2(_pad_cols(w_router, LANES))
    br_pad = _pad_cols(b_router.reshape(1, N_EXPERTS), LANES)

    c_pad = jnp.pad(c, ((0, SUBLANES - batch % SUBLANES if batch % SUBLANES else 0), (0, 0)))
    mod = _mod_call(c_pad, w_ada, b_ada.reshape(1, -1))
    mod3 = mod[:batch].reshape(batch, N_MOD, d)

    proj, vt, small = _inproj_call(x2, mod3, attn_pre_gain.reshape(1, d), w_main, w_vt, ws_hi, ws_lo,
                                   seq)
    o_gla, kx, frow = _gla_call(proj, small, wg_hi, wg_lo, b_gk.reshape(1, -1),
                                gla_norm_gain.reshape(1, -1), bf_pad, batch, seq)
    o_fox = _fox_call(proj, kx, vt, frow, batch, seq)

    x1, h2, idx_w, rank_w, gate_w, cnt = _post_call(
        o_gla, o_fox, x2, mod3, attn_post_gain.reshape(1, d), mlp_pre_gain.reshape(1, d),
        w_o.astype(BF16), wr_hi, wr_lo, br_pad, seq)

    tm = MOE_TM
    n_tiles = (t * TOP_K) // tm + N_EXPERTS
    counts = cnt[0, :N_EXPERTS].astype(jnp.int32)
    padded = ((counts + tm - 1) // tm) * tm
    pend = jnp.cumsum(padded)
    pstart = pend - padded
    experts = jnp.arange(N_EXPERTS, dtype=jnp.int32)
    first_slot = jnp.sum(jnp.where(idx_w[:, :TOP_K, None] == experts, pstart.astype(jnp.int32), 0),
                         axis=-1)
    pos = first_slot + rank_w[:, :TOP_K]
    n_used = (pend[-1] // tm).astype(jnp.int32).reshape(1)
    tile_start = jnp.arange(n_tiles, dtype=jnp.int32) * tm
    tile_expert = jnp.minimum(
        jnp.sum((pend[None, :] <= tile_start[:, None]).astype(jnp.int32), axis=1), N_EXPERTS - 1)

    pos_d = pos.reshape(t // DISP_TM, 1, DISP_TM * TOP_K)
    xs = _dispatch_call(pend.astype(jnp.int32), pos_d, h2, n_tiles)
    y = _experts_call(tile_expert, n_used, xs, w_e_in, b_e_in.reshape(N_EXPERTS, 1, -1), w_e_out,
                      b_e_out.reshape(N_EXPERTS, 1, -1))
    pos_c = pos.reshape(t // COMB_TM, 1, COMB_TM * TOP_K)
    out = _combine_call(pos_c, y, gate_w, x1, mod3, mlp_post_gain.reshape(1, d), seq)
    return out.reshape(batch, seq, d)
```
